```python
import math
import jax, jax.numpy as jnp
from jax import lax
import numpy as np

D_MODEL = 1024
BATCH = 4
SEQ = 4096
DEPTH = 1
DEC_BATCH = 128
DEC_SEQ = 8
PAST_LEN = 2048
PAGE_SIZE = 128

HEAD_DIM = 64
N_DIFF_HEADS = 4
N_NSA_HEADS = 8
N_NSA_KV_HEADS = 2
NSA_GROUP = N_NSA_HEADS // N_NSA_KV_HEADS
CMP_BLOCK = 32
CMP_STRIDE = 16
CMP_HIDDEN = 128
SEL_BLOCK = 64
SEL_TOP_N = 16
WINDOW = 512
D_FF = 4 * D_MODEL
ROPE_THETA = 10000.0
NORM_EPS = 1e-6
DIFF_WIDTH = N_DIFF_HEADS * 2 * HEAD_DIM
NSA_WIDTH = N_NSA_HEADS * HEAD_DIM
MIX_WIDTH = DIFF_WIDTH + NSA_WIDTH
SPLIT_SIZES = (DIFF_WIDTH, DIFF_WIDTH, DIFF_WIDTH, NSA_WIDTH,
               6 * N_NSA_KV_HEADS * HEAD_DIM, 3 * N_NSA_HEADS)
D_IN = sum(SPLIT_SIZES)
NEG = -1e9
BIG = 1e9

kernel_name = 'hymba_diffattn_nsa_decode_step'


def rms_norm(x, g):
    xf = x.astype(jnp.float32)
    y = xf * lax.rsqrt(jnp.mean(xf * xf, axis=-1, keepdims=True) + NORM_EPS)
    return (y * g.astype(jnp.float32)).astype(x.dtype)


def rope(x, pos):
    half = x.shape[-1] // 2
    inv = ROPE_THETA ** (-jnp.arange(half, dtype=jnp.float32) / half)
    ang = pos.astype(jnp.float32)[:, None] * inv[None, :]
    shp = (1, pos.shape[0]) + (1,) * (x.ndim - 3) + (half,)
    cos, sin = jnp.cos(ang).reshape(shp), jnp.sin(ang).reshape(shp)
    x1 = x[..., :half].astype(jnp.float32)
    x2 = x[..., half:].astype(jnp.float32)
    return jnp.concatenate([x1 * cos - x2 * sin, x1 * sin + x2 * cos], axis=-1).astype(x.dtype)


def masked_softmax(s, mask):
    s = jnp.where(mask, s.astype(jnp.float32), NEG)
    m = jnp.max(s, axis=-1, keepdims=True)
    p = jnp.where(mask, jnp.exp(s - m), 0.0)
    return p / jnp.maximum(jnp.sum(p, axis=-1, keepdims=True), 1e-30)


def to_qblocks(a, qb):
    b, s = a.shape[:2]
    return a.reshape((b, s // qb, qb) + a.shape[2:]).swapaxes(0, 1)


def from_qblocks(o):
    nb, b, qb = o.shape[:3]
    return o.swapaxes(0, 1).reshape((b, nb * qb) + o.shape[3:])


def diff_attention(q, k, v, lam, lam_init, subln_g, q0):
    b, sq = q.shape[:2]
    t = k.shape[1]
    qb = math.gcd(sq, 128)
    kpos = jnp.arange(t)
    scale = HEAD_DIM ** -0.5

    def block(args):
        qblk, j = args
        tpos = q0 + j * qb + jnp.arange(qb)
        s = jnp.einsum('bqhcd,bkhcd->bhcqk', qblk, k) * scale
        p = masked_softmax(s, kpos[None, :] <= tpos[:, None])
        a = p[:, :, 0] - lam * p[:, :, 1]
        return jnp.einsum('bhqk,bkhe->bqhe', a.astype(v.dtype), v)

    o = from_qblocks(lax.map(block, (to_qblocks(q, qb), jnp.arange(sq // qb))))
    o = rms_norm(o, subln_g) * (1.0 - lam_init)
    return o.reshape(b, sq, DIFF_WIDTH)


def compress(rows, pos_emb, w1, w2):
    t = rows.shape[1]
    nc = (t - CMP_BLOCK) // CMP_STRIDE + 1
    idx = jnp.arange(nc)[:, None] * CMP_STRIDE + jnp.arange(CMP_BLOCK)[None, :]
    blk = rows[:, idx] + pos_emb[None, None, :, None, :]
    h = jax.nn.gelu(jnp.einsum('bnlhd,ldf->bnhf', blk, w1.reshape(CMP_BLOCK, HEAD_DIM, CMP_HIDDEN)))
    return jnp.einsum('bnhf,fd->bnhd', h, w2)


def nsa_attention(q, gates, k_cmp, v_cmp, k_slc, v_slc, k_win, v_win, q0):
    b, sq = q.shape[:2]
    t = k_slc.shape[1]
    nc = k_cmp.shape[1]
    ns = -(-t // SEL_BLOCK)
    n_sel = min(SEL_TOP_N, ns)
    scale = HEAD_DIM ** -0.5
    q = q.reshape(b, sq, N_NSA_KV_HEADS, NSA_GROUP, HEAD_DIM)
    gates = gates.reshape(b, sq, N_NSA_KV_HEADS, NSA_GROUP, 3)
    cs = np.arange(nc) * CMP_STRIDE
    ss = np.arange(ns) * SEL_BLOCK
    overlap = jnp.asarray(((cs[:, None] < ss[None, :] + SEL_BLOCK)
                           & (cs[:, None] + CMP_BLOCK > ss[None, :])).astype(np.float32))
    cmp_end = jnp.arange(nc) * CMP_STRIDE + CMP_BLOCK - 1
    pad = ns * SEL_BLOCK - t

    def sel_blocks(a):
        a = jnp.pad(a, ((0, 0), (0, pad), (0, 0), (0, 0)))
        return a.reshape(b, ns, SEL_BLOCK, N_NSA_KV_HEADS, HEAD_DIM).transpose(0, 3, 1, 2, 4)

    ks_b, vs_b = sel_blocks(k_slc), sel_blocks(v_slc)
    bi = jnp.arange(b)[:, None, None, None]
    hi = jnp.arange(N_NSA_KV_HEADS)[None, :, None, None]
    blk_ids = jnp.arange(ns)
    qb = math.gcd(sq, max(1, 256 // b))

    def block(args):
        qblk, gblk, j = args
        tpos = q0 + j * qb + jnp.arange(qb)
        s_c = jnp.einsum('bqhgd,bnhd->bhgqn', qblk, k_cmp) * scale
        p_c = masked_softmax(s_c, cmp_end[None, :] <= tpos[:, None])
        o_c = jnp.einsum('bhgqn,bnhd->bqhgd', p_c.astype(v_cmp.dtype), v_cmp)
        p_sel = jnp.einsum('bhgqn,nm->bhqm', p_c, overlap)
        cur = tpos // SEL_BLOCK
        future = blk_ids[None, :] > cur[:, None]
        forced = (blk_ids[None, :] == 0) | (blk_ids[None, :] == cur[:, None]) | (blk_ids[None, :] == cur[:, None] - 1)
        score = jnp.where(future, NEG, jnp.where(forced, BIG, p_sel))
        _, idx = lax.top_k(score, n_sel)
        kg = ks_b[bi, hi, idx]
        vg = vs_b[bi, hi, idx]
        kpos = idx[..., None] * SEL_BLOCK + jnp.arange(SEL_BLOCK)
        smask = (kpos <= tpos[None, None, :, None, None]).reshape(b, N_NSA_KV_HEADS, 1, qb, n_sel * SEL_BLOCK)
        s_s = jnp.einsum('bqhgd,bhqnld->bhgqnl', qblk, kg) * scale
        p_s = masked_softmax(s_s.reshape(b, N_NSA_KV_HEADS, NSA_GROUP, qb, n_sel * SEL_BLOCK), smask)
        p_s = p_s.reshape(b, N_NSA_KV_HEADS, NSA_GROUP, qb, n_sel, SEL_BLOCK)
        o_s = jnp.einsum('bhgqnl,bhqnld->bqhgd', p_s.astype(vg.dtype), vg)
        kw = lax.dynamic_slice_in_dim(k_win, j * qb, WINDOW + qb, axis=1)
        vw = lax.dynamic_slice_in_dim(v_win, j * qb, WINDOW + qb, axis=1)
        wpos = q0 - WINDOW + j * qb + jnp.arange(WINDOW + qb)
        dist = tpos[:, None] - wpos[None, :]
        wmask = (wpos[None, :] >= 0) & (dist >= 0) & (dist < WINDOW)
        s_w = jnp.einsum('bqhgd,bkhd->bhgqk', qblk, kw) * scale
        p_w = masked_softmax(s_w, wmask)
        o_w = jnp.einsum('bhgqk,bkhd->bqhgd', p_w.astype(vw.dtype), vw)
        return gblk[..., 0:1] * o_c + gblk[..., 1:2] * o_s + gblk[..., 2:3] * o_w

    o = lax.map(block, (to_qblocks(q, qb), to_qblocks(gates, qb), jnp.arange(sq // qb)))
    return from_qblocks(o).reshape(b, sq, NSA_WIDTH)


def decoder_layer(x, past_diff, past_nsa, win_buf, layer, w_in, w_out, w_up, w_down,
                  g_pre_mix, g_post_mix, g_pre_mlp, g_post_mlp,
                  lam_q1, lam_k1, lam_q2, lam_k2, diff_subln, cmp_pos, cmp_w1, cmp_w2):
    b, s, _ = x.shape
    q0 = past_diff.shape[1]
    pos = q0 + jnp.arange(s)
    h = rms_norm(x, g_pre_mix)
    proj = h @ w_in
    offs = [int(o) for o in np.cumsum(SPLIT_SIZES)[:-1]]
    q_d, k_d, v_d, q_n, kv_n, g_n = jnp.split(proj, offs, axis=-1)

    q_d = rope(q_d.reshape(b, s, N_DIFF_HEADS, 2, HEAD_DIM), pos)
    k_d = rope(k_d.reshape(b, s, N_DIFF_HEADS, 2, HEAD_DIM), pos)
    v_d = v_d.reshape(b, s, N_DIFF_HEADS, 2 * HEAD_DIM)
    new_diff = jnp.stack([k_d.reshape(b, s, N_DIFF_HEADS, 2 * HEAD_DIM), v_d], axis=2)
    diff_kv = jnp.concatenate([past_diff, new_diff], axis=1)
    t = diff_kv.shape[1]
    lam_init = 0.8 - 0.6 * math.exp(-0.3 * layer)
    lam = (jnp.exp(jnp.sum(lam_q1.astype(jnp.float32) * lam_k1.astype(jnp.float32)))
           - jnp.exp(jnp.sum(lam_q2.astype(jnp.float32) * lam_k2.astype(jnp.float32))) + lam_init)
    o_diff = diff_attention(q_d, diff_kv[:, :, 0].reshape(b, t, N_DIFF_HEADS, 2, HEAD_DIM),
                            diff_kv[:, :, 1], lam, lam_init, diff_subln, q0)

    q_n = rope(q_n.reshape(b, s, N_NSA_HEADS, HEAD_DIM), pos)
    kv_n = kv_n.reshape(b, s, 6, N_NSA_KV_HEADS, HEAD_DIM)
    k_rows = rope(kv_n[:, :, 0::2], pos)
    kv_n = jnp.stack([k_rows, kv_n[:, :, 1::2]], axis=3).reshape(b, s, 6, N_NSA_KV_HEADS, HEAD_DIM)
    new_nsa = kv_n[:, :, :4]
    nsa_all = jnp.concatenate([past_nsa, new_nsa], axis=1)
    k_cmp = compress(nsa_all[:, :, 0], cmp_pos[0], cmp_w1[0], cmp_w2[0])
    v_cmp = compress(nsa_all[:, :, 1], cmp_pos[1], cmp_w1[1], cmp_w2[1])
    wb = win_buf.shape[1]
    win_all = jnp.concatenate([jnp.zeros((b, WINDOW - wb) + win_buf.shape[2:], win_buf.dtype),
                               win_buf, kv_n[:, :, 4:]], axis=1)
    new_win = win_all[:, win_all.shape[1] - min(WINDOW, q0 + s):]
    gates = jax.nn.sigmoid(g_n.astype(jnp.float32)).astype(x.dtype).reshape(b, s, N_NSA_HEADS, 3)
    o_nsa = nsa_attention(q_n, gates, k_cmp, v_cmp, nsa_all[:, :, 2], nsa_all[:, :, 3],
                          win_all[:, :, 0], win_all[:, :, 1], q0)

    mix = jnp.concatenate([o_diff, o_nsa], axis=-1) @ w_out
    x = x + rms_norm(mix, g_post_mix)
    hm = rms_norm(x, g_pre_mlp)
    ff = jnp.square(jax.nn.relu(hm @ w_up)) @ w_down
    x = x + rms_norm(ff, g_post_mlp)
    return x, new_diff, new_nsa, new_win


def setup_inputs(seed: int = 0) -> dict:
    key = jax.random.key(seed)
    ks = jax.random.split(key, 24)
    n_pages = PAST_LEN // PAGE_SIZE
    n_used = DEC_BATCH * n_pages
    n_phys = (5 * n_used + 3) // 4
    win_rows = min(WINDOW, PAST_LEN)
    f32 = jnp.float32
    nrm = lambda k, shp, sc: sc * jax.random.normal(k, shp, f32)
    gain = lambda k, shp: 1.0 + 0.02 * jax.random.normal(k, shp, f32)
    return {
        'x_prompt': nrm(ks[0], (BATCH, SEQ, D_MODEL), 1.0),
        'x_sample': nrm(ks[1], (DEC_BATCH, DEC_SEQ, D_MODEL), 1.0),
        'cache_diff_kv': nrm(ks[2], (DEPTH, n_phys, PAGE_SIZE, 2, N_DIFF_HEADS, 2 * HEAD_DIM), 1.0),
        'cache_nsa_kv': nrm(ks[3], (DEPTH, n_phys, PAGE_SIZE, 4, N_NSA_KV_HEADS, HEAD_DIM), 1.0),
        'state_nsa_win_kv': nrm(ks[4], (DEPTH, DEC_BATCH, win_rows, 2, N_NSA_KV_HEADS, HEAD_DIM), 1.0),
        'page_table': jax.random.permutation(ks[5], n_phys)[:n_used].reshape(DEC_BATCH, n_pages).astype(jnp.int32),
        'w_in': nrm(ks[6], (DEPTH, D_MODEL, D_IN), D_MODEL ** -0.5),
        'w_out': nrm(ks[7], (DEPTH, MIX_WIDTH, D_MODEL), MIX_WIDTH ** -0.5),
        'w_up': nrm(ks[8], (DEPTH, D_MODEL, D_FF), D_MODEL ** -0.5),
        'w_down': nrm(ks[9], (DEPTH, D_FF, D_MODEL), D_FF ** -0.5),
        'g_pre_mix': gain(ks[10], (DEPTH, D_MODEL)),
        'g_post_mix': gain(ks[11], (DEPTH, D_MODEL)),
        'g_pre_mlp': gain(ks[12], (DEPTH, D_MODEL)),
        'g_post_mlp': gain(ks[13], (DEPTH, D_MODEL)),
        'lam_q1': nrm(ks[14], (DEPTH, HEAD_DIM), 0.1),
        'lam_k1': nrm(ks[15], (DEPTH, HEAD_DIM), 0.1),
        'lam_q2': nrm(ks[16], (DEPTH, HEAD_DIM), 0.1),
        'lam_k2': nrm(ks[17], (DEPTH, HEAD_DIM), 0.1),
        'diff_subln': gain(ks[18], (DEPTH, 2 * HEAD_DIM)),
        'cmp_pos': nrm(ks[19], (DEPTH, 2, CMP_BLOCK, HEAD_DIM), 0.1),
        'cmp_w1': nrm(ks[20], (DEPTH, 2, CMP_BLOCK * HEAD_DIM, CMP_HIDDEN), (CMP_BLOCK * HEAD_DIM) ** -0.5),
        'cmp_w2': nrm(ks[21], (DEPTH, 2, CMP_HIDDEN, HEAD_DIM), CMP_HIDDEN ** -0.5),
    }


def reference(x_prompt, x_sample, cache_diff_kv, cache_nsa_kv, state_nsa_win_kv, page_table,
              w_in, w_out, w_up, w_down, g_pre_mix, g_post_mix, g_pre_mlp, g_post_mlp,
              lam_q1, lam_k1, lam_q2, lam_k2, diff_subln, cmp_pos, cmp_w1, cmp_w2):
    b = x_prompt.shape[0]
    db = x_sample.shape[0]
    past_len = page_table.shape[1] * PAGE_SIZE
    yp, ys = x_prompt, x_sample
    p_diff, p_nsa, p_win, s_diff, s_nsa, s_win = [], [], [], [], [], []
    for l in range(DEPTH):
        prm = (w_in[l], w_out[l], w_up[l], w_down[l], g_pre_mix[l], g_post_mix[l], g_pre_mlp[l], g_post_mlp[l],
               lam_q1[l], lam_k1[l], lam_q2[l], lam_k2[l], diff_subln[l], cmp_pos[l], cmp_w1[l], cmp_w2[l])
        yp, pd, pn, pw = decoder_layer(
            yp, jnp.zeros((b, 0) + cache_diff_kv.shape[3:], cache_diff_kv.dtype),
            jnp.zeros((b, 0) + cache_nsa_kv.shape[3:], cache_nsa_kv.dtype),
            jnp.zeros((b, 0) + state_nsa_win_kv.shape[3:], state_nsa_win_kv.dtype), l, *prm)
        past_diff = cache_diff_kv[l, page_table].reshape((db, past_len) + cache_diff_kv.shape[3:])
        past_nsa = cache_nsa_kv[l, page_table].reshape((db, past_len) + cache_nsa_kv.shape[3:])
        ys, sd, sn, sw = decoder_layer(ys, past_diff, past_nsa, state_nsa_win_kv[l], l, *prm)
        p_diff.append(pd); p_nsa.append(pn); p_win.append(pw)
        s_diff.append(sd); s_nsa.append(sn); s_win.append(sw)
    return (yp, ys, jnp.stack(p_diff), jnp.stack(p_nsa), jnp.stack(p_win),
            jnp.stack(s_diff), jnp.stack(s_nsa), jnp.stack(s_win))
```

```python
import functools
import math

import numpy as np
import jax
import jax.numpy as jnp
from jax import lax
from jax.experimental import pallas as pl
from jax.experimental.pallas import tpu as pltpu

HEAD_DIM = 64
N_DIFF_HEADS = 4
N_NSA_HEADS = 8
N_NSA_KV_HEADS = 2
NSA_GROUP = N_NSA_HEADS // N_NSA_KV_HEADS
CMP_BLOCK = 32
CMP_STRIDE = 16
CMP_HIDDEN = 128
SEL_BLOCK = 64
SEL_TOP_N = 16
WINDOW = 512
ROPE_THETA = 10000.0
NORM_EPS = 1e-6
DIFF_WIDTH = N_DIFF_HEADS * 2 * HEAD_DIM
NSA_WIDTH = N_NSA_HEADS * HEAD_DIM
NSA_KV_WIDTH = 6 * N_NSA_KV_HEADS * HEAD_DIM
N_GATES = 3 * N_NSA_HEADS
MAIN_WIDTH = 3 * DIFF_WIDTH + NSA_WIDTH + NSA_KV_WIDTH
NEG = -1e9
BIG = 1e9
SCALE = HEAD_DIM ** -0.5

LANES = 128
SEL_LANES = 64
VMEM_LIMIT = 56 * 1024 * 1024

F32 = jnp.float32
BF16 = jnp.bfloat16

_HEAD_PERM = tuple(g + NSA_GROUP * half for g in range(NSA_GROUP) for half in range(2))


def _dot(a, b):
    return jnp.dot(a, b, preferred_element_type=F32)


def _dot_nt(a, b):
    return lax.dot_general(a, b, (((1,), (1,)), ((), ())), preferred_element_type=F32)


def _rms(x):
    return x * lax.rsqrt(jnp.mean(x * x, axis=-1, keepdims=True) + NORM_EPS)


def _params(*sem):
    return pltpu.CompilerParams(dimension_semantics=sem, vmem_limit_bytes=VMEM_LIMIT)


def _const_spec(shape):
    nd = len(shape)
    return pl.BlockSpec(shape, lambda *_: (0,) * nd)


def _inproj_kernel(x_ref, g_ref, w_ref, wg_ref, cos_ref, sin_ref,
                   qd_ref, kvd_ref, kvdb_ref, qn_ref, nsa_ref, win_ref, nsab_ref, gate_ref):
    hb = (_rms(x_ref[...]) * g_ref[...]).astype(BF16)
    cos = cos_ref[...]
    sin = sin_ref[...]
    lane = lax.broadcasted_iota(jnp.int32, cos.shape, 1)
    first_half = (lane % HEAD_DIM) < HEAD_DIM // 2

    def rope(t):
        partner = jnp.where(first_half, pltpu.roll(t, LANES - HEAD_DIM // 2, 1), pltpu.roll(t, HEAD_DIM // 2, 1))
        return t * cos + partner * sin

    def proj(c0):
        return _dot(hb, w_ref[:, c0:c0 + LANES])

    for j in range(DIFF_WIDTH // LANES):
        qd_ref[:, j * LANES:(j + 1) * LANES] = (rope(proj(j * LANES)) * SCALE).astype(BF16)
        k = rope(proj(DIFF_WIDTH + j * LANES))
        kvd_ref[:, j * LANES:(j + 1) * LANES] = k
        kvdb_ref[:, j * LANES:(j + 1) * LANES] = k.astype(BF16)
        v = proj(2 * DIFF_WIDTH + j * LANES)
        kvd_ref[:, DIFF_WIDTH + j * LANES:DIFF_WIDTH + (j + 1) * LANES] = v
        kvdb_ref[:, DIFF_WIDTH + j * LANES:DIFF_WIDTH + (j + 1) * LANES] = v.astype(BF16)
    for j in range(NSA_WIDTH // LANES):
        qn_ref[:, j * LANES:(j + 1) * LANES] = (rope(proj(3 * DIFF_WIDTH + j * LANES)) * SCALE).astype(BF16)
    base = 3 * DIFF_WIDTH + NSA_WIDTH
    for slot in range(6):
        t = proj(base + slot * LANES)
        if slot % 2 == 0:
            t = rope(t)
        if slot < 4:
            nsa_ref[:, slot * LANES:(slot + 1) * LANES] = t
        else:
            win_ref[:, (slot - 4) * LANES:(slot - 3) * LANES] = t
        if slot >= 2:
            nsab_ref[:, (slot - 2) * LANES:(slot - 1) * LANES] = t.astype(BF16)
    gate_ref[...] = jax.nn.sigmoid(_dot(hb, wg_ref[...]))


def _inproj(x2d, g, w_main, w_gate, cos_t, sin_t, tm, n_tab_blocks):
    rows, d = x2d.shape
    row_spec = lambda n: pl.BlockSpec((tm, n), lambda i: (i, 0))
    tab_spec = pl.BlockSpec((tm, LANES), lambda i: (i % n_tab_blocks, 0))
    out_shapes = (
        jax.ShapeDtypeStruct((rows, DIFF_WIDTH), BF16),
        jax.ShapeDtypeStruct((rows, 2 * DIFF_WIDTH), F32),
        jax.ShapeDtypeStruct((rows, 2 * DIFF_WIDTH), BF16),
        jax.ShapeDtypeStruct((rows, NSA_WIDTH), BF16),
        jax.ShapeDtypeStruct((rows, 4 * LANES), F32),
        jax.ShapeDtypeStruct((rows, 2 * LANES), F32),
        jax.ShapeDtypeStruct((rows, 4 * LANES), BF16),
        jax.ShapeDtypeStruct((rows, LANES), F32),
    )
    return pl.pallas_call(
        _inproj_kernel,
        grid=(rows // tm,),
        in_specs=[row_spec(d), _const_spec((1, d)), _const_spec(w_main.shape), _const_spec(w_gate.shape),
                  tab_spec, tab_spec],
        out_specs=tuple(row_spec(s.shape[1]) for s in out_shapes),
        out_shape=out_shapes,
        compiler_params=_params("parallel"),
        name="inproj",
    )(x2d, g, w_main, w_gate, cos_t, sin_t)


def _lambda(lq1, lk1, lq2, lk2, lam_init):
    return (jnp.exp(jnp.sum(lq1[...] * lk1[...], axis=-1, keepdims=True))
            - jnp.exp(jnp.sum(lq2[...] * lk2[...], axis=-1, keepdims=True)) + lam_init)


def _diff_prompt_kernel(qi_ref, ki_ref, q_ref, k_ref, v_ref, lq1, lk1, lq2, lk2, g_ref, o_ref,
                        m_ref, l_ref, acc_ref, *, tile, lam_init):
    t = pl.program_id(1)
    qi = qi_ref[t]
    ki = ki_ref[t]

    @pl.when(ki == 0)
    def _init():
        m_ref[...] = jnp.full(m_ref.shape, NEG, F32)
        l_ref[...] = jnp.zeros(l_ref.shape, F32)
        acc_ref[...] = jnp.zeros(acc_ref.shape, F32)

    lane = lax.broadcasted_iota(jnp.int32, (tile, LANES), 1)
    row = lax.broadcasted_iota(jnp.int32, (tile, tile), 0)
    col = lax.broadcasted_iota(jnp.int32, (tile, tile), 1)
    visible = (ki * tile + col) <= (qi * tile + row)
    for h in range(N_DIFF_HEADS):
        q = q_ref[:, h * LANES:(h + 1) * LANES].astype(F32)
        k = k_ref[:, h * LANES:(h + 1) * LANES]
        v = v_ref[:, h * LANES:(h + 1) * LANES]
        for c in range(2):
            idx = 2 * h + c
            keep = (lane < HEAD_DIM) if c == 0 else (lane >= HEAD_DIM)
            s = jnp.where(visible, _dot_nt(jnp.where(keep, q, 0.0).astype(BF16), k), NEG)
            m_prev = m_ref[idx]
            m_new = jnp.maximum(m_prev, jnp.max(s, axis=1, keepdims=True))
            alpha = jnp.exp(m_prev - m_new)
            p = jnp.exp(s - m_new)
            l_ref[idx] = alpha * l_ref[idx] + jnp.sum(p, axis=1, keepdims=True)
            acc_ref[idx] = alpha * acc_ref[idx] + _dot(p.astype(BF16), v)
            m_ref[idx] = m_new

    @pl.when(ki == qi)
    def _finish():
        lam = _lambda(lq1, lk1, lq2, lk2, lam_init)
        for h in range(N_DIFF_HEADS):
            a = acc_ref[2 * h] / l_ref[2 * h] - lam * (acc_ref[2 * h + 1] / l_ref[2 * h + 1])
            o_ref[:, h * LANES:(h + 1) * LANES] = (_rms(a) * g_ref[...] * (1.0 - lam_init)).astype(BF16)


def _diff_prompt(qd, kvdb, lams, subln, b, s, tile, lam_init):
    nq = s // tile
    pairs = [(qi, ki) for qi in range(nq) for ki in range(qi + 1)]
    qi_arr = jnp.asarray([p[0] for p in pairs], jnp.int32)
    ki_arr = jnp.asarray([p[1] for p in pairs], jnp.int32)
    n_kcol = DIFF_WIDTH // (N_DIFF_HEADS * LANES)
    assert n_kcol == 1
    width = N_DIFF_HEADS * LANES
    grid_spec = pltpu.PrefetchScalarGridSpec(
        num_scalar_prefetch=2,
        grid=(b, len(pairs)),
        in_specs=[
            pl.BlockSpec((tile, width), lambda bi, t, qa, ka: (bi * nq + qa[t], 0)),
            pl.BlockSpec((tile, width), lambda bi, t, qa, ka: (bi * nq + ka[t], 0)),
            pl.BlockSpec((tile, width), lambda bi, t, qa, ka: (bi * nq + ka[t], 1)),
        ] + [_const_spec((1, HEAD_DIM))] * 4 + [_const_spec((1, 2 * HEAD_DIM))],
        out_specs=pl.BlockSpec((tile, width), lambda bi, t, qa, ka: (bi * nq + qa[t], 0)),
        scratch_shapes=[pltpu.VMEM((2 * N_DIFF_HEADS, tile, 1), F32),
                        pltpu.VMEM((2 * N_DIFF_HEADS, tile, 1), F32),
                        pltpu.VMEM((2 * N_DIFF_HEADS, tile, LANES), F32)],
    )
    return pl.pallas_call(
        functools.partial(_diff_prompt_kernel, tile=tile, lam_init=lam_init),
        grid_spec=grid_spec,
        out_shape=jax.ShapeDtypeStruct((b * s, DIFF_WIDTH), BF16),
        compiler_params=_params("parallel", "arbitrary"),
        name="diff_prompt",
    )(qi_arr, ki_arr, qd, kvdb, kvdb, *lams, subln)


def _pad_rows(x, rows):
    return jnp.concatenate([x, jnp.zeros((rows - x.shape[0], x.shape[1]), x.dtype)], axis=0)


def _diff_sample_kernel(pt_ref, *refs, n_pages, page, past, sq, lam_init):
    page_refs = refs[:n_pages]
    q_ref, new_ref, lq1, lk1, lq2, lk2, g_ref, o_ref = refs[n_pages:]
    width = DIFF_WIDTH
    n_rows = 2 * N_DIFF_HEADS * sq
    q = q_ref[0].astype(F32)
    lane = lax.broadcasted_iota(jnp.int32, (sq, width), 1)
    qbd = jnp.concatenate(
        [jnp.where(lane // HEAD_DIM == 2 * h + c, q, 0.0) for h in range(N_DIFF_HEADS) for c in range(2)],
        axis=0).astype(BF16)
    new = _pad_rows(new_ref[0], page)
    ks = [r[0, :, 0:width].astype(BF16) for r in page_refs] + [new[:, 0:width].astype(BF16)]
    vs = [r[0, :, width:2 * width].astype(BF16) for r in page_refs] + [new[:, width:2 * width].astype(BF16)]
    s = jnp.concatenate([_dot_nt(qbd, k) for k in ks], axis=1)
    n_keys = (n_pages + 1) * page
    kpos = lax.broadcasted_iota(jnp.int32, (n_rows, n_keys), 1)
    tpos = past + lax.broadcasted_iota(jnp.int32, (n_rows, n_keys), 0) % sq
    visible = kpos <= tpos
    s = jnp.where(visible, s, NEG)
    m = jnp.max(s, axis=1, keepdims=True)
    p = jnp.where(visible, jnp.exp(s - m), 0.0)
    l = jnp.sum(p, axis=1, keepdims=True)
    pb = p.astype(BF16)
    acc = _dot(pb[:, 0:page], vs[0])
    for i in range(1, n_pages + 1):
        acc = acc + _dot(pb[:, i * page:(i + 1) * page], vs[i])
    acc = acc / jnp.maximum(l, 1e-30)
    lam = _lambda(lq1, lk1, lq2, lk2, lam_init)
    for h in range(N_DIFF_HEADS):
        a1 = acc[(2 * h) * sq:(2 * h + 1) * sq, h * LANES:(h + 1) * LANES]
        a2 = acc[(2 * h + 1) * sq:(2 * h + 2) * sq, h * LANES:(h + 1) * LANES]
        a = a1 - lam * a2
        o_ref[0, :, h * LANES:(h + 1) * LANES] = (_rms(a) * g_ref[...] * (1.0 - lam_init)).astype(BF16)


def _diff_sample(cache_pages, page_table, qd3, kvd3, lams, subln, lam_init):
    db, sq, _ = qd3.shape
    n_pages = page_table.shape[1]
    page = cache_pages.shape[1]
    past = n_pages * page
    page_specs = [pl.BlockSpec((1, page, 2 * DIFF_WIDTH), functools.partial(lambda bi, pt, p: (pt[bi, p], 0, 0), p=p))
                  for p in range(n_pages)]
    grid_spec = pltpu.PrefetchScalarGridSpec(
        num_scalar_prefetch=1,
        grid=(db,),
        in_specs=page_specs + [
            pl.BlockSpec((1, sq, DIFF_WIDTH), lambda bi, pt: (bi, 0, 0)),
            pl.BlockSpec((1, sq, 2 * DIFF_WIDTH), lambda bi, pt: (bi, 0, 0)),
        ] + [_const_spec((1, HEAD_DIM))] * 4 + [_const_spec((1, 2 * HEAD_DIM))],
        out_specs=pl.BlockSpec((1, sq, DIFF_WIDTH), lambda bi, pt: (bi, 0, 0)),
    )
    return pl.pallas_call(
        functools.partial(_diff_sample_kernel, n_pages=n_pages, page=page, past=past, sq=sq, lam_init=lam_init),
        grid_spec=grid_spec,
        out_shape=jax.ShapeDtypeStruct((db, sq, DIFF_WIDTH), BF16),
        compiler_params=_params("parallel"),
        name="diff_sample",
    )(page_table, *([cache_pages] * n_pages), qd3, kvd3, *lams, subln)


def _compress(chunk_rows, slot, pos_a, pos_b, w_a, w_b, w2):
    row_w = 4 * LANES
    xs = jnp.concatenate(
        [jnp.concatenate([r[:, l * row_w + slot * LANES:l * row_w + (slot + 1) * LANES] for l in range(CMP_STRIDE)],
                         axis=1) for r in chunk_rows], axis=0)
    n = xs.shape[0]
    a = _dot((xs + pos_a).astype(BF16), w_a)
    b = _dot((xs + pos_b).astype(BF16), w_b)
    hidden = jax.nn.gelu(a + pltpu.roll(b, n - 1, 0))
    out = _dot(hidden.astype(BF16), w2)
    rid = lax.broadcasted_iota(jnp.int32, out.shape, 0)
    return jnp.where(rid < n - 1, out, 0.0)


class _ChunkView:
    def __init__(self, ref):
        self.ref = ref

    def __getitem__(self, idx):
        return self.ref[(0,) + idx]


def _compress_prompt_kernel(x_ref, pa_ref, pb_ref, wa_ref, wb_ref, w2_ref, o_ref):
    for slot in range(2):
        o_ref[0, slot] = _compress([_ChunkView(x_ref)], slot, pa_ref[slot], pb_ref[slot], wa_ref[slot],
                                   wb_ref[slot], w2_ref[slot]).astype(BF16)


def _compress_prompt(chunks, cw):
    b, nch, width = chunks.shape
    return pl.pallas_call(
        _compress_prompt_kernel,
        grid=(b,),
        in_specs=[pl.BlockSpec((1, nch, width), lambda i: (i, 0, 0))] + [_const_spec(a.shape) for a in cw],
        out_specs=pl.BlockSpec((1, 2, nch, LANES), lambda i: (i, 0, 0, 0)),
        out_shape=jax.ShapeDtypeStruct((b, 2, nch, LANES), BF16),
        compiler_params=_params("parallel"),
        name="compress_prompt",
    )(chunks, *cw)


def _cmp_branch(qs, kc, vc, ovl, tpos, n_q):
    s = _dot_nt(qs, kc)
    cend = lax.broadcasted_iota(jnp.int32, s.shape, 1) * CMP_STRIDE + (CMP_BLOCK - 1)
    vis = cend <= tpos
    s = jnp.where(vis, s, NEG)
    m = jnp.max(s, axis=1, keepdims=True)
    p = jnp.where(vis, jnp.exp(s - m), 0.0)
    p = (p / jnp.maximum(jnp.sum(p, axis=1, keepdims=True), 1e-30)).astype(BF16)
    o_c = _dot(p, vc)
    imp = _dot(p, ovl)
    imp = imp.reshape(N_NSA_KV_HEADS, NSA_GROUP, n_q, SEL_LANES)
    imp = (imp[:, 0] + imp[:, 1]) + (imp[:, 2] + imp[:, 3])
    return o_c, imp.reshape(N_NSA_KV_HEADS * n_q, SEL_LANES)


def _select_blocks(imp, tpos2):
    blk = lax.broadcasted_iota(jnp.int32, imp.shape, 1)
    cur = tpos2 // SEL_BLOCK
    forced = (blk == 0) | (blk == cur) | (blk == cur - 1)
    score = jnp.where(blk > cur, NEG, jnp.where(forced, BIG, imp))
    rank = jnp.zeros(imp.shape, F32)
    for j in range(SEL_LANES):
        sj = score[:, j:j + 1]
        tie = jnp.where(blk > j, 1.0, 0.0)
        rank = rank + jnp.where(sj > score, 1.0, 0.0) + jnp.where(sj == score, tie, 0.0)
    return jnp.where(rank < SEL_TOP_N, 1.0, 0.0).astype(BF16)


def _stack_queries(q, n_q):
    lane = lax.broadcasted_iota(jnp.int32, (n_q, LANES), 1)
    low = lane < HEAD_DIM
    groups = [q[:, g * LANES:(g + 1) * LANES].astype(F32) for g in range(NSA_GROUP)]
    return jnp.concatenate([jnp.where(low, t, 0.0) for t in groups] + [jnp.where(low, 0.0, t) for t in groups],
                           axis=0).astype(BF16)


def _combine(o_c, o_s, o_w, gates, n_q, store):
    lane = lax.broadcasted_iota(jnp.int32, (n_q, LANES), 1)
    low = lane < HEAD_DIM
    for g in range(NSA_GROUP):
        def pick(o):
            return jnp.where(low, o[g * n_q:(g + 1) * n_q], o[(NSA_GROUP + g) * n_q:(NSA_GROUP + g + 1) * n_q])

        def gate(j):
            c = j * N_NSA_HEADS + 2 * g
            return jnp.where(low, gates[:, c:c + 1], gates[:, c + 1:c + 2])

        store(g, gate(0) * pick(o_c) + gate(1) * pick(o_s) + gate(2) * pick(o_w))


def _nsa_prompt_kernel(qn_ref, gate_ref, cmp_ref, kv_ref, ovl_ref, exp_ref, o_ref, *, tq, tk):
    q_start = pl.program_id(1) * tq
    m_rows = N_NSA_HEADS * tq
    qs = _stack_queries(qn_ref[...], tq)
    tpos = q_start + lax.broadcasted_iota(jnp.int32, (m_rows, 1), 0) % tq
    tpos2 = q_start + lax.broadcasted_iota(jnp.int32, (N_NSA_KV_HEADS * tq, 1), 0) % tq

    o_c, imp = _cmp_branch(qs, cmp_ref[0, 0], cmp_ref[0, 1], ovl_ref[...], tpos, tq)
    sel = _select_blocks(imp, tpos2)

    def slc_step(kt, carry):
        m_prev, l_prev, acc = carry
        k0 = pl.multiple_of(kt * tk, tk)
        k = kv_ref[pl.ds(k0, tk), 0:LANES]
        v = kv_ref[pl.ds(k0, tk), LANES:2 * LANES]
        picked = _dot(sel, exp_ref[:, pl.ds(k0, tk)])
        kpos = k0 + lax.broadcasted_iota(jnp.int32, (1, tk), 1)
        bias = jnp.where(jnp.where(kpos <= tpos2, picked, 0.0) > 0.5, 0.0, NEG)
        s = _dot_nt(qs, k).reshape(N_NSA_KV_HEADS, NSA_GROUP, tq, tk) + bias.reshape(N_NSA_KV_HEADS, 1, tq, tk)
        s = s.reshape(m_rows, tk)
        m_new = jnp.maximum(m_prev, jnp.max(s, axis=1, keepdims=True))
        alpha = jnp.exp(m_prev - m_new)
        p = jnp.exp(s - m_new)
        return (m_new, alpha * l_prev + jnp.sum(p, axis=1, keepdims=True),
                alpha * acc + _dot(p.astype(BF16), v))

    n_kt = (q_start + tq - 1) // tk + 1
    init = (jnp.full((m_rows, 1), NEG, F32), jnp.zeros((m_rows, 1), F32), jnp.zeros((m_rows, LANES), F32))
    _, l_s, acc_s = lax.fori_loop(0, n_kt, slc_step, init)
    o_s = acc_s / l_s

    span = WINDOW + tq
    w0 = pl.multiple_of(jnp.maximum(q_start - WINDOW, 0), tq)
    kw = kv_ref[pl.ds(w0, span), 2 * LANES:3 * LANES]
    vw = kv_ref[pl.ds(w0, span), 3 * LANES:4 * LANES]
    s = _dot_nt(qs, kw)
    dist = tpos - (w0 + lax.broadcasted_iota(jnp.int32, (1, span), 1))
    vis = (dist >= 0) & (dist < WINDOW)
    s = jnp.where(vis, s, NEG)
    m = jnp.max(s, axis=1, keepdims=True)
    p = jnp.where(vis, jnp.exp(s - m), 0.0)
    o_w = _dot(p.astype(BF16), vw) / jnp.maximum(jnp.sum(p, axis=1, keepdims=True), 1e-30)

    def store(g, val):
        o_ref[:, g * LANES:(g + 1) * LANES] = val.astype(BF16)

    _combine(o_c, o_s, o_w, gate_ref[...], tq, store)


def _nsa_prompt(qn, gates, cmp_kv, nsab, ovl, expand, b, s, tq, tk):
    nq = s // tq
    nch = cmp_kv.shape[2]
    return pl.pallas_call(
        functools.partial(_nsa_prompt_kernel, tq=tq, tk=tk),
        grid=(b, nq),
        in_specs=[
            pl.BlockSpec((tq, NSA_WIDTH), lambda bi, qi: (bi * nq + qi, 0)),
            pl.BlockSpec((tq, LANES), lambda bi, qi: (bi * nq + qi, 0)),
            pl.BlockSpec((1, 2, nch, LANES), lambda bi, qi: (bi, 0, 0, 0)),
            pl.BlockSpec((s, 4 * LANES), lambda bi, qi: (bi, 0)),
            _const_spec(ovl.shape), _const_spec(expand.shape),
        ],
        out_specs=pl.BlockSpec((tq, NSA_WIDTH), lambda bi, qi: (bi * nq + qi, 0)),
        out_shape=jax.ShapeDtypeStruct((b * s, NSA_WIDTH), BF16),
        compiler_params=_params("parallel", "arbitrary"),
        name="nsa_prompt",
    )(qn, gates, cmp_kv, nsab, ovl, expand)


def _nsa_sample_kernel(pt_ref, *refs, n_pages, page, past, sq):
    chunk_refs = refs[:n_pages]
    row_refs = refs[n_pages:2 * n_pages]
    (win_ref, nsanew_ref, winnew_ref, qn_ref, gate_ref, pa_ref, pb_ref, wa_ref, wb_ref, w2_ref, ovl_ref, exp_ref,
     o_ref, wout_ref) = refs[2 * n_pages:]
    m_rows = N_NSA_HEADS * sq
    qs = _stack_queries(qn_ref[0], sq)
    tpos = past + lax.broadcasted_iota(jnp.int32, (m_rows, 1), 0) % sq
    tpos2 = past + lax.broadcasted_iota(jnp.int32, (N_NSA_KV_HEADS * sq, 1), 0) % sq

    views = [_ChunkView(r) for r in chunk_refs]
    kc, vc = [_compress(views, slot, pa_ref[slot], pb_ref[slot], wa_ref[slot], wb_ref[slot],
                        w2_ref[slot]).astype(BF16) for slot in range(2)]
    o_c, imp = _cmp_branch(qs, kc, vc, ovl_ref[...], tpos, sq)
    sel = _select_blocks(imp, tpos2)

    nsanew = _pad_rows(nsanew_ref[0], page)
    ks = [r[0, :, 0:LANES].astype(BF16) for r in row_refs] + [nsanew[:, 2 * LANES:3 * LANES].astype(BF16)]
    vs = [r[0, :, LANES:2 * LANES].astype(BF16) for r in row_refs] + [nsanew[:, 3 * LANES:4 * LANES].astype(BF16)]
    n_keys = (n_pages + 1) * page
    s = jnp.concatenate([_dot_nt(qs, k) for k in ks], axis=1)
    picked = _dot(sel, exp_ref[...])
    kpos = lax.broadcasted_iota(jnp.int32, (1, n_keys), 1)
    bias = jnp.where(jnp.where(kpos <= tpos2, picked, 0.0) > 0.5, 0.0, NEG)
    s = (s.reshape(N_NSA_KV_HEADS, NSA_GROUP, sq, n_keys) + bias.reshape(N_NSA_KV_HEADS, 1, sq, n_keys))
    s = s.reshape(m_rows, n_keys)
    m = jnp.max(s, axis=1, keepdims=True)
    p = jnp.exp(s - m)
    l = jnp.sum(p, axis=1, keepdims=True)
    pb = p.astype(BF16)
    acc = _dot(pb[:, 0:page], vs[0])
    for i in range(1, n_pages + 1):
        acc = acc + _dot(pb[:, i * page:(i + 1) * page], vs[i])
    o_s = acc / l

    win = win_ref[0]
    winnew = _pad_rows(winnew_ref[0], page)
    kw = jnp.concatenate([win[:, 0:LANES], winnew[:, 0:LANES]], axis=0).astype(BF16)
    vw = jnp.concatenate([win[:, LANES:2 * LANES], winnew[:, LANES:2 * LANES]], axis=0).astype(BF16)
    span = WINDOW + page
    col = lax.broadcasted_iota(jnp.int32, (1, span), 1)
    wpos = jnp.where(col < WINDOW, past - WINDOW + col, past + col - WINDOW)
    dist = tpos - wpos
    vis = (dist >= 0) & (dist < WINDOW) & (col < WINDOW + sq)
    s = jnp.where(vis, _dot_nt(qs, kw), NEG)
    m = jnp.max(s, axis=1, keepdims=True)
    p = jnp.where(vis, jnp.exp(s - m), 0.0)
    o_w = _dot(p.astype(BF16), vw) / jnp.maximum(jnp.sum(p, axis=1, keepdims=True), 1e-30)

    def store(g, val):
        o_ref[0, :, g * LANES:(g + 1) * LANES] = val.astype(BF16)

    _combine(o_c, o_s, o_w, gate_ref[0], sq, store)

    wout_ref[0, 0:WINDOW - sq, :] = win_ref[0, sq:WINDOW, :]
    wout_ref[0, WINDOW - sq:WINDOW, :] = winnew_ref[0]


def _nsa_sample(cache_chunks, cache_rows, page_table, win_state, nsanew3, winnew3, qn3, gates3, cw, ovl, expand):
    db, sq, _ = qn3.shape
    n_pages = page_table.shape[1]
    page = cache_rows.shape[1]
    past = n_pages * page
    cpp = page // CMP_STRIDE
    chunk_specs = [pl.BlockSpec((1, cpp, cache_chunks.shape[2]),
                                functools.partial(lambda bi, pt, p: (pt[bi, p], 0, 0), p=p)) for p in range(n_pages)]
    row_specs = [pl.BlockSpec((1, page, 2 * LANES), functools.partial(lambda bi, pt, p: (pt[bi, p], 0, 1), p=p))
                 for p in range(n_pages)]
    per_b = lambda shape: pl.BlockSpec((1,) + shape, lambda bi, pt: (bi, 0, 0))
    grid_spec = pltpu.PrefetchScalarGridSpec(
        num_scalar_prefetch=1,
        grid=(db,),
        in_specs=chunk_specs + row_specs + [
            per_b((WINDOW, 2 * LANES)), per_b((sq, 4 * LANES)), per_b((sq, 2 * LANES)), per_b((sq, NSA_WIDTH)),
            per_b((sq, LANES)),
        ] + [_const_spec(a.shape) for a in cw] + [_const_spec(ovl.shape), _const_spec(expand.shape)],
        out_specs=(per_b((sq, NSA_WIDTH)), per_b((WINDOW, 2 * LANES))),
    )
    return pl.pallas_call(
        functools.partial(_nsa_sample_kernel, n_pages=n_pages, page=page, past=past, sq=sq),
        grid_spec=grid_spec,
        out_shape=(jax.ShapeDtypeStruct((db, sq, NSA_WIDTH), BF16),
                   jax.ShapeDtypeStruct((db, WINDOW, 2 * LANES), F32)),
        compiler_params=_params("parallel"),
        name="nsa_sample",
    )(page_table, *([cache_chunks] * n_pages), *([cache_rows] * n_pages), win_state, nsanew3, winnew3, qn3, gates3,
      *cw, ovl, expand)


def _mlp_kernel(x_ref, od_ref, on_ref, wo_ref, wu_ref, wd_ref, g1_ref, g2_ref, g3_ref, y_ref, *, ff_chunk):
    mix = _dot(od_ref[...], wo_ref[0:DIFF_WIDTH, :]) + _dot(on_ref[...], wo_ref[DIFF_WIDTH:DIFF_WIDTH + NSA_WIDTH, :])
    x1 = x_ref[...] + _rms(mix) * g1_ref[...]
    hm = (_rms(x1) * g2_ref[...]).astype(BF16)
    d_ff = wu_ref.shape[1]
    ff = jnp.zeros(x1.shape, F32)
    for c in range(d_ff // ff_chunk):
        u = jnp.maximum(_dot(hm, wu_ref[:, c * ff_chunk:(c + 1) * ff_chunk]), 0.0)
        ff = ff + _dot((u * u).astype(BF16), wd_ref[c * ff_chunk:(c + 1) * ff_chunk, :])
    y_ref[...] = x1 + _rms(ff) * g3_ref[...]


def _mlp(x2d, od, on, w_out, w_up, w_down, g1, g2, g3, tm):
    rows, d = x2d.shape
    row_spec = lambda n: pl.BlockSpec((tm, n), lambda i: (i, 0))
    resident = lambda a: pl.BlockSpec(a.shape, lambda i: (0, 0), pipeline_mode=pl.Buffered(1))
    return pl.pallas_call(
        functools.partial(_mlp_kernel, ff_chunk=1024),
        grid=(rows // tm,),
        in_specs=[row_spec(d), row_spec(DIFF_WIDTH), row_spec(NSA_WIDTH), resident(w_out), resident(w_up),
                  resident(w_down), _const_spec((1, d)), _const_spec((1, d)), _const_spec((1, d))],
        out_specs=row_spec(d),
        out_shape=jax.ShapeDtypeStruct((rows, d), F32),
        compiler_params=_params("parallel"),
        name="mlp",
    )(x2d, od, on, w_out, w_up, w_down, g1, g2, g3)


def _rope_tables(pos, reps):
    half = HEAD_DIM // 2
    inv = ROPE_THETA ** (-jnp.arange(half, dtype=F32) / half)
    ang = pos.astype(F32)[:, None] * inv[None, :]
    cos, sin = jnp.cos(ang), jnp.sin(ang)
    cos_t = jnp.tile(cos, (reps, LANES // half))
    sin_t = jnp.tile(jnp.concatenate([-sin, sin], axis=1), (reps, LANES // HEAD_DIM))
    return cos_t, sin_t


def _compress_weights(cmp_pos, cmp_w1, cmp_w2):
    eye = jnp.eye(N_NSA_KV_HEADS, dtype=F32)
    w1 = cmp_w1.reshape(2, CMP_BLOCK, HEAD_DIM, CMP_HIDDEN)

    def expand_w1(w):
        t = jnp.einsum('sldf,hg->slhdgf', w, eye)
        return t.reshape(2, CMP_STRIDE * LANES, N_NSA_KV_HEADS * CMP_HIDDEN).astype(BF16)

    def expand_pos(p):
        return jnp.tile(p[:, :, None, :], (1, 1, N_NSA_KV_HEADS, 1)).reshape(2, 1, CMP_STRIDE * LANES)

    w2 = jnp.einsum('sfd,hg->shfgd', cmp_w2, eye).reshape(2, N_NSA_KV_HEADS * CMP_HIDDEN, LANES).astype(BF16)
    return (expand_pos(cmp_pos[:, :CMP_STRIDE]), expand_pos(cmp_pos[:, CMP_STRIDE:]),
            expand_w1(w1[:, :CMP_STRIDE]), expand_w1(w1[:, CMP_STRIDE:]), w2)


def _selection_constants(n_chunks, n_keys):
    n = np.arange(n_chunks)
    m = np.arange(SEL_LANES)
    cs, ss = n * CMP_STRIDE, m * SEL_BLOCK
    ovl = (cs[:, None] < ss[None, :] + SEL_BLOCK) & (cs[:, None] + CMP_BLOCK > ss[None, :]) & (n[:, None] < n_chunks - 1)
    expand = (np.arange(n_keys)[None, :] // SEL_BLOCK) == m[:, None]
    return jnp.asarray(ovl, BF16), jnp.asarray(expand, BF16)


def _row_tile(rows, want):
    t = min(rows, want)
    assert rows % t == 0
    return t


def kernel(x_prompt, x_sample, cache_diff_kv, cache_nsa_kv, state_nsa_win_kv, page_table, w_in, w_out, w_up, w_down,
           g_pre_mix, g_post_mix, g_pre_mlp, g_post_mlp, lam_q1, lam_k1, lam_q2, lam_k2, diff_subln, cmp_pos,
           cmp_w1, cmp_w2):
    depth = w_in.shape[0]
    assert depth == 1, "one layer: the sample group's paged caches are read in place"
    b, s, d = x_prompt.shape
    db, sq, _ = x_sample.shape
    n_phys, page = cache_diff_kv.shape[1:3]
    n_pages = page_table.shape[1]
    past = n_pages * page
    assert s % (4 * LANES) == 0 and s >= WINDOW + LANES and s <= SEL_LANES * SEL_BLOCK
    assert past >= WINDOW and sq < CMP_STRIDE and sq <= page and past + sq <= SEL_LANES * SEL_BLOCK
    assert state_nsa_win_kv.shape[2] == WINDOW and page % CMP_STRIDE == 0
    layer = 0
    lam_init = 0.8 - 0.6 * math.exp(-0.3 * layer)

    wl = w_in[layer]
    qn0 = 3 * DIFF_WIDTH
    qn_cols = np.concatenate([qn0 + h * HEAD_DIM + np.arange(HEAD_DIM) for h in _HEAD_PERM])
    main_cols = np.concatenate([np.arange(qn0), qn_cols, np.arange(qn0 + NSA_WIDTH, MAIN_WIDTH)])
    gate_cols = np.asarray([MAIN_WIDTH + h * 3 + j for j in range(3) for h in _HEAD_PERM])
    w_main = wl[:, main_cols].astype(BF16)
    w_gate = jnp.pad(wl[:, gate_cols], ((0, 0), (0, LANES - N_GATES))).astype(BF16)
    out_rows = np.concatenate([np.arange(DIFF_WIDTH)] + [DIFF_WIDTH + h * HEAD_DIM + np.arange(HEAD_DIM) for h in _HEAD_PERM])
    wo = w_out[layer][out_rows].astype(BF16)
    wu = w_up[layer].astype(BF16)
    wd = w_down[layer].astype(BF16)
    vec = lambda a: a[layer].reshape(1, -1)
    lams = (vec(lam_q1), vec(lam_k1), vec(lam_q2), vec(lam_k2))
    cw = _compress_weights(cmp_pos[layer], cmp_w1[layer], cmp_w2[layer])

    rows_p = b * s
    tm_p = _row_tile(s, 256)
    cos_p, sin_p = _rope_tables(jnp.arange(s), 1)
    xp = x_prompt.reshape(rows_p, d)
    qd, kvd, kvdb, qn, nsa4, win2, nsab, gates = _inproj(xp, vec(g_pre_mix), w_main, w_gate, cos_p, sin_p, tm_p,
                                                         s // tm_p)
    od = _diff_prompt(qd, kvdb, lams, vec(diff_subln), b, s, _row_tile(s, 512), lam_init)
    n_chunks_p = s // CMP_STRIDE
    cmp_kv = _compress_prompt(nsa4.reshape(b, n_chunks_p, CMP_STRIDE * 4 * LANES), cw)
    ovl_p, exp_p = _selection_constants(n_chunks_p, s)
    on = _nsa_prompt(qn, gates, cmp_kv, nsab, ovl_p, exp_p, b, s, LANES, 4 * LANES)
    yp = _mlp(xp, od, on, wo, wu, wd, vec(g_post_mix), vec(g_pre_mlp), vec(g_post_mlp), tm_p).reshape(b, s, d)
    p_diff = kvd.reshape(1, b, s, 2, N_DIFF_HEADS, 2 * HEAD_DIM)
    p_nsa = nsa4.reshape(1, b, s, 4, N_NSA_KV_HEADS, HEAD_DIM)
    p_win = win2.reshape(b, s, 2, N_NSA_KV_HEADS, HEAD_DIM)[None, :, s - WINDOW:]

    rows_s = db * sq
    tm_s = _row_tile(rows_s, 256)
    assert tm_s % sq == 0
    cos_s, sin_s = _rope_tables(past + jnp.arange(sq), tm_s // sq)
    xs = x_sample.reshape(rows_s, d)
    qd, kvd, kvdb, qn, nsa4, win2, nsab, gates = _inproj(xs, vec(g_pre_mix), w_main, w_gate, cos_s, sin_s, tm_s, 1)
    r3 = lambda a: a.reshape(db, sq, a.shape[1])
    diff_pages = cache_diff_kv[layer].reshape(n_phys, page, 2 * DIFF_WIDTH)
    od = _diff_sample(diff_pages, page_table, r3(qd), r3(kvd), lams, vec(diff_subln), lam_init)
    nsa_rows = cache_nsa_kv[layer].reshape(n_phys, page, 4 * LANES)
    nsa_chunks = cache_nsa_kv[layer].reshape(n_phys, page // CMP_STRIDE, CMP_STRIDE * 4 * LANES)
    ovl_s, exp_s = _selection_constants(past // CMP_STRIDE, past + page)
    win_state = state_nsa_win_kv[layer].reshape(db, WINDOW, 2 * LANES)
    on, win_new = _nsa_sample(nsa_chunks, nsa_rows, page_table, win_state, r3(nsa4), r3(win2), r3(qn), r3(gates), cw,
                              ovl_s, exp_s)
    ys = _mlp(xs, od.reshape(rows_s, DIFF_WIDTH), on.reshape(rows_s, NSA_WIDTH), wo, wu, wd, vec(g_post_mix),
              vec(g_pre_mlp), vec(g_post_mlp), tm_s).reshape(db, sq, d)
    s_diff = kvd.reshape(1, db, sq, 2, N_DIFF_HEADS, 2 * HEAD_DIM)
    s_nsa = nsa4.reshape(1, db, sq, 4, N_NSA_KV_HEADS, HEAD_DIM)
    s_win = win_new.reshape(1, db, WINDOW, 2, N_NSA_KV_HEADS, HEAD_DIM)

    return yp, ys, p_diff, p_nsa, p_win, s_diff, s_nsa, s_win
```

```python
import functools
import math

import numpy as np
import jax
import jax.numpy as jnp
from jax import lax
from jax.experimental import pallas as pl
from jax.experimental.pallas import tpu as pltpu

HEAD_DIM = 64
N_DIFF_HEADS = 4
N_NSA_HEADS = 8
N_NSA_KV_HEADS = 2
NSA_GROUP = N_NSA_HEADS // N_NSA_KV_HEADS
CMP_BLOCK = 32
CMP_STRIDE = 16
CMP_HIDDEN = 128
SEL_BLOCK = 64
SEL_TOP_N = 16
WINDOW = 512
ROPE_THETA = 10000.0
NORM_EPS = 1e-6
DIFF_WIDTH = N_DIFF_HEADS * 2 * HEAD_DIM
NSA_WIDTH = N_NSA_HEADS * HEAD_DIM
N_GATES = 3 * N_NSA_HEADS
NEG = -1e9
BIG = 1e9
SCALE = HEAD_DIM ** -0.5

LANES = 128
SUBLANES = 8
SEL_LANES = 64
VMEM_LIMIT = 56 * 1024 * 1024
DIFF_SLABS = 2 * N_DIFF_HEADS
QKV_WIDTH = 3 * DIFF_WIDTH + NSA_WIDTH
KV_SLOTS = 6

F32 = jnp.float32
BF16 = jnp.bfloat16

_HEAD_PERM = tuple(g + NSA_GROUP * half for g in range(NSA_GROUP) for half in range(2))


def _dot(a, b):
    return jnp.dot(a, b, preferred_element_type=F32)


def _dot_nt(a, b):
    return lax.dot_general(a, b, (((1,), (1,)), ((), ())), preferred_element_type=F32)


def _rms(x):
    return x * lax.rsqrt(jnp.mean(x * x, axis=-1, keepdims=True) + NORM_EPS)


def _params(*sem):
    return pltpu.CompilerParams(dimension_semantics=sem, vmem_limit_bytes=VMEM_LIMIT)


def _const_spec(shape):
    nd = len(shape)
    return pl.BlockSpec(shape, lambda *_: (0,) * nd)


def _pad_rows(x, rows):
    return jnp.concatenate([x, jnp.zeros((rows - x.shape[0], x.shape[1]), x.dtype)], axis=0)


def _rotary(t, cos, sin, axis):
    idx = lax.broadcasted_iota(jnp.int32, t.shape, axis)
    first_half = (idx % HEAD_DIM) < HEAD_DIM // 2
    partner = jnp.where(first_half, pltpu.roll(t, LANES - HEAD_DIM // 2, axis), pltpu.roll(t, HEAD_DIM // 2, axis))
    return t * cos + partner * sin


def _inproj_kernel(*refs, kv_transposed):
    if kv_transposed:
        (x_ref, g_ref, w_ref, wkv_ref, wg_ref, cos_ref, sin_ref, cost_ref, sint_ref,
         qd_ref, kvd_ref, kvdb_ref, qn_ref, nsa_ref, win_ref, kvb_ref, gate_ref) = refs
    else:
        (x_ref, g_ref, w_ref, wkv_ref, wg_ref, cos_ref, sin_ref,
         qd_ref, kvd_ref, qn_ref, nsa_ref, win_ref, gate_ref) = refs
    tm = x_ref.shape[0]
    hb = (_rms(x_ref[...]) * g_ref[...]).astype(BF16)
    cos = cos_ref[...]
    sin = sin_ref[...]

    def proj(c0):
        return _dot(hb, w_ref[:, c0:c0 + LANES])

    for j in range(N_DIFF_HEADS):
        qd_ref[:, j * LANES:(j + 1) * LANES] = (_rotary(proj(j * LANES), cos, sin, 1) * SCALE).astype(BF16)
        k = _rotary(proj(DIFF_WIDTH + j * LANES), cos, sin, 1)
        v = proj(2 * DIFF_WIDTH + j * LANES)
        kvd_ref[pl.ds(j, tm, stride=DIFF_SLABS), :] = k
        kvd_ref[pl.ds(N_DIFF_HEADS + j, tm, stride=DIFF_SLABS), :] = v
        if kv_transposed:
            kvdb_ref[:, j * LANES:(j + 1) * LANES] = k.astype(BF16)
            kvdb_ref[:, DIFF_WIDTH + j * LANES:DIFF_WIDTH + (j + 1) * LANES] = v.astype(BF16)
    for j in range(NSA_WIDTH // LANES):
        qn_ref[:, j * LANES:(j + 1) * LANES] = (_rotary(proj(3 * DIFF_WIDTH + j * LANES), cos, sin, 1)
                                                * SCALE).astype(BF16)
    gate_ref[...] = jax.nn.sigmoid(_dot(hb, wg_ref[...]))

    if kv_transposed:
        t_all = _dot_nt(wkv_ref[...], hb)
        cost = cost_ref[...]
        sint = sint_ref[...]
        for slot in range(KV_SLOTS):
            t = t_all[slot * LANES:(slot + 1) * LANES]
            if slot % 2 == 0:
                t = _rotary(t, cost, sint, 0)
            if slot < 4:
                nsa_ref[0, slot * LANES:(slot + 1) * LANES, :] = t
            else:
                win_ref[0, (slot - 4) * LANES:(slot - 3) * LANES, :] = t
            if slot >= 2:
                kvb_ref[0, (slot - 2) * LANES:(slot - 1) * LANES, :] = t.astype(BF16)
    else:
        for slot in range(KV_SLOTS):
            t = _dot(hb, wkv_ref[:, slot * LANES:(slot + 1) * LANES])
            if slot % 2 == 0:
                t = _rotary(t, cos, sin, 1)
            if slot < 4:
                nsa_ref[:, slot * LANES:(slot + 1) * LANES] = t
            else:
                win_ref[:, (slot - 4) * LANES:(slot - 3) * LANES] = t


def _inproj(x2d, g, w_qkv, w_kv, w_gate, tabs, tm, n_tab_blocks, batch, kv_transposed):
    rows, d = x2d.shape
    row_spec = lambda n: pl.BlockSpec((tm, n), lambda i: (i, 0))
    tab_spec = pl.BlockSpec((tm, LANES), lambda i: (i % n_tab_blocks, 0))
    in_specs = [row_spec(d), _const_spec((1, d)), _const_spec(w_qkv.shape), _const_spec(w_kv.shape),
                _const_spec(w_gate.shape), tab_spec, tab_spec]
    sds = jax.ShapeDtypeStruct
    interleaved = (sds((rows * DIFF_SLABS, LANES), F32), pl.BlockSpec((tm * DIFF_SLABS, LANES), lambda i: (i, 0)))
    if kv_transposed:
        seq = rows // batch
        nt = seq // tm
        tabt_spec = pl.BlockSpec((LANES, tm), lambda i: (0, i % nt))
        in_specs += [tabt_spec, tabt_spec]
        tspec = lambda n: pl.BlockSpec((1, n, tm), lambda i: (i // nt, 0, i % nt))
        outs = [
            (sds((rows, DIFF_WIDTH), BF16), row_spec(DIFF_WIDTH)),
            interleaved,
            (sds((rows, 2 * DIFF_WIDTH), BF16), row_spec(2 * DIFF_WIDTH)),
            (sds((rows, NSA_WIDTH), BF16), row_spec(NSA_WIDTH)),
            (sds((batch, 4 * LANES, seq), F32), tspec(4 * LANES)),
            (sds((batch, 2 * LANES, seq), F32), tspec(2 * LANES)),
            (sds((batch, 4 * LANES, seq), BF16), tspec(4 * LANES)),
            (sds((rows, LANES), F32), row_spec(LANES)),
        ]
    else:
        outs = [
            (sds((rows, DIFF_WIDTH), BF16), row_spec(DIFF_WIDTH)),
            interleaved,
            (sds((rows, NSA_WIDTH), BF16), row_spec(NSA_WIDTH)),
            (sds((rows, 4 * LANES), F32), row_spec(4 * LANES)),
            (sds((rows, 2 * LANES), F32), row_spec(2 * LANES)),
            (sds((rows, LANES), F32), row_spec(LANES)),
        ]
    return pl.pallas_call(
        functools.partial(_inproj_kernel, kv_transposed=kv_transposed),
        grid=(rows // tm,),
        in_specs=in_specs,
        out_specs=tuple(o[1] for o in outs),
        out_shape=tuple(o[0] for o in outs),
        compiler_params=_params("parallel"),
        name="inproj",
    )(x2d, g, w_qkv, w_kv, w_gate, *tabs)


def _lambda(lq1, lk1, lq2, lk2, lam_init):
    return (jnp.exp(jnp.sum(lq1[...] * lk1[...], axis=-1, keepdims=True))
            - jnp.exp(jnp.sum(lq2[...] * lk2[...], axis=-1, keepdims=True)) + lam_init)


def _split_components(q):
    lane = lax.broadcasted_iota(jnp.int32, q.shape, 1)
    low = lane < HEAD_DIM
    return jnp.concatenate([jnp.where(low, q, 0.0), jnp.where(low, 0.0, q)], axis=0).astype(BF16)


def _diff_prompt_kernel(qi_ref, ki_ref, q_ref, k_ref, v_ref, lq1, lk1, lq2, lk2, g_ref, o_ref,
                        m_ref, l_ref, acc_ref, *, tile, lam_init):
    t = pl.program_id(1)
    qi = qi_ref[t]
    ki = ki_ref[t]

    @pl.when(ki == 0)
    def _init():
        m_ref[...] = jnp.full(m_ref.shape, NEG, F32)
        l_ref[...] = jnp.zeros(l_ref.shape, F32)
        acc_ref[...] = jnp.zeros(acc_ref.shape, F32)

    row = lax.broadcasted_iota(jnp.int32, (2 * tile, tile), 0) % tile
    col = lax.broadcasted_iota(jnp.int32, (2 * tile, tile), 1)
    visible = (ki * tile + col) <= (qi * tile + row)
    for h in range(N_DIFF_HEADS):
        q2 = _split_components(q_ref[:, h * LANES:(h + 1) * LANES].astype(F32))
        k = k_ref[:, h * LANES:(h + 1) * LANES]
        v = v_ref[:, h * LANES:(h + 1) * LANES]
        s = jnp.where(visible, _dot_nt(q2, k), NEG)
        m_prev = m_ref[h]
        m_new = jnp.maximum(m_prev, jnp.max(s, axis=1, keepdims=True))
        alpha = jnp.exp(m_prev - m_new)
        p = jnp.exp(s - m_new)
        l_ref[h] = alpha * l_ref[h] + jnp.sum(p, axis=1, keepdims=True)
        acc_ref[h] = alpha * acc_ref[h] + _dot(p.astype(BF16), v)
        m_ref[h] = m_new

    @pl.when(ki == qi)
    def _finish():
        lam = _lambda(lq1, lk1, lq2, lk2, lam_init)
        for h in range(N_DIFF_HEADS):
            o = acc_ref[h] / l_ref[h]
            a = o[0:tile] - lam * o[tile:2 * tile]
            o_ref[:, h * LANES:(h + 1) * LANES] = (_rms(a) * g_ref[...] * (1.0 - lam_init)).astype(BF16)


def _diff_prompt(qd, kvdb, lams, subln, b, s, tile, lam_init):
    nq = s // tile
    pairs = [(qi, ki) for qi in range(nq) for ki in range(qi + 1)]
    qi_arr = jnp.asarray([p[0] for p in pairs], jnp.int32)
    ki_arr = jnp.asarray([p[1] for p in pairs], jnp.int32)
    grid_spec = pltpu.PrefetchScalarGridSpec(
        num_scalar_prefetch=2,
        grid=(b, len(pairs)),
        in_specs=[
            pl.BlockSpec((tile, DIFF_WIDTH), lambda bi, t, qa, ka: (bi * nq + qa[t], 0)),
            pl.BlockSpec((tile, DIFF_WIDTH), lambda bi, t, qa, ka: (bi * nq + ka[t], 0)),
            pl.BlockSpec((tile, DIFF_WIDTH), lambda bi, t, qa, ka: (bi * nq + ka[t], 1)),
        ] + [_const_spec((1, HEAD_DIM))] * 4 + [_const_spec((1, 2 * HEAD_DIM))],
        out_specs=pl.BlockSpec((tile, DIFF_WIDTH), lambda bi, t, qa, ka: (bi * nq + qa[t], 0)),
        scratch_shapes=[pltpu.VMEM((N_DIFF_HEADS, 2 * tile, 1), F32),
                        pltpu.VMEM((N_DIFF_HEADS, 2 * tile, 1), F32),
                        pltpu.VMEM((N_DIFF_HEADS, 2 * tile, LANES), F32)],
    )
    return pl.pallas_call(
        functools.partial(_diff_prompt_kernel, tile=tile, lam_init=lam_init),
        grid_spec=grid_spec,
        out_shape=jax.ShapeDtypeStruct((b * s, DIFF_WIDTH), BF16),
        compiler_params=_params("parallel", "arbitrary"),
        name="diff_prompt",
    )(qi_arr, ki_arr, qd, kvdb, kvdb, *lams, subln)


def _diff_sample_kernel(pt_ref, *refs, n_pages, page, past, sq, lam_init):
    page_refs = refs[:n_pages]
    q_ref, new_ref, lq1, lk1, lq2, lk2, g_ref, o_ref = refs[n_pages:]
    n_keys = (n_pages + 1) * page
    kpos = lax.broadcasted_iota(jnp.int32, (2 * sq, n_keys), 1)
    tpos = past + lax.broadcasted_iota(jnp.int32, (2 * sq, n_keys), 0) % sq
    visible = kpos <= tpos
    lam = _lambda(lq1, lk1, lq2, lk2, lam_init)

    def slab(j):
        parts = [r[pl.ds(j, page, stride=DIFF_SLABS), :] for r in page_refs]
        parts.append(_pad_rows(new_ref[pl.ds(j, sq, stride=DIFF_SLABS), :], page))
        return jnp.concatenate(parts, axis=0).astype(BF16)

    for h in range(N_DIFF_HEADS):
        q2 = _split_components(q_ref[0, :, h * LANES:(h + 1) * LANES].astype(F32))
        s = jnp.where(visible, _dot_nt(q2, slab(h)), NEG)
        m = jnp.max(s, axis=1, keepdims=True)
        p = jnp.where(visible, jnp.exp(s - m), 0.0)
        o = _dot(p.astype(BF16), slab(N_DIFF_HEADS + h)) / jnp.maximum(jnp.sum(p, axis=1, keepdims=True), 1e-30)
        a = o[0:sq] - lam * o[sq:2 * sq]
        o_ref[0, :, h * LANES:(h + 1) * LANES] = (_rms(a) * g_ref[...] * (1.0 - lam_init)).astype(BF16)


def _diff_sample(cache_rows, page_table, qd3, kvd_rows, lams, subln, page, lam_init):
    db, sq, _ = qd3.shape
    n_pages = page_table.shape[1]
    past = n_pages * page
    page_specs = [pl.BlockSpec((page * DIFF_SLABS, LANES), functools.partial(lambda bi, pt, p: (pt[bi, p], 0), p=p))
                  for p in range(n_pages)]
    grid_spec = pltpu.PrefetchScalarGridSpec(
        num_scalar_prefetch=1,
        grid=(db,),
        in_specs=page_specs + [
            pl.BlockSpec((1, sq, DIFF_WIDTH), lambda bi, pt: (bi, 0, 0)),
            pl.BlockSpec((sq * DIFF_SLABS, LANES), lambda bi, pt: (bi, 0)),
        ] + [_const_spec((1, HEAD_DIM))] * 4 + [_const_spec((1, 2 * HEAD_DIM))],
        out_specs=pl.BlockSpec((1, sq, DIFF_WIDTH), lambda bi, pt: (bi, 0, 0)),
    )
    return pl.pallas_call(
        functools.partial(_diff_sample_kernel, n_pages=n_pages, page=page, past=past, sq=sq, lam_init=lam_init),
        grid_spec=grid_spec,
        out_shape=jax.ShapeDtypeStruct((db, sq, DIFF_WIDTH), BF16),
        compiler_params=_params("parallel"),
        name="diff_sample",
    )(page_table, *([cache_rows] * n_pages), qd3, kvd_rows, *lams, subln)


def _compress(rows_ref, slot, n_chunks, pos_a, pos_b, w_a, w_b, w2):
    xs = jnp.concatenate([rows_ref[slot, pl.ds(j, n_chunks, stride=CMP_STRIDE), :] for j in range(CMP_STRIDE)],
                         axis=1)
    a = _dot((xs + pos_a).astype(BF16), w_a)
    b = _dot((xs + pos_b).astype(BF16), w_b)
    hidden = jax.nn.gelu(a + pltpu.roll(b, n_chunks - 1, 0))
    out = _dot(hidden.astype(BF16), w2)
    rid = lax.broadcasted_iota(jnp.int32, out.shape, 0)
    return jnp.where(rid < n_chunks - 1, out, 0.0)


def _compress_prompt_kernel(x_ref, pa_ref, pb_ref, wa_ref, wb_ref, w2_ref, o_ref, rows_ref):
    seq = x_ref.shape[2]
    for slot in range(2):
        for c in range(seq // LANES):
            rows_ref[slot, c * LANES:(c + 1) * LANES, :] = x_ref[0, slot * LANES:(slot + 1) * LANES,
                                                                 c * LANES:(c + 1) * LANES].T
        o_ref[0, slot] = _compress(rows_ref, slot, seq // CMP_STRIDE, pa_ref[slot], pb_ref[slot], wa_ref[slot],
                                   wb_ref[slot], w2_ref[slot]).astype(BF16)


def _compress_prompt(nsa_t, cw):
    b, _, seq = nsa_t.shape
    nch = seq // CMP_STRIDE
    return pl.pallas_call(
        _compress_prompt_kernel,
        grid=(b,),
        in_specs=[pl.BlockSpec((1, 2 * LANES, seq), lambda i: (i, 0, 0))] + [_const_spec(a.shape) for a in cw],
        out_specs=pl.BlockSpec((1, 2, nch, LANES), lambda i: (i, 0, 0, 0)),
        out_shape=jax.ShapeDtypeStruct((b, 2, nch, LANES), BF16),
        scratch_shapes=[pltpu.VMEM((2, seq, LANES), F32)],
        compiler_params=_params("parallel"),
        name="compress_prompt",
    )(nsa_t, *cw)


def _cmp_probs(qs, kc, tpos):
    s = _dot_nt(qs, kc)
    cend = lax.broadcasted_iota(jnp.int32, s.shape, 1) * CMP_STRIDE + (CMP_BLOCK - 1)
    vis = cend <= tpos
    s = jnp.where(vis, s, NEG)
    m = jnp.max(s, axis=1, keepdims=True)
    p = jnp.where(vis, jnp.exp(s - m), 0.0)
    return (p / jnp.maximum(jnp.sum(p, axis=1, keepdims=True), 1e-30)).astype(BF16)


def _block_scores(imp, blk, cur):
    forced = (blk == 0) | (blk == cur) | (blk == cur - 1)
    return jnp.where(blk > cur, NEG, jnp.where(forced, BIG, imp))


def _select_blocks(imp, tpos2):
    blk = lax.broadcasted_iota(jnp.int32, imp.shape, 1)
    score = _block_scores(imp, blk, tpos2 // SEL_BLOCK)
    rank = jnp.zeros(imp.shape, F32)
    for j in range(SEL_LANES):
        sj = score[:, j:j + 1]
        tie = jnp.where(blk > j, 1.0, 0.0)
        rank = rank + jnp.where(sj > score, 1.0, 0.0) + jnp.where(sj == score, tie, 0.0)
    return jnp.where(rank < SEL_TOP_N, 1.0, 0.0).astype(BF16)


def _select_blocks_t(imp_t, tpos_t):
    n = imp_t.shape[1]
    blk = lax.broadcasted_iota(jnp.int32, imp_t.shape, 0)
    score = _block_scores(imp_t, blk, tpos_t // SEL_BLOCK)
    n_groups = SEL_LANES // SUBLANES
    groups = [score[g * SUBLANES:(g + 1) * SUBLANES] for g in range(n_groups)]
    ranks = [jnp.zeros((SUBLANES, n), F32) for _ in range(n_groups)]
    rid = lax.broadcasted_iota(jnp.int32, (SUBLANES, n), 0)
    for j in range(SEL_LANES):
        sj = score[j:j + 1]
        for g in range(n_groups):
            if g * SUBLANES > j:
                beats = jnp.where(sj >= groups[g], 1.0, 0.0)
            elif (g + 1) * SUBLANES - 1 < j:
                beats = jnp.where(sj > groups[g], 1.0, 0.0)
            else:
                beats = jnp.where(rid + g * SUBLANES > j, jnp.where(sj >= groups[g], 1.0, 0.0),
                                  jnp.where(sj > groups[g], 1.0, 0.0))
            ranks[g] = ranks[g] + beats
    sel_t = jnp.concatenate([jnp.where(r < SEL_TOP_N, 1.0, 0.0) for r in ranks]
                            + [jnp.zeros((LANES - SEL_LANES, n), F32)], axis=0)
    return sel_t.T.astype(BF16)


def _stack_queries(q, n_q):
    lane = lax.broadcasted_iota(jnp.int32, (n_q, LANES), 1)
    low = lane < HEAD_DIM
    groups = [q[:, g * LANES:(g + 1) * LANES].astype(F32) for g in range(NSA_GROUP)]
    return jnp.concatenate([jnp.where(low, t, 0.0) for t in groups] + [jnp.where(low, 0.0, t) for t in groups],
                           axis=0).astype(BF16)


def _combine(o_c, o_s, o_w, gates, n_q, store):
    lane = lax.broadcasted_iota(jnp.int32, (n_q, LANES), 1)
    low = lane < HEAD_DIM
    for g in range(NSA_GROUP):
        def pick(o):
            return jnp.where(low, o[g * n_q:(g + 1) * n_q], o[(NSA_GROUP + g) * n_q:(NSA_GROUP + g + 1) * n_q])

        def gate(j):
            c = j * N_NSA_HEADS + 2 * g
            return jnp.where(low, gates[:, c:c + 1], gates[:, c + 1:c + 2])

        store(g, gate(0) * pick(o_c) + gate(1) * pick(o_s) + gate(2) * pick(o_w))


def _per_kv_head(x, n_q):
    return x.reshape(N_NSA_KV_HEADS, NSA_GROUP, n_q, x.shape[-1])


def _nsa_prompt_kernel(qn_ref, gate_ref, cmp_ref, kv_ref, ovl_ref, exp_ref, o_ref, *, tq, tk):
    q_start = pl.program_id(1) * tq
    m_rows = N_NSA_HEADS * tq
    qs = _stack_queries(qn_ref[...], tq)
    tpos = q_start + lax.broadcasted_iota(jnp.int32, (m_rows, 1), 0) % tq
    tpos2 = q_start + lax.broadcasted_iota(jnp.int32, (N_NSA_KV_HEADS * tq, 1), 0) % tq
    tpos2_t = q_start + lax.broadcasted_iota(jnp.int32, (1, N_NSA_KV_HEADS * tq), 1) % tq

    p_c = _cmp_probs(qs, cmp_ref[0, 0], tpos)
    o_c = _dot(p_c, cmp_ref[0, 1])
    imp_t = _dot_nt(ovl_ref[...], p_c)
    imp_t = jnp.concatenate(
        [(imp_t[:, (4 * h) * tq:(4 * h + 1) * tq] + imp_t[:, (4 * h + 1) * tq:(4 * h + 2) * tq])
         + (imp_t[:, (4 * h + 2) * tq:(4 * h + 3) * tq] + imp_t[:, (4 * h + 3) * tq:(4 * h + 4) * tq])
         for h in range(N_NSA_KV_HEADS)], axis=1)
    sel = _select_blocks_t(imp_t, tpos2_t)

    def slc_step(kt, carry):
        m_prev, l_prev, acc = carry
        k0 = pl.multiple_of(kt * tk, tk)
        k_t = kv_ref[0, 0:LANES, pl.ds(k0, tk)]
        v_t = kv_ref[0, LANES:2 * LANES, pl.ds(k0, tk)]
        picked = _dot(sel, exp_ref[:, pl.ds(k0, tk)])
        kpos = k0 + lax.broadcasted_iota(jnp.int32, (1, tk), 1)
        bias = jnp.where(jnp.where(kpos <= tpos2, picked, 0.0) > 0.5, 0.0, NEG)
        s = (_per_kv_head(_dot(qs, k_t), tq) + bias.reshape(N_NSA_KV_HEADS, 1, tq, tk)).reshape(m_rows, tk)
        m_new = jnp.maximum(m_prev, jnp.max(s, axis=1, keepdims=True))
        alpha = jnp.exp(m_prev - m_new)
        p = jnp.exp(s - m_new)
        return (m_new, alpha * l_prev + jnp.sum(p, axis=1, keepdims=True),
                alpha * acc + _dot_nt(p.astype(BF16), v_t))

    n_kt = (q_start + tq - 1) // tk + 1
    init = (jnp.full((m_rows, 1), NEG, F32), jnp.zeros((m_rows, 1), F32), jnp.zeros((m_rows, LANES), F32))
    _, l_s, acc_s = lax.fori_loop(0, n_kt, slc_step, init)
    o_s = acc_s / l_s

    span = WINDOW + tq
    w0 = pl.multiple_of(jnp.maximum(q_start - WINDOW, 0), tq)
    s = _dot(qs, kv_ref[0, 2 * LANES:3 * LANES, pl.ds(w0, span)])
    dist = tpos - (w0 + lax.broadcasted_iota(jnp.int32, (1, span), 1))
    vis = (dist >= 0) & (dist < WINDOW)
    s = jnp.where(vis, s, NEG)
    m = jnp.max(s, axis=1, keepdims=True)
    p = jnp.where(vis, jnp.exp(s - m), 0.0)
    o_w = (_dot_nt(p.astype(BF16), kv_ref[0, 3 * LANES:4 * LANES, pl.ds(w0, span)])
           / jnp.maximum(jnp.sum(p, axis=1, keepdims=True), 1e-30))

    def store(g, val):
        o_ref[:, g * LANES:(g + 1) * LANES] = val.astype(BF16)

    _combine(o_c, o_s, o_w, gate_ref[...], tq, store)


def _nsa_prompt(qn, gates, cmp_kv, kv_t, ovl_t, expand, b, s, tq, tk):
    nq = s // tq
    nch = cmp_kv.shape[2]
    return pl.pallas_call(
        functools.partial(_nsa_prompt_kernel, tq=tq, tk=tk),
        grid=(b, nq),
        in_specs=[
            pl.BlockSpec((tq, NSA_WIDTH), lambda bi, qi: (bi * nq + qi, 0)),
            pl.BlockSpec((tq, LANES), lambda bi, qi: (bi * nq + qi, 0)),
            pl.BlockSpec((1, 2, nch, LANES), lambda bi, qi: (bi, 0, 0, 0)),
            pl.BlockSpec((1, 4 * LANES, s), lambda bi, qi: (bi, 0, 0)),
            _const_spec(ovl_t.shape), _const_spec(expand.shape),
        ],
        out_specs=pl.BlockSpec((tq, NSA_WIDTH), lambda bi, qi: (bi * nq + qi, 0)),
        out_shape=jax.ShapeDtypeStruct((b * s, NSA_WIDTH), BF16),
        compiler_params=_params("parallel", "arbitrary"),
        name="nsa_prompt",
    )(qn, gates, cmp_kv, kv_t, ovl_t, expand)


def _nsa_sample_kernel(pt_ref, *refs, n_pages, page, past, sq):
    page_refs = refs[:n_pages]
    (win_ref, nsanew_ref, winnew_ref, qn_ref, gate_ref, pa_ref, pb_ref, wa_ref, wb_ref, w2_ref, ovl_ref, exp_ref,
     o_ref, wout_ref, rows_ref) = refs[n_pages:]
    m_rows = N_NSA_HEADS * sq
    qs = _stack_queries(qn_ref[0], sq)
    tpos = past + lax.broadcasted_iota(jnp.int32, (m_rows, 1), 0) % sq
    tpos2 = past + lax.broadcasted_iota(jnp.int32, (N_NSA_KV_HEADS * sq, 1), 0) % sq

    def page_t(i, slot):
        return page_refs[i][0, slot].reshape(N_NSA_KV_HEADS * HEAD_DIM, page)

    cmp_kv = []
    for slot in range(2):
        for i in range(n_pages):
            rows_ref[slot, i * page:(i + 1) * page, :] = page_t(i, slot).T
        cmp_kv.append(_compress(rows_ref, slot, past // CMP_STRIDE, pa_ref[slot], pb_ref[slot], wa_ref[slot],
                                wb_ref[slot], w2_ref[slot]).astype(BF16))
    p_c = _cmp_probs(qs, cmp_kv[0], tpos)
    o_c = _dot(p_c, cmp_kv[1])
    imp = _per_kv_head(_dot(p_c, ovl_ref[...]), sq)
    imp = ((imp[:, 0] + imp[:, 1]) + (imp[:, 2] + imp[:, 3])).reshape(N_NSA_KV_HEADS * sq, SEL_LANES)
    sel = _select_blocks(imp, tpos2)

    nsanew = _pad_rows(nsanew_ref[0], page)
    n_keys = (n_pages + 1) * page
    s = jnp.concatenate([_dot(qs, page_t(i, 2).astype(BF16)) for i in range(n_pages)]
                        + [_dot_nt(qs, nsanew[:, 2 * LANES:3 * LANES].astype(BF16))], axis=1)
    picked = _dot(sel, exp_ref[...])
    kpos = lax.broadcasted_iota(jnp.int32, (1, n_keys), 1)
    bias = jnp.where(jnp.where(kpos <= tpos2, picked, 0.0) > 0.5, 0.0, NEG)
    s = (_per_kv_head(s, sq) + bias.reshape(N_NSA_KV_HEADS, 1, sq, n_keys)).reshape(m_rows, n_keys)
    m = jnp.max(s, axis=1, keepdims=True)
    p = jnp.exp(s - m)
    l = jnp.sum(p, axis=1, keepdims=True)
    pb = p.astype(BF16)
    acc = _dot(pb[:, past:n_keys], nsanew[:, 3 * LANES:4 * LANES].astype(BF16))
    for i in range(n_pages):
        acc = acc + _dot_nt(pb[:, i * page:(i + 1) * page], page_t(i, 3).astype(BF16))
    o_s = acc / l

    winnew = _pad_rows(winnew_ref[0], page)
    s = jnp.concatenate([_dot(qs, win_ref[0, 0:LANES, :].astype(BF16)),
                         _dot_nt(qs, winnew[:, 0:LANES].astype(BF16))], axis=1)
    col = lax.broadcasted_iota(jnp.int32, (1, WINDOW + page), 1)
    dist = tpos - (past - WINDOW + col)
    vis = (dist >= 0) & (dist < WINDOW)
    s = jnp.where(vis, s, NEG)
    m = jnp.max(s, axis=1, keepdims=True)
    p = jnp.where(vis, jnp.exp(s - m), 0.0)
    pb = p.astype(BF16)
    o_w = ((_dot_nt(pb[:, 0:WINDOW], win_ref[0, LANES:2 * LANES, :].astype(BF16))
            + _dot(pb[:, WINDOW:WINDOW + page], winnew[:, LANES:2 * LANES].astype(BF16)))
           / jnp.maximum(jnp.sum(p, axis=1, keepdims=True), 1e-30))

    def store(g, val):
        o_ref[0, :, g * LANES:(g + 1) * LANES] = val.astype(BF16)

    _combine(o_c, o_s, o_w, gate_ref[0], sq, store)

    extended = jnp.concatenate([win_ref[0], winnew.T], axis=1)
    wout_ref[0] = extended[:, sq:sq + WINDOW]


def _nsa_sample(cache_t, page_table, win_t, nsanew3, winnew3, qn3, gates3, cw, ovl, expand):
    db, sq, _ = qn3.shape
    n_pages = page_table.shape[1]
    page = cache_t.shape[-1]
    past = n_pages * page
    page_specs = [pl.BlockSpec((1,) + cache_t.shape[1:], functools.partial(lambda bi, pt, p: (pt[bi, p], 0, 0, 0, 0), p=p))
                  for p in range(n_pages)]
    per_b = lambda shape: pl.BlockSpec((1,) + shape, lambda bi, pt: (bi, 0, 0))
    grid_spec = pltpu.PrefetchScalarGridSpec(
        num_scalar_prefetch=1,
        grid=(db,),
        in_specs=page_specs + [
            per_b((2 * LANES, WINDOW)), per_b((sq, 4 * LANES)), per_b((sq, 2 * LANES)), per_b((sq, NSA_WIDTH)),
            per_b((sq, LANES)),
        ] + [_const_spec(a.shape) for a in cw] + [_const_spec(ovl.shape), _const_spec(expand.shape)],
        out_specs=(per_b((sq, NSA_WIDTH)), per_b((2 * LANES, WINDOW))),
        scratch_shapes=[pltpu.VMEM((2, past, LANES), F32)],
    )
    return pl.pallas_call(
        functools.partial(_nsa_sample_kernel, n_pages=n_pages, page=page, past=past, sq=sq),
        grid_spec=grid_spec,
        out_shape=(jax.ShapeDtypeStruct((db, sq, NSA_WIDTH), BF16),
                   jax.ShapeDtypeStruct((db, 2 * LANES, WINDOW), F32)),
        compiler_params=_params("parallel"),
        name="nsa_sample",
    )(page_table, *([cache_t] * n_pages), win_t, nsanew3, winnew3, qn3, gates3, *cw, ovl, expand)


def _mlp_kernel(x_ref, od_ref, on_ref, wo_ref, wu_ref, wd_ref, g1_ref, g2_ref, g3_ref, y_ref, *, ff_chunk):
    mix = _dot(od_ref[...], wo_ref[0:DIFF_WIDTH, :]) + _dot(on_ref[...], wo_ref[DIFF_WIDTH:DIFF_WIDTH + NSA_WIDTH, :])
    x1 = x_ref[...] + _rms(mix) * g1_ref[...]
    hm = (_rms(x1) * g2_ref[...]).astype(BF16)
    d_ff = wu_ref.shape[1]
    ff = jnp.zeros(x1.shape, F32)
    for c in range(d_ff // ff_chunk):
        u = jnp.maximum(_dot(hm, wu_ref[:, c * ff_chunk:(c + 1) * ff_chunk]), 0.0)
        ff = ff + _dot((u * u).astype(BF16), wd_ref[c * ff_chunk:(c + 1) * ff_chunk, :])
    y_ref[...] = x1 + _rms(ff) * g3_ref[...]


def _mlp(x2d, od, on, w_out, w_up, w_down, g1, g2, g3, tm):
    rows, d = x2d.shape
    row_spec = lambda n: pl.BlockSpec((tm, n), lambda i: (i, 0))
    resident = lambda a: pl.BlockSpec(a.shape, lambda i: (0, 0), pipeline_mode=pl.Buffered(1))
    return pl.pallas_call(
        functools.partial(_mlp_kernel, ff_chunk=1024),
        grid=(rows // tm,),
        in_specs=[row_spec(d), row_spec(DIFF_WIDTH), row_spec(NSA_WIDTH), resident(w_out), resident(w_up),
                  resident(w_down), _const_spec((1, d)), _const_spec((1, d)), _const_spec((1, d))],
        out_specs=row_spec(d),
        out_shape=jax.ShapeDtypeStruct((rows, d), F32),
        compiler_params=_params("parallel"),
        name="mlp",
    )(x2d, od, on, w_out, w_up, w_down, g1, g2, g3)


def _rope_tables(pos, reps):
    half = HEAD_DIM // 2
    inv = ROPE_THETA ** (-jnp.arange(half, dtype=F32) / half)
    ang = pos.astype(F32)[:, None] * inv[None, :]
    cos, sin = jnp.cos(ang), jnp.sin(ang)
    cos_t = jnp.tile(cos, (reps, LANES // half))
    sin_t = jnp.tile(jnp.concatenate([-sin, sin], axis=1), (reps, LANES // HEAD_DIM))
    return cos_t, sin_t


def _compress_weights(cmp_pos, cmp_w1, cmp_w2):
    eye = jnp.eye(N_NSA_KV_HEADS, dtype=F32)
    w1 = cmp_w1.reshape(2, CMP_BLOCK, HEAD_DIM, CMP_HIDDEN)

    def expand_w1(w):
        t = jnp.einsum('sldf,hg->slhdgf', w, eye)
        return t.reshape(2, CMP_STRIDE * LANES, N_NSA_KV_HEADS * CMP_HIDDEN).astype(BF16)

    def expand_pos(p):
        return jnp.tile(p[:, :, None, :], (1, 1, N_NSA_KV_HEADS, 1)).reshape(2, 1, CMP_STRIDE * LANES)

    w2 = jnp.einsum('sfd,hg->shfgd', cmp_w2, eye).reshape(2, N_NSA_KV_HEADS * CMP_HIDDEN, LANES).astype(BF16)
    return (expand_pos(cmp_pos[:, :CMP_STRIDE]), expand_pos(cmp_pos[:, CMP_STRIDE:]),
            expand_w1(w1[:, :CMP_STRIDE]), expand_w1(w1[:, CMP_STRIDE:]), w2)


def _selection_constants(n_chunks, n_keys, expand_rows):
    n = np.arange(n_chunks)
    m = np.arange(SEL_LANES)
    cs, ss = n * CMP_STRIDE, m * SEL_BLOCK
    ovl = (cs[:, None] < ss[None, :] + SEL_BLOCK) & (cs[:, None] + CMP_BLOCK > ss[None, :]) & (n[:, None] < n_chunks - 1)
    expand = (np.arange(n_keys)[None, :] // SEL_BLOCK) == np.arange(expand_rows)[:, None]
    return ovl, jnp.asarray(expand, BF16)


def _row_tile(rows, want):
    t = min(rows, want)
    assert rows % t == 0
    return t


def kernel(x_prompt, x_sample, cache_diff_kv, cache_nsa_kv, state_nsa_win_kv, page_table, w_in, w_out, w_up, w_down,
           g_pre_mix, g_post_mix, g_pre_mlp, g_post_mlp, lam_q1, lam_k1, lam_q2, lam_k2, diff_subln, cmp_pos,
           cmp_w1, cmp_w2):
    depth = w_in.shape[0]
    assert depth == 1, "one layer: the sample group's paged caches are read in place"
    b, s, d = x_prompt.shape
    db, sq, _ = x_sample.shape
    n_phys, page = cache_diff_kv.shape[1:3]
    n_pages = page_table.shape[1]
    past = n_pages * page
    assert s % (4 * LANES) == 0 and s >= WINDOW + LANES and s <= SEL_LANES * SEL_BLOCK
    assert past >= WINDOW and sq < CMP_STRIDE and sq <= page and past + sq <= SEL_LANES * SEL_BLOCK
    assert state_nsa_win_kv.shape[2] == WINDOW and page == LANES
    layer = 0
    lam_init = 0.8 - 0.6 * math.exp(-0.3 * layer)

    wl = w_in[layer]
    qn0 = 3 * DIFF_WIDTH
    qn_cols = np.concatenate([qn0 + h * HEAD_DIM + np.arange(HEAD_DIM) for h in _HEAD_PERM])
    w_qkv = wl[:, np.concatenate([np.arange(qn0), qn_cols])].astype(BF16)
    w_kv = wl[:, QKV_WIDTH:QKV_WIDTH + KV_SLOTS * LANES].astype(BF16)
    gate0 = QKV_WIDTH + KV_SLOTS * LANES
    gate_cols = np.asarray([gate0 + h * 3 + j for j in range(3) for h in _HEAD_PERM])
    w_gate = jnp.pad(wl[:, gate_cols], ((0, 0), (0, LANES - N_GATES))).astype(BF16)
    out_rows = np.concatenate([np.arange(DIFF_WIDTH)] + [DIFF_WIDTH + h * HEAD_DIM + np.arange(HEAD_DIM) for h in _HEAD_PERM])
    wo = w_out[layer][out_rows].astype(BF16)
    wu = w_up[layer].astype(BF16)
    wd = w_down[layer].astype(BF16)
    vec = lambda a: a[layer].reshape(1, -1)
    lams = (vec(lam_q1), vec(lam_k1), vec(lam_q2), vec(lam_k2))
    cw = _compress_weights(cmp_pos[layer], cmp_w1[layer], cmp_w2[layer])

    rows_p = b * s
    tm_p = _row_tile(s, 256)
    cos_p, sin_p = _rope_tables(jnp.arange(s), 1)
    xp = x_prompt.reshape(rows_p, d)
    qd, kvd, kvdb, qn, nsa_t, win_t, kvb_t, gates = _inproj(
        xp, vec(g_pre_mix), w_qkv, w_kv.T, w_gate, (cos_p, sin_p, cos_p.T, sin_p.T), tm_p, s // tm_p, b, True)
    od = _diff_prompt(qd, kvdb, lams, vec(diff_subln), b, s, _row_tile(s, 512), lam_init)
    cmp_kv = _compress_prompt(nsa_t, cw)
    ovl_p, exp_p = _selection_constants(s // CMP_STRIDE, s, LANES)
    on = _nsa_prompt(qn, gates, cmp_kv, kvb_t, jnp.asarray(ovl_p.T, BF16), exp_p, b, s, LANES, 4 * LANES)
    yp = _mlp(xp, od, on, wo, wu, wd, vec(g_post_mix), vec(g_pre_mlp), vec(g_post_mlp), tm_p).reshape(b, s, d)
    p_diff = kvd.reshape(1, b, s, 2, N_DIFF_HEADS, 2 * HEAD_DIM)
    token_minor = lambda a, slots: jnp.transpose(
        a.reshape(a.shape[0], slots, N_NSA_KV_HEADS, HEAD_DIM, a.shape[-1]), (0, 4, 1, 2, 3))[None]
    p_nsa = token_minor(nsa_t, 4)
    p_win = token_minor(win_t[:, :, s - WINDOW:], 2)

    rows_s = db * sq
    tm_s = _row_tile(rows_s, 256)
    assert tm_s % sq == 0
    cos_s, sin_s = _rope_tables(past + jnp.arange(sq), tm_s // sq)
    xs = x_sample.reshape(rows_s, d)
    qd, kvd, qn, nsa4, win2, gates = _inproj(xs, vec(g_pre_mix), w_qkv, w_kv, w_gate, (cos_s, sin_s), tm_s, 1, db,
                                             False)
    r3 = lambda a: a.reshape(db, sq, a.shape[1])
    diff_rows = cache_diff_kv[layer].reshape(n_phys * page * DIFF_SLABS, LANES)
    od = _diff_sample(diff_rows, page_table, r3(qd), kvd, lams, vec(diff_subln), page, lam_init)
    nsa_cache_t = jnp.transpose(cache_nsa_kv[layer], (0, 2, 3, 4, 1))
    win_state_t = jnp.transpose(state_nsa_win_kv[layer], (0, 2, 3, 4, 1)).reshape(db, 2 * LANES, WINDOW)
    ovl_s, exp_s = _selection_constants(past // CMP_STRIDE, past + page, SEL_LANES)
    on, win_new_t = _nsa_sample(nsa_cache_t, page_table, win_state_t, r3(nsa4), r3(win2), r3(qn), r3(gates), cw,
                                jnp.asarray(ovl_s, BF16), exp_s)
    ys = _mlp(xs, od.reshape(rows_s, DIFF_WIDTH), on.reshape(rows_s, NSA_WIDTH), wo, wu, wd, vec(g_post_mix),
              vec(g_pre_mlp), vec(g_post_mlp), tm_s).reshape(db, sq, d)
    s_diff = kvd.reshape(1, db, sq, 2, N_DIFF_HEADS, 2 * HEAD_DIM)
    s_nsa = nsa4.reshape(1, db, sq, 4, N_NSA_KV_HEADS, HEAD_DIM)
    s_win = token_minor(win_new_t, 2)

    return yp, ys, p_diff, p_nsa, p_win, s_diff, s_nsa, s_win
```

```python
import functools
import math

import numpy as np
import jax
import jax.numpy as jnp
from jax import lax
from jax.experimental import pallas as pl
from jax.experimental.pallas import tpu as pltpu

HEAD_DIM = 64
N_DIFF_HEADS = 4
N_NSA_HEADS = 8
N_NSA_KV_HEADS = 2
NSA_GROUP = N_NSA_HEADS // N_NSA_KV_HEADS
CMP_BLOCK = 32
CMP_STRIDE = 16
CMP_HIDDEN = 128
SEL_BLOCK = 64
SEL_TOP_N = 16
WINDOW = 512
ROPE_THETA = 10000.0
NORM_EPS = 1e-6
DIFF_WIDTH = N_DIFF_HEADS * 2 * HEAD_DIM
NSA_WIDTH = N_NSA_HEADS * HEAD_DIM
N_GATES = 3 * N_NSA_HEADS
NEG = -1e9
BIG = 1e9
SCALE = HEAD_DIM ** -0.5

LANES = 128
SUBLANES = 8
SEL_LANES = 64
SOFTMAX_ROWS = 32
VMEM_LIMIT = 56 * 1024 * 1024
DIFF_SLABS = 2 * N_DIFF_HEADS
QKV_WIDTH = 3 * DIFF_WIDTH + NSA_WIDTH
KV_SLOTS = 6

F32 = jnp.float32
BF16 = jnp.bfloat16

_HEAD_PERM = tuple(g + NSA_GROUP * half for g in range(NSA_GROUP) for half in range(2))


def _dot(a, b):
    return jnp.dot(a, b, preferred_element_type=F32)


def _dot_nt(a, b):
    return lax.dot_general(a, b, (((1,), (1,)), ((), ())), preferred_element_type=F32)


def _rms(x):
    return x * lax.rsqrt(jnp.mean(x * x, axis=-1, keepdims=True) + NORM_EPS)


def _params(*sem):
    return pltpu.CompilerParams(dimension_semantics=sem, vmem_limit_bytes=VMEM_LIMIT)


def _const_spec(shape):
    nd = len(shape)
    return pl.BlockSpec(shape, lambda *_: (0,) * nd)


def _pad_rows(x, rows):
    return jnp.concatenate([x, jnp.zeros((rows - x.shape[0], x.shape[1]), x.dtype)], axis=0)


def _rotary(t, cos, sin, axis):
    idx = lax.broadcasted_iota(jnp.int32, t.shape, axis)
    first_half = (idx % HEAD_DIM) < HEAD_DIM // 2
    partner = jnp.where(first_half, pltpu.roll(t, LANES - HEAD_DIM // 2, axis), pltpu.roll(t, HEAD_DIM // 2, axis))
    return t * cos + partner * sin


def _inproj_kernel(*refs, kv_transposed):
    if kv_transposed:
        (x_ref, g_ref, w_ref, wkv_ref, wg_ref, cos_ref, sin_ref, cost_ref, sint_ref,
         qd_ref, kvd_ref, kvdb_ref, qn_ref, nsa_ref, win_ref, kvb_ref, gate_ref) = refs
    else:
        (x_ref, g_ref, w_ref, wkv_ref, wg_ref, cos_ref, sin_ref,
         qd_ref, kvd_ref, qn_ref, nsa_ref, win_ref, gate_ref) = refs
    tm = x_ref.shape[0]
    hb = (_rms(x_ref[...]) * g_ref[...]).astype(BF16)
    cos = cos_ref[...]
    sin = sin_ref[...]

    def proj(c0):
        return _dot(hb, w_ref[:, c0:c0 + LANES])

    for j in range(N_DIFF_HEADS):
        qd_ref[:, j * LANES:(j + 1) * LANES] = (_rotary(proj(j * LANES), cos, sin, 1) * SCALE).astype(BF16)
        k = _rotary(proj(DIFF_WIDTH + j * LANES), cos, sin, 1)
        v = proj(2 * DIFF_WIDTH + j * LANES)
        kvd_ref[pl.ds(j, tm, stride=DIFF_SLABS), :] = k
        kvd_ref[pl.ds(N_DIFF_HEADS + j, tm, stride=DIFF_SLABS), :] = v
        if kv_transposed:
            kvdb_ref[:, j * LANES:(j + 1) * LANES] = k.astype(BF16)
            kvdb_ref[:, DIFF_WIDTH + j * LANES:DIFF_WIDTH + (j + 1) * LANES] = v.astype(BF16)
    for j in range(NSA_WIDTH // LANES):
        qn_ref[:, j * LANES:(j + 1) * LANES] = (_rotary(proj(3 * DIFF_WIDTH + j * LANES), cos, sin, 1)
                                                * SCALE).astype(BF16)
    gate_ref[...] = jax.nn.sigmoid(_dot(hb, wg_ref[...]))

    if kv_transposed:
        t_all = _dot_nt(wkv_ref[...], hb)
        cost = cost_ref[...]
        sint = sint_ref[...]
        for slot in range(KV_SLOTS):
            t = t_all[slot * LANES:(slot + 1) * LANES]
            if slot % 2 == 0:
                t = _rotary(t, cost, sint, 0)
            if slot < 4:
                nsa_ref[0, slot * LANES:(slot + 1) * LANES, :] = t
            else:
                win_ref[0, (slot - 4) * LANES:(slot - 3) * LANES, :] = t
            if slot >= 2:
                kvb_ref[0, (slot - 2) * LANES:(slot - 1) * LANES, :] = t.astype(BF16)
    else:
        for slot in range(KV_SLOTS):
            t = _dot(hb, wkv_ref[:, slot * LANES:(slot + 1) * LANES])
            if slot % 2 == 0:
                t = _rotary(t, cos, sin, 1)
            if slot < 4:
                nsa_ref[:, slot * LANES:(slot + 1) * LANES] = t
            else:
                win_ref[:, (slot - 4) * LANES:(slot - 3) * LANES] = t


def _inproj(x2d, g, w_qkv, w_kv, w_gate, tabs, tm, n_tab_blocks, batch, kv_transposed):
    rows, d = x2d.shape
    row_spec = lambda n: pl.BlockSpec((tm, n), lambda i: (i, 0))
    tab_spec = pl.BlockSpec((tm, LANES), lambda i: (i % n_tab_blocks, 0))
    in_specs = [row_spec(d), _const_spec((1, d)), _const_spec(w_qkv.shape), _const_spec(w_kv.shape),
                _const_spec(w_gate.shape), tab_spec, tab_spec]
    sds = jax.ShapeDtypeStruct
    interleaved = (sds((rows * DIFF_SLABS, LANES), F32), pl.BlockSpec((tm * DIFF_SLABS, LANES), lambda i: (i, 0)))
    if kv_transposed:
        seq = rows // batch
        nt = seq // tm
        tabt_spec = pl.BlockSpec((LANES, tm), lambda i: (0, i % nt))
        in_specs += [tabt_spec, tabt_spec]
        tspec = lambda n: pl.BlockSpec((1, n, tm), lambda i: (i // nt, 0, i % nt))
        outs = [
            (sds((rows, DIFF_WIDTH), BF16), row_spec(DIFF_WIDTH)),
            interleaved,
            (sds((rows, 2 * DIFF_WIDTH), BF16), row_spec(2 * DIFF_WIDTH)),
            (sds((rows, NSA_WIDTH), BF16), row_spec(NSA_WIDTH)),
            (sds((batch, 4 * LANES, seq), F32), tspec(4 * LANES)),
            (sds((batch, 2 * LANES, seq), F32), tspec(2 * LANES)),
            (sds((batch, 4 * LANES, seq), BF16), tspec(4 * LANES)),
            (sds((rows, LANES), F32), row_spec(LANES)),
        ]
    else:
        outs = [
            (sds((rows, DIFF_WIDTH), BF16), row_spec(DIFF_WIDTH)),
            interleaved,
            (sds((rows, NSA_WIDTH), BF16), row_spec(NSA_WIDTH)),
            (sds((rows, 4 * LANES), F32), row_spec(4 * LANES)),
            (sds((rows, 2 * LANES), F32), row_spec(2 * LANES)),
            (sds((rows, LANES), F32), row_spec(LANES)),
        ]
    return pl.pallas_call(
        functools.partial(_inproj_kernel, kv_transposed=kv_transposed),
        grid=(rows // tm,),
        in_specs=in_specs,
        out_specs=tuple(o[1] for o in outs),
        out_shape=tuple(o[0] for o in outs),
        compiler_params=_params("parallel"),
        name="inproj",
    )(x2d, g, w_qkv, w_kv, w_gate, *tabs)


def _lambda(lq1, lk1, lq2, lk2, lam_init):
    return (jnp.exp(jnp.sum(lq1[...] * lk1[...], axis=-1, keepdims=True))
            - jnp.exp(jnp.sum(lq2[...] * lk2[...], axis=-1, keepdims=True)) + lam_init)


def _split_components(q):
    lane = lax.broadcasted_iota(jnp.int32, q.shape, 1)
    low = lane < HEAD_DIM
    return jnp.concatenate([jnp.where(low, q, 0.0), jnp.where(low, 0.0, q)], axis=0).astype(BF16)


def _lane_tile(x, width):
    return jnp.concatenate([x] * (width // LANES), axis=1)


def _with_ones(v, axis):
    return jnp.concatenate([v, jnp.ones(v.shape, v.dtype)], axis=axis)


def _softmax_chunks(s_ref, p_ref, m_ref, a_ref, n_rows, adjust, unrolled):
    keys = s_ref.shape[1]

    def new_max(r0):
        rows = pl.ds(r0, SOFTMAX_ROWS)
        m_prev = m_ref[rows, :]
        m_new = jnp.maximum(m_prev, jnp.max(adjust(s_ref[rows, :], r0), axis=1, keepdims=True))
        a_ref[rows, :] = jnp.exp(m_prev - m_new)
        m_ref[rows, :] = m_new

    def exponentiate(r0):
        rows = pl.ds(r0, SOFTMAX_ROWS)
        p_ref[rows, :] = jnp.exp(adjust(s_ref[rows, :], r0) - _lane_tile(m_ref[rows, :], keys)).astype(BF16)

    for one_pass in (new_max, exponentiate):
        if unrolled:
            for c in range(n_rows // SOFTMAX_ROWS):
                one_pass(c * SOFTMAX_ROWS)
        else:
            def body(c, carry, one_pass=one_pass):
                one_pass(pl.multiple_of(c * SOFTMAX_ROWS, SOFTMAX_ROWS))
                return carry
            lax.fori_loop(0, n_rows // SOFTMAX_ROWS, body, 0)


def _diff_prompt_kernel(qi_ref, ki_ref, q_ref, k_ref, v_ref, lq1, lk1, lq2, lk2, g_ref, o_ref,
                        m_ref, a_ref, acc_ref, s_ref, p_ref, *, tile, lam_init):
    t = pl.program_id(1)
    qi = qi_ref[t]
    ki = ki_ref[t]
    n_rows = 2 * tile

    @pl.when(ki == 0)
    def _init():
        m_ref[...] = jnp.full(m_ref.shape, NEG, F32)
        acc_ref[...] = jnp.zeros(acc_ref.shape, F32)

    def causal(s, r0):
        row = r0 % tile + lax.broadcasted_iota(jnp.int32, s.shape, 0)
        return jnp.where(lax.broadcasted_iota(jnp.int32, s.shape, 1) <= row, s, NEG)

    def update(diagonal):
        for h in range(N_DIFF_HEADS):
            buf = h % 2
            q2 = _split_components(q_ref[:, h * LANES:(h + 1) * LANES].astype(F32))
            s_ref[buf] = _dot_nt(q2, k_ref[:, h * LANES:(h + 1) * LANES])
            _softmax_chunks(s_ref.at[buf], p_ref.at[buf], m_ref.at[h], a_ref.at[buf], n_rows,
                            causal if diagonal else (lambda s, r0: s), unrolled=True)
            acc_ref[h] = (_lane_tile(a_ref[buf], 2 * LANES) * acc_ref[h]
                          + _dot(p_ref[buf], _with_ones(v_ref[:, h * LANES:(h + 1) * LANES], 1)))

    @pl.when(ki < qi)
    def _below_diagonal():
        update(False)

    @pl.when(ki == qi)
    def _finish():
        update(True)
        lam = _lambda(lq1, lk1, lq2, lk2, lam_init)
        for h in range(N_DIFF_HEADS):
            o = acc_ref[h, :, 0:LANES] / acc_ref[h, :, LANES:2 * LANES]
            a = o[0:tile] - lam * o[tile:2 * tile]
            o_ref[:, h * LANES:(h + 1) * LANES] = (_rms(a) * g_ref[...] * (1.0 - lam_init)).astype(BF16)


def _diff_prompt(qd, kvdb, lams, subln, b, s, tile, lam_init):
    nq = s // tile
    pairs = [(qi, ki) for qi in range(nq) for ki in range(qi + 1)]
    qi_arr = jnp.asarray([p[0] for p in pairs], jnp.int32)
    ki_arr = jnp.asarray([p[1] for p in pairs], jnp.int32)
    grid_spec = pltpu.PrefetchScalarGridSpec(
        num_scalar_prefetch=2,
        grid=(b, len(pairs)),
        in_specs=[
            pl.BlockSpec((tile, DIFF_WIDTH), lambda bi, t, qa, ka: (bi * nq + qa[t], 0)),
            pl.BlockSpec((tile, DIFF_WIDTH), lambda bi, t, qa, ka: (bi * nq + ka[t], 0)),
            pl.BlockSpec((tile, DIFF_WIDTH), lambda bi, t, qa, ka: (bi * nq + ka[t], 1)),
        ] + [_const_spec((1, HEAD_DIM))] * 4 + [_const_spec((1, 2 * HEAD_DIM))],
        out_specs=pl.BlockSpec((tile, DIFF_WIDTH), lambda bi, t, qa, ka: (bi * nq + qa[t], 0)),
        scratch_shapes=[pltpu.VMEM((N_DIFF_HEADS, 2 * tile, LANES), F32),
                        pltpu.VMEM((2, 2 * tile, LANES), F32),
                        pltpu.VMEM((N_DIFF_HEADS, 2 * tile, 2 * LANES), F32),
                        pltpu.VMEM((2, 2 * tile, tile), F32),
                        pltpu.VMEM((2, 2 * tile, tile), BF16)],
    )
    return pl.pallas_call(
        functools.partial(_diff_prompt_kernel, tile=tile, lam_init=lam_init),
        grid_spec=grid_spec,
        out_shape=jax.ShapeDtypeStruct((b * s, DIFF_WIDTH), BF16),
        compiler_params=_params("parallel", "arbitrary"),
        name="diff_prompt",
    )(qi_arr, ki_arr, qd, kvdb, kvdb, *lams, subln)


def _diff_sample_kernel(pt_ref, *refs, n_pages, page, past, sq, lam_init):
    page_refs = refs[:n_pages]
    q_ref, new_ref, lq1, lk1, lq2, lk2, g_ref, o_ref = refs[n_pages:]
    n_keys = (n_pages + 1) * page
    kpos = lax.broadcasted_iota(jnp.int32, (2 * sq, n_keys), 1)
    tpos = past + lax.broadcasted_iota(jnp.int32, (2 * sq, n_keys), 0) % sq
    visible = kpos <= tpos
    lam = _lambda(lq1, lk1, lq2, lk2, lam_init)

    def slab(j):
        parts = [r[pl.ds(j, page, stride=DIFF_SLABS), :] for r in page_refs]
        parts.append(_pad_rows(new_ref[pl.ds(j, sq, stride=DIFF_SLABS), :], page))
        return jnp.concatenate(parts, axis=0).astype(BF16)

    for h in range(N_DIFF_HEADS):
        q2 = _split_components(q_ref[0, :, h * LANES:(h + 1) * LANES].astype(F32))
        s = jnp.where(visible, _dot_nt(q2, slab(h)), NEG)
        m = jnp.max(s, axis=1, keepdims=True)
        p = jnp.where(visible, jnp.exp(s - m), 0.0)
        o = _dot(p.astype(BF16), slab(N_DIFF_HEADS + h)) / jnp.maximum(jnp.sum(p, axis=1, keepdims=True), 1e-30)
        a = o[0:sq] - lam * o[sq:2 * sq]
        o_ref[0, :, h * LANES:(h + 1) * LANES] = (_rms(a) * g_ref[...] * (1.0 - lam_init)).astype(BF16)


def _diff_sample(cache_rows, page_table, qd3, kvd_rows, lams, subln, page, lam_init):
    db, sq, _ = qd3.shape
    n_pages = page_table.shape[1]
    past = n_pages * page
    page_specs = [pl.BlockSpec((page * DIFF_SLABS, LANES), functools.partial(lambda bi, pt, p: (pt[bi, p], 0), p=p))
                  for p in range(n_pages)]
    grid_spec = pltpu.PrefetchScalarGridSpec(
        num_scalar_prefetch=1,
        grid=(db,),
        in_specs=page_specs + [
            pl.BlockSpec((1, sq, DIFF_WIDTH), lambda bi, pt: (bi, 0, 0)),
            pl.BlockSpec((sq * DIFF_SLABS, LANES), lambda bi, pt: (bi, 0)),
        ] + [_const_spec((1, HEAD_DIM))] * 4 + [_const_spec((1, 2 * HEAD_DIM))],
        out_specs=pl.BlockSpec((1, sq, DIFF_WIDTH), lambda bi, pt: (bi, 0, 0)),
    )
    return pl.pallas_call(
        functools.partial(_diff_sample_kernel, n_pages=n_pages, page=page, past=past, sq=sq, lam_init=lam_init),
        grid_spec=grid_spec,
        out_shape=jax.ShapeDtypeStruct((db, sq, DIFF_WIDTH), BF16),
        compiler_params=_params("parallel"),
        name="diff_sample",
    )(page_table, *([cache_rows] * n_pages), qd3, kvd_rows, *lams, subln)


def _compress(rows_ref, slot, n_chunks, pos_a, pos_b, w_a, w_b, w2):
    xs = jnp.concatenate([rows_ref[slot, pl.ds(j, n_chunks, stride=CMP_STRIDE), :] for j in range(CMP_STRIDE)],
                         axis=1)
    a = _dot((xs + pos_a).astype(BF16), w_a)
    b = _dot((xs + pos_b).astype(BF16), w_b)
    hidden = jax.nn.gelu(a + pltpu.roll(b, n_chunks - 1, 0))
    out = _dot(hidden.astype(BF16), w2)
    rid = lax.broadcasted_iota(jnp.int32, out.shape, 0)
    return jnp.where(rid < n_chunks - 1, out, 0.0)


def _compress_prompt_kernel(x_ref, pa_ref, pb_ref, wa_ref, wb_ref, w2_ref, o_ref, rows_ref):
    seq = x_ref.shape[2]
    for slot in range(2):
        for c in range(seq // LANES):
            rows_ref[slot, c * LANES:(c + 1) * LANES, :] = x_ref[0, slot * LANES:(slot + 1) * LANES,
                                                                 c * LANES:(c + 1) * LANES].T
        o_ref[0, slot] = _compress(rows_ref, slot, seq // CMP_STRIDE, pa_ref[slot], pb_ref[slot], wa_ref[slot],
                                   wb_ref[slot], w2_ref[slot]).astype(BF16)


def _compress_prompt(nsa_t, cw):
    b, _, seq = nsa_t.shape
    nch = seq // CMP_STRIDE
    return pl.pallas_call(
        _compress_prompt_kernel,
        grid=(b,),
        in_specs=[pl.BlockSpec((1, 2 * LANES, seq), lambda i: (i, 0, 0))] + [_const_spec(a.shape) for a in cw],
        out_specs=pl.BlockSpec((1, 2, nch, LANES), lambda i: (i, 0, 0, 0)),
        out_shape=jax.ShapeDtypeStruct((b, 2, nch, LANES), BF16),
        scratch_shapes=[pltpu.VMEM((2, seq, LANES), F32)],
        compiler_params=_params("parallel"),
        name="compress_prompt",
    )(nsa_t, *cw)


def _cmp_probs(qs, kc, tpos):
    s = _dot_nt(qs, kc)
    cend = lax.broadcasted_iota(jnp.int32, s.shape, 1) * CMP_STRIDE + (CMP_BLOCK - 1)
    vis = cend <= tpos
    s = jnp.where(vis, s, NEG)
    m = jnp.max(s, axis=1, keepdims=True)
    p = jnp.where(vis, jnp.exp(s - m), 0.0)
    return (p / jnp.maximum(jnp.sum(p, axis=1, keepdims=True), 1e-30)).astype(BF16)


def _block_scores(imp, blk, cur):
    forced = (blk == 0) | (blk == cur) | (blk == cur - 1)
    return jnp.where(blk > cur, NEG, jnp.where(forced, BIG, imp))


def _select_blocks(imp, tpos2):
    blk = lax.broadcasted_iota(jnp.int32, imp.shape, 1)
    score = _block_scores(imp, blk, tpos2 // SEL_BLOCK)
    rank = jnp.zeros(imp.shape, F32)
    for j in range(SEL_LANES):
        sj = score[:, j:j + 1]
        tie = jnp.where(blk > j, 1.0, 0.0)
        rank = rank + jnp.where(sj > score, 1.0, 0.0) + jnp.where(sj == score, tie, 0.0)
    return jnp.where(rank < SEL_TOP_N, 1.0, 0.0).astype(BF16)


def _select_blocks_t(imp_t, tpos_t):
    n = imp_t.shape[1]
    blk = lax.broadcasted_iota(jnp.int32, imp_t.shape, 0)
    score = _block_scores(imp_t, blk, tpos_t // SEL_BLOCK)
    n_groups = SEL_LANES // SUBLANES
    groups = [score[g * SUBLANES:(g + 1) * SUBLANES] for g in range(n_groups)]
    ranks = [jnp.zeros((SUBLANES, n), F32) for _ in range(n_groups)]
    rid = lax.broadcasted_iota(jnp.int32, (SUBLANES, n), 0)
    for j in range(SEL_LANES):
        sj = score[j:j + 1]
        for g in range(n_groups):
            if g * SUBLANES > j:
                beats = jnp.where(sj >= groups[g], 1.0, 0.0)
            elif (g + 1) * SUBLANES - 1 < j:
                beats = jnp.where(sj > groups[g], 1.0, 0.0)
            else:
                beats = jnp.where(rid + g * SUBLANES > j, jnp.where(sj >= groups[g], 1.0, 0.0),
                                  jnp.where(sj > groups[g], 1.0, 0.0))
            ranks[g] = ranks[g] + beats
    sel_t = jnp.concatenate([jnp.where(r < SEL_TOP_N, 1.0, 0.0) for r in ranks]
                            + [jnp.zeros((LANES - SEL_LANES, n), F32)], axis=0)
    return sel_t.T.astype(BF16)


def _stack_queries(q, n_q):
    lane = lax.broadcasted_iota(jnp.int32, (n_q, LANES), 1)
    low = lane < HEAD_DIM
    groups = [q[:, g * LANES:(g + 1) * LANES].astype(F32) for g in range(NSA_GROUP)]
    return jnp.concatenate([jnp.where(low, t, 0.0) for t in groups] + [jnp.where(low, 0.0, t) for t in groups],
                           axis=0).astype(BF16)


def _combine(o_c, o_s, o_w, gates, n_q, store):
    lane = lax.broadcasted_iota(jnp.int32, (n_q, LANES), 1)
    low = lane < HEAD_DIM
    for g in range(NSA_GROUP):
        def pick(o):
            return jnp.where(low, o[g * n_q:(g + 1) * n_q], o[(NSA_GROUP + g) * n_q:(NSA_GROUP + g + 1) * n_q])

        def gate(j):
            c = j * N_NSA_HEADS + 2 * g
            return jnp.where(low, gates[:, c:c + 1], gates[:, c + 1:c + 2])

        store(g, gate(0) * pick(o_c) + gate(1) * pick(o_s) + gate(2) * pick(o_w))


def _per_kv_head(x, n_q):
    return x.reshape(N_NSA_KV_HEADS, NSA_GROUP, n_q, x.shape[-1])


def _nsa_prompt_kernel(qn_ref, gate_ref, cmp_ref, kv_ref, ovl_ref, exp_ref, o_ref,
                       m_ref, a_ref, acc_ref, s_ref, p_ref, sw_ref, pw_ref, bias_ref, *, tq, tk):
    q_start = pl.program_id(1) * tq
    m_rows = N_NSA_HEADS * tq
    qs = _stack_queries(qn_ref[...], tq)
    tpos = q_start + lax.broadcasted_iota(jnp.int32, (m_rows, 1), 0) % tq
    tpos2 = q_start + lax.broadcasted_iota(jnp.int32, (N_NSA_KV_HEADS * tq, 1), 0) % tq
    tpos2_t = q_start + lax.broadcasted_iota(jnp.int32, (1, N_NSA_KV_HEADS * tq), 1) % tq

    p_c = _cmp_probs(qs, cmp_ref[0, 0], tpos)
    o_c = _dot(p_c, cmp_ref[0, 1])
    imp_t = _dot_nt(ovl_ref[...], p_c)
    imp_t = jnp.concatenate(
        [(imp_t[:, (4 * h) * tq:(4 * h + 1) * tq] + imp_t[:, (4 * h + 1) * tq:(4 * h + 2) * tq])
         + (imp_t[:, (4 * h + 2) * tq:(4 * h + 3) * tq] + imp_t[:, (4 * h + 3) * tq:(4 * h + 4) * tq])
         for h in range(N_NSA_KV_HEADS)], axis=1)
    sel = _select_blocks_t(imp_t, tpos2_t)

    m_ref[...] = jnp.full(m_ref.shape, NEG, F32)
    acc_ref[...] = jnp.zeros(acc_ref.shape, F32)

    def slc_step(kt, carry):
        k0 = pl.multiple_of(kt * tk, tk)
        picked = _dot(sel, exp_ref[:, pl.ds(k0, tk)])
        kpos = k0 + lax.broadcasted_iota(jnp.int32, (1, tk), 1)
        bias_ref[...] = jnp.where(jnp.where(kpos <= tpos2, picked, 0.0) > 0.5, 0.0, NEG)
        s_ref[...] = _dot(qs, kv_ref[0, 0:LANES, pl.ds(k0, tk)])

        def biased(s, r0):
            b0 = r0 // (NSA_GROUP * tq) * tq + r0 % tq
            return s + bias_ref[b0:b0 + SOFTMAX_ROWS, :]

        _softmax_chunks(s_ref, p_ref, m_ref.at[0], a_ref, m_rows, biased, unrolled=True)
        acc_ref[...] = (_lane_tile(a_ref[...], 2 * LANES) * acc_ref[...]
                        + _dot_nt(p_ref[...], _with_ones(kv_ref[0, LANES:2 * LANES, pl.ds(k0, tk)], 0)))
        return carry

    lax.fori_loop(0, (q_start + tq - 1) // tk + 1, slc_step, 0)
    o_s = acc_ref[:, 0:LANES] / acc_ref[:, LANES:2 * LANES]

    span = WINDOW + tq
    w0 = pl.multiple_of(jnp.maximum(q_start - WINDOW, 0), tq)
    sw_ref[...] = _dot(qs, kv_ref[0, 2 * LANES:3 * LANES, pl.ds(w0, span)])

    def windowed(s, r0):
        qpos = q_start + r0 % tq + lax.broadcasted_iota(jnp.int32, s.shape, 0)
        dist = qpos - (w0 + lax.broadcasted_iota(jnp.int32, s.shape, 1))
        return jnp.where(lax.bitcast_convert_type(dist, jnp.uint32) < WINDOW, s, NEG)

    _softmax_chunks(sw_ref, pw_ref, m_ref.at[1], a_ref, m_rows, windowed, unrolled=True)
    o_w = _dot_nt(pw_ref[...], _with_ones(kv_ref[0, 3 * LANES:4 * LANES, pl.ds(w0, span)], 0))
    o_w = o_w[:, 0:LANES] / o_w[:, LANES:2 * LANES]

    def store(g, val):
        o_ref[:, g * LANES:(g + 1) * LANES] = val.astype(BF16)

    _combine(o_c, o_s, o_w, gate_ref[...], tq, store)


def _nsa_prompt(qn, gates, cmp_kv, kv_t, ovl_t, expand, b, s, tq, tk):
    nq = s // tq
    nch = cmp_kv.shape[2]
    m_rows = N_NSA_HEADS * tq
    return pl.pallas_call(
        functools.partial(_nsa_prompt_kernel, tq=tq, tk=tk),
        grid=(b, nq),
        in_specs=[
            pl.BlockSpec((tq, NSA_WIDTH), lambda bi, qi: (bi * nq + qi, 0)),
            pl.BlockSpec((tq, LANES), lambda bi, qi: (bi * nq + qi, 0)),
            pl.BlockSpec((1, 2, nch, LANES), lambda bi, qi: (bi, 0, 0, 0)),
            pl.BlockSpec((1, 4 * LANES, s), lambda bi, qi: (bi, 0, 0)),
            _const_spec(ovl_t.shape), _const_spec(expand.shape),
        ],
        out_specs=pl.BlockSpec((tq, NSA_WIDTH), lambda bi, qi: (bi * nq + qi, 0)),
        out_shape=jax.ShapeDtypeStruct((b * s, NSA_WIDTH), BF16),
        scratch_shapes=[pltpu.VMEM((2, m_rows, LANES), F32),
                        pltpu.VMEM((m_rows, LANES), F32),
                        pltpu.VMEM((m_rows, 2 * LANES), F32),
                        pltpu.VMEM((m_rows, tk), F32), pltpu.VMEM((m_rows, tk), BF16),
                        pltpu.VMEM((m_rows, WINDOW + tq), F32), pltpu.VMEM((m_rows, WINDOW + tq), BF16),
                        pltpu.VMEM((N_NSA_KV_HEADS * tq, tk), F32)],
        compiler_params=_params("parallel", "arbitrary"),
        name="nsa_prompt",
    )(qn, gates, cmp_kv, kv_t, ovl_t, expand)


def _nsa_sample_kernel(pt_ref, *refs, n_pages, page, past, sq):
    page_refs = refs[:n_pages]
    (win_ref, nsanew_ref, winnew_ref, qn_ref, gate_ref, pa_ref, pb_ref, wa_ref, wb_ref, w2_ref, ovl_ref, exp_ref,
     o_ref, wout_ref, rows_ref) = refs[n_pages:]
    m_rows = N_NSA_HEADS * sq
    qs = _stack_queries(qn_ref[0], sq)
    tpos = past + lax.broadcasted_iota(jnp.int32, (m_rows, 1), 0) % sq
    tpos2 = past + lax.broadcasted_iota(jnp.int32, (N_NSA_KV_HEADS * sq, 1), 0) % sq

    def page_t(i, slot):
        return page_refs[i][0, slot].reshape(N_NSA_KV_HEADS * HEAD_DIM, page)

    cmp_kv = []
    for slot in range(2):
        for i in range(n_pages):
            rows_ref[slot, i * page:(i + 1) * page, :] = page_t(i, slot).T
        cmp_kv.append(_compress(rows_ref, slot, past // CMP_STRIDE, pa_ref[slot], pb_ref[slot], wa_ref[slot],
                                wb_ref[slot], w2_ref[slot]).astype(BF16))
    p_c = _cmp_probs(qs, cmp_kv[0], tpos)
    o_c = _dot(p_c, cmp_kv[1])
    imp = _per_kv_head(_dot(p_c, ovl_ref[...]), sq)
    imp = ((imp[:, 0] + imp[:, 1]) + (imp[:, 2] + imp[:, 3])).reshape(N_NSA_KV_HEADS * sq, SEL_LANES)
    sel = _select_blocks(imp, tpos2)

    nsanew = _pad_rows(nsanew_ref[0], page)
    n_keys = (n_pages + 1) * page
    s = jnp.concatenate([_dot(qs, page_t(i, 2).astype(BF16)) for i in range(n_pages)]
                        + [_dot_nt(qs, nsanew[:, 2 * LANES:3 * LANES].astype(BF16))], axis=1)
    picked = _dot(sel, exp_ref[...])
    kpos = lax.broadcasted_iota(jnp.int32, (1, n_keys), 1)
    bias = jnp.where(jnp.where(kpos <= tpos2, picked, 0.0) > 0.5, 0.0, NEG)
    s = (_per_kv_head(s, sq) + bias.reshape(N_NSA_KV_HEADS, 1, sq, n_keys)).reshape(m_rows, n_keys)
    m = jnp.max(s, axis=1, keepdims=True)
    p = jnp.exp(s - m)
    l = jnp.sum(p, axis=1, keepdims=True)
    pb = p.astype(BF16)
    acc = _dot(pb[:, past:n_keys], nsanew[:, 3 * LANES:4 * LANES].astype(BF16))
    for i in range(n_pages):
        acc = acc + _dot_nt(pb[:, i * page:(i + 1) * page], page_t(i, 3).astype(BF16))
    o_s = acc / l

    winnew = _pad_rows(winnew_ref[0], page)
    s = jnp.concatenate([_dot(qs, win_ref[0, 0:LANES, :].astype(BF16)),
                         _dot_nt(qs, winnew[:, 0:LANES].astype(BF16))], axis=1)
    col = lax.broadcasted_iota(jnp.int32, (1, WINDOW + page), 1)
    dist = tpos - (past - WINDOW + col)
    vis = (dist >= 0) & (dist < WINDOW)
    s = jnp.where(vis, s, NEG)
    m = jnp.max(s, axis=1, keepdims=True)
    p = jnp.where(vis, jnp.exp(s - m), 0.0)
    pb = p.astype(BF16)
    o_w = ((_dot_nt(pb[:, 0:WINDOW], win_ref[0, LANES:2 * LANES, :].astype(BF16))
            + _dot(pb[:, WINDOW:WINDOW + page], winnew[:, LANES:2 * LANES].astype(BF16)))
           / jnp.maximum(jnp.sum(p, axis=1, keepdims=True), 1e-30))

    def store(g, val):
        o_ref[0, :, g * LANES:(g + 1) * LANES] = val.astype(BF16)

    _combine(o_c, o_s, o_w, gate_ref[0], sq, store)

    extended = jnp.concatenate([win_ref[0], winnew.T], axis=1)
    wout_ref[0] = extended[:, sq:sq + WINDOW]


def _nsa_sample(cache_t, page_table, win_t, nsanew3, winnew3, qn3, gates3, cw, ovl, expand):
    db, sq, _ = qn3.shape
    n_pages = page_table.shape[1]
    page = cache_t.shape[-1]
    past = n_pages * page
    page_specs = [pl.BlockSpec((1,) + cache_t.shape[1:], functools.partial(lambda bi, pt, p: (pt[bi, p], 0, 0, 0, 0), p=p))
                  for p in range(n_pages)]
    per_b = lambda shape: pl.BlockSpec((1,) + shape, lambda bi, pt: (bi, 0, 0))
    grid_spec = pltpu.PrefetchScalarGridSpec(
        num_scalar_prefetch=1,
        grid=(db,),
        in_specs=page_specs + [
            per_b((2 * LANES, WINDOW)), per_b((sq, 4 * LANES)), per_b((sq, 2 * LANES)), per_b((sq, NSA_WIDTH)),
            per_b((sq, LANES)),
        ] + [_const_spec(a.shape) for a in cw] + [_const_spec(ovl.shape), _const_spec(expand.shape)],
        out_specs=(per_b((sq, NSA_WIDTH)), per_b((2 * LANES, WINDOW))),
        scratch_shapes=[pltpu.VMEM((2, past, LANES), F32)],
    )
    return pl.pallas_call(
        functools.partial(_nsa_sample_kernel, n_pages=n_pages, page=page, past=past, sq=sq),
        grid_spec=grid_spec,
        out_shape=(jax.ShapeDtypeStruct((db, sq, NSA_WIDTH), BF16),
                   jax.ShapeDtypeStruct((db, 2 * LANES, WINDOW), F32)),
        compiler_params=_params("parallel"),
        name="nsa_sample",
    )(page_table, *([cache_t] * n_pages), win_t, nsanew3, winnew3, qn3, gates3, *cw, ovl, expand)


def _mlp_kernel(x_ref, od_ref, on_ref, wo_ref, wu_ref, wd_ref, g1_ref, g2_ref, g3_ref, y_ref, *, ff_chunk):
    mix = _dot(od_ref[...], wo_ref[0:DIFF_WIDTH, :]) + _dot(on_ref[...], wo_ref[DIFF_WIDTH:DIFF_WIDTH + NSA_WIDTH, :])
    x1 = x_ref[...] + _rms(mix) * g1_ref[...]
    hm = (_rms(x1) * g2_ref[...]).astype(BF16)
    d_ff = wu_ref.shape[1]
    ff = jnp.zeros(x1.shape, F32)
    for c in range(d_ff // ff_chunk):
        u = jnp.maximum(_dot(hm, wu_ref[:, c * ff_chunk:(c + 1) * ff_chunk]), 0.0)
        ff = ff + _dot((u * u).astype(BF16), wd_ref[c * ff_chunk:(c + 1) * ff_chunk, :])
    y_ref[...] = x1 + _rms(ff) * g3_ref[...]


def _mlp(x2d, od, on, w_out, w_up, w_down, g1, g2, g3, tm):
    rows, d = x2d.shape
    row_spec = lambda n: pl.BlockSpec((tm, n), lambda i: (i, 0))
    resident = lambda a: pl.BlockSpec(a.shape, lambda i: (0, 0), pipeline_mode=pl.Buffered(1))
    return pl.pallas_call(
        functools.partial(_mlp_kernel, ff_chunk=1024),
        grid=(rows // tm,),
        in_specs=[row_spec(d), row_spec(DIFF_WIDTH), row_spec(NSA_WIDTH), resident(w_out), resident(w_up),
                  resident(w_down), _const_spec((1, d)), _const_spec((1, d)), _const_spec((1, d))],
        out_specs=row_spec(d),
        out_shape=jax.ShapeDtypeStruct((rows, d), F32),
        compiler_params=_params("parallel"),
        name="mlp",
    )(x2d, od, on, w_out, w_up, w_down, g1, g2, g3)


def _rope_tables(pos, reps):
    half = HEAD_DIM // 2
    inv = ROPE_THETA ** (-jnp.arange(half, dtype=F32) / half)
    ang = pos.astype(F32)[:, None] * inv[None, :]
    cos, sin = jnp.cos(ang), jnp.sin(ang)
    cos_t = jnp.tile(cos, (reps, LANES // half))
    sin_t = jnp.tile(jnp.concatenate([-sin, sin], axis=1), (reps, LANES // HEAD_DIM))
    return cos_t, sin_t


def _compress_weights(cmp_pos, cmp_w1, cmp_w2):
    eye = jnp.eye(N_NSA_KV_HEADS, dtype=F32)
    w1 = cmp_w1.reshape(2, CMP_BLOCK, HEAD_DIM, CMP_HIDDEN)

    def expand_w1(w):
        t = jnp.einsum('sldf,hg->slhdgf', w, eye)
        return t.reshape(2, CMP_STRIDE * LANES, N_NSA_KV_HEADS * CMP_HIDDEN).astype(BF16)

    def expand_pos(p):
        return jnp.tile(p[:, :, None, :], (1, 1, N_NSA_KV_HEADS, 1)).reshape(2, 1, CMP_STRIDE * LANES)

    w2 = jnp.einsum('sfd,hg->shfgd', cmp_w2, eye).reshape(2, N_NSA_KV_HEADS * CMP_HIDDEN, LANES).astype(BF16)
    return (expand_pos(cmp_pos[:, :CMP_STRIDE]), expand_pos(cmp_pos[:, CMP_STRIDE:]),
            expand_w1(w1[:, :CMP_STRIDE]), expand_w1(w1[:, CMP_STRIDE:]), w2)


def _selection_constants(n_chunks, n_keys, expand_rows):
    n = np.arange(n_chunks)
    m = np.arange(SEL_LANES)
    cs, ss = n * CMP_STRIDE, m * SEL_BLOCK
    ovl = (cs[:, None] < ss[None, :] + SEL_BLOCK) & (cs[:, None] + CMP_BLOCK > ss[None, :]) & (n[:, None] < n_chunks - 1)
    expand = (np.arange(n_keys)[None, :] // SEL_BLOCK) == np.arange(expand_rows)[:, None]
    return ovl, jnp.asarray(expand, BF16)


def _row_tile(rows, want):
    t = min(rows, want)
    assert rows % t == 0
    return t


def kernel(x_prompt, x_sample, cache_diff_kv, cache_nsa_kv, state_nsa_win_kv, page_table, w_in, w_out, w_up, w_down,
           g_pre_mix, g_post_mix, g_pre_mlp, g_post_mlp, lam_q1, lam_k1, lam_q2, lam_k2, diff_subln, cmp_pos,
           cmp_w1, cmp_w2):
    depth = w_in.shape[0]
    assert depth == 1, "one layer: the sample group's paged caches are read in place"
    b, s, d = x_prompt.shape
    db, sq, _ = x_sample.shape
    n_phys, page = cache_diff_kv.shape[1:3]
    n_pages = page_table.shape[1]
    past = n_pages * page
    assert s % (4 * LANES) == 0 and s >= WINDOW + LANES and s <= SEL_LANES * SEL_BLOCK
    assert past >= WINDOW and sq < CMP_STRIDE and sq <= page and past + sq <= SEL_LANES * SEL_BLOCK
    assert state_nsa_win_kv.shape[2] == WINDOW and page == LANES
    layer = 0
    lam_init = 0.8 - 0.6 * math.exp(-0.3 * layer)

    wl = w_in[layer]
    qn0 = 3 * DIFF_WIDTH
    qn_cols = np.concatenate([qn0 + h * HEAD_DIM + np.arange(HEAD_DIM) for h in _HEAD_PERM])
    w_qkv = wl[:, np.concatenate([np.arange(qn0), qn_cols])].astype(BF16)
    w_kv = wl[:, QKV_WIDTH:QKV_WIDTH + KV_SLOTS * LANES].astype(BF16)
    gate0 = QKV_WIDTH + KV_SLOTS * LANES
    gate_cols = np.asarray([gate0 + h * 3 + j for j in range(3) for h in _HEAD_PERM])
    w_gate = jnp.pad(wl[:, gate_cols], ((0, 0), (0, LANES - N_GATES))).astype(BF16)
    out_rows = np.concatenate([np.arange(DIFF_WIDTH)] + [DIFF_WIDTH + h * HEAD_DIM + np.arange(HEAD_DIM) for h in _HEAD_PERM])
    wo = w_out[layer][out_rows].astype(BF16)
    wu = w_up[layer].astype(BF16)
    wd = w_down[layer].astype(BF16)
    vec = lambda a: a[layer].reshape(1, -1)
    lams = (vec(lam_q1), vec(lam_k1), vec(lam_q2), vec(lam_k2))
    cw = _compress_weights(cmp_pos[layer], cmp_w1[layer], cmp_w2[layer])

    rows_p = b * s
    tm_p = _row_tile(s, 256)
    cos_p, sin_p = _rope_tables(jnp.arange(s), 1)
    xp = x_prompt.reshape(rows_p, d)
    qd, kvd, kvdb, qn, nsa_t, win_t, kvb_t, gates = _inproj(
        xp, vec(g_pre_mix), w_qkv, w_kv.T, w_gate, (cos_p, sin_p, cos_p.T, sin_p.T), tm_p, s // tm_p, b, True)
    od = _diff_prompt(qd, kvdb, lams, vec(diff_subln), b, s, _row_tile(s, 512), lam_init)
    cmp_kv = _compress_prompt(nsa_t, cw)
    ovl_p, exp_p = _selection_constants(s // CMP_STRIDE, s, LANES)
    on = _nsa_prompt(qn, gates, cmp_kv, kvb_t, jnp.asarray(ovl_p.T, BF16), exp_p, b, s, LANES, 4 * LANES)
    yp = _mlp(xp, od, on, wo, wu, wd, vec(g_post_mix), vec(g_pre_mlp), vec(g_post_mlp), tm_p).reshape(b, s, d)
    p_diff = kvd.reshape(1, b, s, 2, N_DIFF_HEADS, 2 * HEAD_DIM)
    token_minor = lambda a, slots: jnp.transpose(
        a.reshape(a.shape[0], slots, N_NSA_KV_HEADS, HEAD_DIM, a.shape[-1]), (0, 4, 1, 2, 3))[None]
    p_nsa = token_minor(nsa_t, 4)
    p_win = token_minor(win_t[:, :, s - WINDOW:], 2)

    rows_s = db * sq
    tm_s = _row_tile(rows_s, 256)
    assert tm_s % sq == 0
    cos_s, sin_s = _rope_tables(past + jnp.arange(sq), tm_s // sq)
    xs = x_sample.reshape(rows_s, d)
    qd, kvd, qn, nsa4, win2, gates = _inproj(xs, vec(g_pre_mix), w_qkv, w_kv, w_gate, (cos_s, sin_s), tm_s, 1, db,
                                             False)
    r3 = lambda a: a.reshape(db, sq, a.shape[1])
    diff_rows = cache_diff_kv[layer].reshape(n_phys * page * DIFF_SLABS, LANES)
    od = _diff_sample(diff_rows, page_table, r3(qd), kvd, lams, vec(diff_subln), page, lam_init)
    nsa_cache_t = jnp.transpose(cache_nsa_kv[layer], (0, 2, 3, 4, 1))
    win_state_t = jnp.transpose(state_nsa_win_kv[layer], (0, 2, 3, 4, 1)).reshape(db, 2 * LANES, WINDOW)
    ovl_s, exp_s = _selection_constants(past // CMP_STRIDE, past + page, SEL_LANES)
    on, win_new_t = _nsa_sample(nsa_cache_t, page_table, win_state_t, r3(nsa4), r3(win2), r3(qn), r3(gates), cw,
                                jnp.asarray(ovl_s, BF16), exp_s)
    ys = _mlp(xs, od.reshape(rows_s, DIFF_WIDTH), on.reshape(rows_s, NSA_WIDTH), wo, wu, wd, vec(g_post_mix),
              vec(g_pre_mlp), vec(g_post_mlp), tm_s).reshape(db, sq, d)
    s_diff = kvd.reshape(1, db, sq, 2, N_DIFF_HEADS, 2 * HEAD_DIM)
    s_nsa = nsa4.reshape(1, db, sq, 4, N_NSA_KV_HEADS, HEAD_DIM)
    s_win = token_minor(win_new_t, 2)

    return yp, ys, p_diff, p_nsa, p_win, s_diff, s_nsa, s_win
```

```python
import functools
import math

import numpy as np
import jax
import jax.numpy as jnp
from jax import lax
from jax.experimental import pallas as pl
from jax.experimental.pallas import tpu as pltpu

HEAD_DIM = 64
N_DIFF_HEADS = 4
N_NSA_HEADS = 8
N_NSA_KV_HEADS = 2
NSA_GROUP = N_NSA_HEADS // N_NSA_KV_HEADS
CMP_BLOCK = 32
CMP_STRIDE = 16
CMP_HIDDEN = 128
SEL_BLOCK = 64
SEL_TOP_N = 16
WINDOW = 512
ROPE_THETA = 10000.0
NORM_EPS = 1e-6
DIFF_WIDTH = N_DIFF_HEADS * 2 * HEAD_DIM
NSA_WIDTH = N_NSA_HEADS * HEAD_DIM
N_GATES = 3 * N_NSA_HEADS
NEG = -1e9
BIG = 1e9
SCALE = HEAD_DIM ** -0.5

LANES = 128
SUBLANES = 8
SEL_LANES = 64
SOFTMAX_ROWS = 32
VMEM_LIMIT = 56 * 1024 * 1024
DIFF_SLABS = 2 * N_DIFF_HEADS
QKV_WIDTH = 3 * DIFF_WIDTH + NSA_WIDTH
KV_SLOTS = 6

F32 = jnp.float32
BF16 = jnp.bfloat16

_HEAD_PERM = tuple(g + NSA_GROUP * half for g in range(NSA_GROUP) for half in range(2))


def _dot(a, b):
    return jnp.dot(a, b, preferred_element_type=F32)


def _dot_nt(a, b):
    return lax.dot_general(a, b, (((1,), (1,)), ((), ())), preferred_element_type=F32)


def _rms(x):
    return x * lax.rsqrt(jnp.mean(x * x, axis=-1, keepdims=True) + NORM_EPS)


def _params(*sem):
    return pltpu.CompilerParams(dimension_semantics=sem, vmem_limit_bytes=VMEM_LIMIT)


def _const_spec(shape):
    nd = len(shape)
    return pl.BlockSpec(shape, lambda *_: (0,) * nd)


def _pad_rows(x, rows):
    return jnp.concatenate([x, jnp.zeros((rows - x.shape[0], x.shape[1]), x.dtype)], axis=0)


def _rotary(t, cos, sin, axis):
    idx = lax.broadcasted_iota(jnp.int32, t.shape, axis)
    first_half = (idx % HEAD_DIM) < HEAD_DIM // 2
    partner = jnp.where(first_half, pltpu.roll(t, LANES - HEAD_DIM // 2, axis), pltpu.roll(t, HEAD_DIM // 2, axis))
    return t * cos + partner * sin


def _inproj_kernel(*refs, kv_transposed):
    if kv_transposed:
        (x_ref, g_ref, w_ref, wkv_ref, wg_ref, cos_ref, sin_ref, cost_ref, sint_ref,
         qd_ref, kvd_ref, kvdb_ref, qn_ref, nsa_ref, win_ref, kvb_ref, gate_ref) = refs
    else:
        (x_ref, g_ref, w_ref, wkv_ref, wg_ref, cos_ref, sin_ref,
         qd_ref, kvd_ref, qn_ref, nsa_ref, win_ref, gate_ref) = refs
    tm = x_ref.shape[0]
    hb = (_rms(x_ref[...]) * g_ref[...]).astype(BF16)
    cos = cos_ref[...]
    sin = sin_ref[...]

    groups = [_dot(hb, w_ref[:, c0:c0 + DIFF_WIDTH]) for c0 in range(0, QKV_WIDTH, DIFF_WIDTH)]

    def proj(c0):
        return groups[c0 // DIFF_WIDTH][:, c0 % DIFF_WIDTH:c0 % DIFF_WIDTH + LANES]

    for j in range(N_DIFF_HEADS):
        qd_ref[:, j * LANES:(j + 1) * LANES] = (_rotary(proj(j * LANES), cos, sin, 1) * SCALE).astype(BF16)
        k = _rotary(proj(DIFF_WIDTH + j * LANES), cos, sin, 1)
        v = proj(2 * DIFF_WIDTH + j * LANES)
        kvd_ref[pl.ds(j, tm, stride=DIFF_SLABS), :] = k
        kvd_ref[pl.ds(N_DIFF_HEADS + j, tm, stride=DIFF_SLABS), :] = v
        if kv_transposed:
            kvdb_ref[:, j * LANES:(j + 1) * LANES] = k.astype(BF16)
            kvdb_ref[:, DIFF_WIDTH + j * LANES:DIFF_WIDTH + (j + 1) * LANES] = v.astype(BF16)
    for j in range(NSA_WIDTH // LANES):
        qn_ref[:, j * LANES:(j + 1) * LANES] = (_rotary(proj(3 * DIFF_WIDTH + j * LANES), cos, sin, 1)
                                                * SCALE).astype(BF16)
    gate_ref[...] = jax.nn.sigmoid(_dot(hb, wg_ref[...]))

    if kv_transposed:
        t_all = _dot_nt(wkv_ref[...], hb)
        cost = cost_ref[...]
        sint = sint_ref[...]
        for slot in range(KV_SLOTS):
            t = t_all[slot * LANES:(slot + 1) * LANES]
            if slot % 2 == 0:
                t = _rotary(t, cost, sint, 0)
            if slot < 4:
                nsa_ref[0, slot * LANES:(slot + 1) * LANES, :] = t
            else:
                win_ref[0, (slot - 4) * LANES:(slot - 3) * LANES, :] = t
            if slot >= 2:
                kvb_ref[0, (slot - 2) * LANES:(slot - 1) * LANES, :] = t.astype(BF16)
    else:
        t_all = _dot(hb, wkv_ref[...])
        for slot in range(KV_SLOTS):
            t = t_all[:, slot * LANES:(slot + 1) * LANES]
            if slot % 2 == 0:
                t = _rotary(t, cos, sin, 1)
            if slot < 4:
                nsa_ref[:, slot * LANES:(slot + 1) * LANES] = t
            else:
                win_ref[:, (slot - 4) * LANES:(slot - 3) * LANES] = t


def _inproj(x2d, g, w_qkv, w_kv, w_gate, tabs, tm, n_tab_blocks, batch, kv_transposed):
    rows, d = x2d.shape
    row_spec = lambda n: pl.BlockSpec((tm, n), lambda i: (i, 0))
    tab_spec = pl.BlockSpec((tm, LANES), lambda i: (i % n_tab_blocks, 0))
    in_specs = [row_spec(d), _const_spec((1, d)), _const_spec(w_qkv.shape), _const_spec(w_kv.shape),
                _const_spec(w_gate.shape), tab_spec, tab_spec]
    sds = jax.ShapeDtypeStruct
    interleaved = (sds((rows * DIFF_SLABS, LANES), F32), pl.BlockSpec((tm * DIFF_SLABS, LANES), lambda i: (i, 0)))
    if kv_transposed:
        seq = rows // batch
        nt = seq // tm
        tabt_spec = pl.BlockSpec((LANES, tm), lambda i: (0, i % nt))
        in_specs += [tabt_spec, tabt_spec]
        tspec = lambda n: pl.BlockSpec((1, n, tm), lambda i: (i // nt, 0, i % nt))
        outs = [
            (sds((rows, DIFF_WIDTH), BF16), row_spec(DIFF_WIDTH)),
            interleaved,
            (sds((rows, 2 * DIFF_WIDTH), BF16), row_spec(2 * DIFF_WIDTH)),
            (sds((rows, NSA_WIDTH), BF16), row_spec(NSA_WIDTH)),
            (sds((batch, 4 * LANES, seq), F32), tspec(4 * LANES)),
            (sds((batch, 2 * LANES, seq), F32), tspec(2 * LANES)),
            (sds((batch, 4 * LANES, seq), BF16), tspec(4 * LANES)),
            (sds((rows, LANES), F32), row_spec(LANES)),
        ]
    else:
        outs = [
            (sds((rows, DIFF_WIDTH), BF16), row_spec(DIFF_WIDTH)),
            interleaved,
            (sds((rows, NSA_WIDTH), BF16), row_spec(NSA_WIDTH)),
            (sds((rows, 4 * LANES), F32), row_spec(4 * LANES)),
            (sds((rows, 2 * LANES), F32), row_spec(2 * LANES)),
            (sds((rows, LANES), F32), row_spec(LANES)),
        ]
    return pl.pallas_call(
        functools.partial(_inproj_kernel, kv_transposed=kv_transposed),
        grid=(rows // tm,),
        in_specs=in_specs,
        out_specs=tuple(o[1] for o in outs),
        out_shape=tuple(o[0] for o in outs),
        compiler_params=_params("parallel"),
        name="inproj",
    )(x2d, g, w_qkv, w_kv, w_gate, *tabs)


def _lambda(lq1, lk1, lq2, lk2, lam_init):
    return (jnp.exp(jnp.sum(lq1[...] * lk1[...], axis=-1, keepdims=True))
            - jnp.exp(jnp.sum(lq2[...] * lk2[...], axis=-1, keepdims=True)) + lam_init)


def _split_components(q):
    lane = lax.broadcasted_iota(jnp.int32, q.shape, 1)
    low = lane < HEAD_DIM
    return jnp.concatenate([jnp.where(low, q, 0.0), jnp.where(low, 0.0, q)], axis=0).astype(BF16)


def _lane_tile(x, width):
    return jnp.concatenate([x] * (width // LANES), axis=1)


def _with_ones(v, axis):
    return jnp.concatenate([v, jnp.ones(v.shape, v.dtype)], axis=axis)


def _softmax_chunks(s_ref, p_ref, m_ref, a_ref, n_rows, adjust, unrolled):
    keys = s_ref.shape[1]

    def new_max(r0):
        rows = pl.ds(r0, SOFTMAX_ROWS)
        m_prev = m_ref[rows, :]
        m_new = jnp.maximum(m_prev, jnp.max(adjust(s_ref[rows, :], r0), axis=1, keepdims=True))
        a_ref[rows, :] = jnp.exp(m_prev - m_new)
        m_ref[rows, :] = m_new

    def exponentiate(r0):
        rows = pl.ds(r0, SOFTMAX_ROWS)
        p_ref[rows, :] = jnp.exp(adjust(s_ref[rows, :], r0) - _lane_tile(m_ref[rows, :], keys)).astype(BF16)

    for one_pass in (new_max, exponentiate):
        if unrolled:
            for c in range(n_rows // SOFTMAX_ROWS):
                one_pass(c * SOFTMAX_ROWS)
        else:
            def body(c, carry, one_pass=one_pass):
                one_pass(pl.multiple_of(c * SOFTMAX_ROWS, SOFTMAX_ROWS))
                return carry
            lax.fori_loop(0, n_rows // SOFTMAX_ROWS, body, 0)


def _diff_prompt_kernel(qi_ref, ki_ref, q_ref, k_ref, v_ref, lq1, lk1, lq2, lk2, g_ref, o_ref,
                        m_ref, a_ref, acc_ref, s_ref, p_ref, *, tile, lam_init):
    t = pl.program_id(1)
    qi = qi_ref[t]
    ki = ki_ref[t]
    n_rows = 2 * tile

    @pl.when(ki == 0)
    def _init():
        m_ref[...] = jnp.full(m_ref.shape, NEG, F32)
        acc_ref[...] = jnp.zeros(acc_ref.shape, F32)

    def causal(s, r0):
        row = r0 % tile + lax.broadcasted_iota(jnp.int32, s.shape, 0)
        return jnp.where(lax.broadcasted_iota(jnp.int32, s.shape, 1) <= row, s, NEG)

    def update(diagonal):
        for h in range(N_DIFF_HEADS):
            buf = h % 2
            q2 = _split_components(q_ref[:, h * LANES:(h + 1) * LANES].astype(F32))
            s_ref[buf] = _dot_nt(q2, k_ref[:, h * LANES:(h + 1) * LANES])
            _softmax_chunks(s_ref.at[buf], p_ref.at[buf], m_ref.at[h], a_ref.at[buf], n_rows,
                            causal if diagonal else (lambda s, r0: s), unrolled=True)
            acc_ref[h] = (_lane_tile(a_ref[buf], 2 * LANES) * acc_ref[h]
                          + _dot(p_ref[buf], _with_ones(v_ref[:, h * LANES:(h + 1) * LANES], 1)))

    @pl.when(ki < qi)
    def _below_diagonal():
        update(False)

    @pl.when(ki == qi)
    def _finish():
        update(True)
        lam = _lambda(lq1, lk1, lq2, lk2, lam_init)
        for h in range(N_DIFF_HEADS):
            o = acc_ref[h, :, 0:LANES] / acc_ref[h, :, LANES:2 * LANES]
            a = o[0:tile] - lam * o[tile:2 * tile]
            o_ref[:, h * LANES:(h + 1) * LANES] = (_rms(a) * g_ref[...] * (1.0 - lam_init)).astype(BF16)


def _diff_prompt(qd, kvdb, lams, subln, b, s, tile, lam_init):
    nq = s // tile
    pairs = [(qi, ki) for qi in range(nq) for ki in range(qi + 1)]
    qi_arr = jnp.asarray([p[0] for p in pairs], jnp.int32)
    ki_arr = jnp.asarray([p[1] for p in pairs], jnp.int32)
    grid_spec = pltpu.PrefetchScalarGridSpec(
        num_scalar_prefetch=2,
        grid=(b, len(pairs)),
        in_specs=[
            pl.BlockSpec((tile, DIFF_WIDTH), lambda bi, t, qa, ka: (bi * nq + qa[t], 0)),
            pl.BlockSpec((tile, DIFF_WIDTH), lambda bi, t, qa, ka: (bi * nq + ka[t], 0)),
            pl.BlockSpec((tile, DIFF_WIDTH), lambda bi, t, qa, ka: (bi * nq + ka[t], 1)),
        ] + [_const_spec((1, HEAD_DIM))] * 4 + [_const_spec((1, 2 * HEAD_DIM))],
        out_specs=pl.BlockSpec((tile, DIFF_WIDTH), lambda bi, t, qa, ka: (bi * nq + qa[t], 0)),
        scratch_shapes=[pltpu.VMEM((N_DIFF_HEADS, 2 * tile, LANES), F32),
                        pltpu.VMEM((2, 2 * tile, LANES), F32),
                        pltpu.VMEM((N_DIFF_HEADS, 2 * tile, 2 * LANES), F32),
                        pltpu.VMEM((2, 2 * tile, tile), F32),
                        pltpu.VMEM((2, 2 * tile, tile), BF16)],
    )
    return pl.pallas_call(
        functools.partial(_diff_prompt_kernel, tile=tile, lam_init=lam_init),
        grid_spec=grid_spec,
        out_shape=jax.ShapeDtypeStruct((b * s, DIFF_WIDTH), BF16),
        compiler_params=_params("parallel", "arbitrary"),
        name="diff_prompt",
    )(qi_arr, ki_arr, qd, kvdb, kvdb, *lams, subln)


def _diff_sample_kernel(pt_ref, *refs, n_pages, page, past, sq, lam_init):
    page_refs = refs[:n_pages]
    q_ref, new_ref, lq1, lk1, lq2, lk2, g_ref, o_ref = refs[n_pages:]
    n_keys = (n_pages + 1) * page
    kpos = lax.broadcasted_iota(jnp.int32, (2 * sq, n_keys), 1)
    tpos = past + lax.broadcasted_iota(jnp.int32, (2 * sq, n_keys), 0) % sq
    visible = kpos <= tpos
    lam = _lambda(lq1, lk1, lq2, lk2, lam_init)

    def slab(j):
        parts = [r[pl.ds(j, page, stride=DIFF_SLABS), :] for r in page_refs]
        parts.append(_pad_rows(new_ref[pl.ds(j, sq, stride=DIFF_SLABS), :], page))
        return jnp.concatenate(parts, axis=0).astype(BF16)

    for h in range(N_DIFF_HEADS):
        q2 = _split_components(q_ref[0, :, h * LANES:(h + 1) * LANES].astype(F32))
        s = jnp.where(visible, _dot_nt(q2, slab(h)), NEG)
        m = jnp.max(s, axis=1, keepdims=True)
        p = jnp.where(visible, jnp.exp(s - m), 0.0)
        o = _dot(p.astype(BF16), slab(N_DIFF_HEADS + h)) / jnp.maximum(jnp.sum(p, axis=1, keepdims=True), 1e-30)
        a = o[0:sq] - lam * o[sq:2 * sq]
        o_ref[0, :, h * LANES:(h + 1) * LANES] = (_rms(a) * g_ref[...] * (1.0 - lam_init)).astype(BF16)


def _diff_sample(cache_rows, page_table, qd3, kvd_rows, lams, subln, page, lam_init):
    db, sq, _ = qd3.shape
    n_pages = page_table.shape[1]
    past = n_pages * page
    page_specs = [pl.BlockSpec((page * DIFF_SLABS, LANES), functools.partial(lambda bi, pt, p: (pt[bi, p], 0), p=p))
                  for p in range(n_pages)]
    grid_spec = pltpu.PrefetchScalarGridSpec(
        num_scalar_prefetch=1,
        grid=(db,),
        in_specs=page_specs + [
            pl.BlockSpec((1, sq, DIFF_WIDTH), lambda bi, pt: (bi, 0, 0)),
            pl.BlockSpec((sq * DIFF_SLABS, LANES), lambda bi, pt: (bi, 0)),
        ] + [_const_spec((1, HEAD_DIM))] * 4 + [_const_spec((1, 2 * HEAD_DIM))],
        out_specs=pl.BlockSpec((1, sq, DIFF_WIDTH), lambda bi, pt: (bi, 0, 0)),
    )
    return pl.pallas_call(
        functools.partial(_diff_sample_kernel, n_pages=n_pages, page=page, past=past, sq=sq, lam_init=lam_init),
        grid_spec=grid_spec,
        out_shape=jax.ShapeDtypeStruct((db, sq, DIFF_WIDTH), BF16),
        compiler_params=_params("parallel"),
        name="diff_sample",
    )(page_table, *([cache_rows] * n_pages), qd3, kvd_rows, *lams, subln)


def _compress(rows_ref, slot, n_chunks, pos_a, pos_b, w_a, w_b, w2):
    xs = jnp.concatenate([rows_ref[slot, pl.ds(j, n_chunks, stride=CMP_STRIDE), :] for j in range(CMP_STRIDE)],
                         axis=1)
    a = _dot((xs + pos_a).astype(BF16), w_a)
    b = _dot((xs + pos_b).astype(BF16), w_b)
    hidden = jax.nn.gelu(a + pltpu.roll(b, n_chunks - 1, 0))
    out = _dot(hidden.astype(BF16), w2)
    rid = lax.broadcasted_iota(jnp.int32, out.shape, 0)
    return jnp.where(rid < n_chunks - 1, out, 0.0)


def _compress_prompt_kernel(x_ref, pa_ref, pb_ref, wa_ref, wb_ref, w2_ref, o_ref, rows_ref):
    seq = x_ref.shape[2]
    for slot in range(2):
        for c in range(seq // LANES):
            rows_ref[slot, c * LANES:(c + 1) * LANES, :] = x_ref[0, slot * LANES:(slot + 1) * LANES,
                                                                 c * LANES:(c + 1) * LANES].T
        o_ref[0, slot] = _compress(rows_ref, slot, seq // CMP_STRIDE, pa_ref[slot], pb_ref[slot], wa_ref[slot],
                                   wb_ref[slot], w2_ref[slot]).astype(BF16)


def _compress_prompt(nsa_t, cw):
    b, _, seq = nsa_t.shape
    nch = seq // CMP_STRIDE
    return pl.pallas_call(
        _compress_prompt_kernel,
        grid=(b,),
        in_specs=[pl.BlockSpec((1, 2 * LANES, seq), lambda i: (i, 0, 0))] + [_const_spec(a.shape) for a in cw],
        out_specs=pl.BlockSpec((1, 2, nch, LANES), lambda i: (i, 0, 0, 0)),
        out_shape=jax.ShapeDtypeStruct((b, 2, nch, LANES), BF16),
        scratch_shapes=[pltpu.VMEM((2, seq, LANES), F32)],
        compiler_params=_params("parallel"),
        name="compress_prompt",
    )(nsa_t, *cw)


def _cmp_probs(qs, kc, tpos):
    s = _dot_nt(qs, kc)
    cend = lax.broadcasted_iota(jnp.int32, s.shape, 1) * CMP_STRIDE + (CMP_BLOCK - 1)
    vis = cend <= tpos
    s = jnp.where(vis, s, NEG)
    m = jnp.max(s, axis=1, keepdims=True)
    p = jnp.where(vis, jnp.exp(s - m), 0.0)
    return (p / jnp.maximum(jnp.sum(p, axis=1, keepdims=True), 1e-30)).astype(BF16)


def _block_scores(imp, blk, cur):
    forced = (blk == 0) | (blk == cur) | (blk == cur - 1)
    return jnp.where(blk > cur, NEG, jnp.where(forced, BIG, imp))


def _select_blocks(imp, tpos2):
    blk = lax.broadcasted_iota(jnp.int32, imp.shape, 1)
    score = _block_scores(imp, blk, tpos2 // SEL_BLOCK)
    rank = jnp.zeros(imp.shape, F32)
    for j in range(SEL_LANES):
        sj = score[:, j:j + 1]
        tie = jnp.where(blk > j, 1.0, 0.0)
        rank = rank + jnp.where(sj > score, 1.0, 0.0) + jnp.where(sj == score, tie, 0.0)
    return jnp.where(rank < SEL_TOP_N, 1.0, 0.0).astype(BF16)


def _select_blocks_t(imp_t, tpos_t):
    n = imp_t.shape[1]
    blk = lax.broadcasted_iota(jnp.int32, imp_t.shape, 0)
    score = _block_scores(imp_t, blk, tpos_t // SEL_BLOCK)
    n_groups = SEL_LANES // SUBLANES
    groups = [score[g * SUBLANES:(g + 1) * SUBLANES] for g in range(n_groups)]
    ranks = [jnp.zeros((SUBLANES, n), F32) for _ in range(n_groups)]
    rid = lax.broadcasted_iota(jnp.int32, (SUBLANES, n), 0)
    for j in range(SEL_LANES):
        sj = score[j:j + 1]
        for g in range(n_groups):
            if g * SUBLANES > j:
                beats = jnp.where(sj >= groups[g], 1.0, 0.0)
            elif (g + 1) * SUBLANES - 1 < j:
                beats = jnp.where(sj > groups[g], 1.0, 0.0)
            else:
                beats = jnp.where(rid + g * SUBLANES > j, jnp.where(sj >= groups[g], 1.0, 0.0),
                                  jnp.where(sj > groups[g], 1.0, 0.0))
            ranks[g] = ranks[g] + beats
    sel_t = jnp.concatenate([jnp.where(r < SEL_TOP_N, 1.0, 0.0) for r in ranks]
                            + [jnp.zeros((LANES - SEL_LANES, n), F32)], axis=0)
    return sel_t.T


def _stack_queries(q, n_q):
    lane = lax.broadcasted_iota(jnp.int32, (n_q, LANES), 1)
    low = lane < HEAD_DIM
    groups = [q[:, g * LANES:(g + 1) * LANES].astype(F32) for g in range(NSA_GROUP)]
    return jnp.concatenate([jnp.where(low, t, 0.0) for t in groups] + [jnp.where(low, 0.0, t) for t in groups],
                           axis=0).astype(BF16)


def _combine(o_c, o_s, o_w, gates, n_q, store):
    lane = lax.broadcasted_iota(jnp.int32, (n_q, LANES), 1)
    low = lane < HEAD_DIM
    for g in range(NSA_GROUP):
        def pick(o):
            return jnp.where(low, o[g * n_q:(g + 1) * n_q], o[(NSA_GROUP + g) * n_q:(NSA_GROUP + g + 1) * n_q])

        def gate(j):
            c = j * N_NSA_HEADS + 2 * g
            return jnp.where(low, gates[:, c:c + 1], gates[:, c + 1:c + 2])

        store(g, gate(0) * pick(o_c) + gate(1) * pick(o_s) + gate(2) * pick(o_w))


def _per_kv_head(x, n_q):
    return x.reshape(N_NSA_KV_HEADS, NSA_GROUP, n_q, x.shape[-1])


def _nsa_prompt_kernel(qn_ref, gate_ref, cmp_ref, kv_ref, ovl_ref, exp_ref, o_ref,
                       m_ref, a_ref, acc_ref, s_ref, p_ref, sw_ref, pw_ref, sc_ref, pc_ref, qse_ref, *, tq, tk):
    q_start = pl.program_id(1) * tq
    m_rows = N_NSA_HEADS * tq
    qse_ref[:, 0:LANES] = _stack_queries(qn_ref[...], tq)
    tpos2_t = q_start + lax.broadcasted_iota(jnp.int32, (1, N_NSA_KV_HEADS * tq), 1) % tq
    m_ref[...] = jnp.full(m_ref.shape, NEG, F32)
    acc_ref[...] = jnp.zeros(acc_ref.shape, F32)

    m_ref[2] = jnp.full(m_ref.shape[1:], 0.5 * NEG, F32)
    sc_ref[...] = _dot_nt(qse_ref[:, 0:LANES], cmp_ref[0, 0])

    def block_ended(s, r0):
        qpos = q_start + r0 % tq + lax.broadcasted_iota(jnp.int32, s.shape, 0)
        cend = lax.broadcasted_iota(jnp.int32, s.shape, 1) * CMP_STRIDE + (CMP_BLOCK - 1)
        return jnp.where(cend <= qpos, s, NEG)

    _softmax_chunks(sc_ref, pc_ref, m_ref.at[2], a_ref, m_rows, block_ended, unrolled=True)
    o_c = _dot(pc_ref[...], _with_ones(cmp_ref[0, 1], 1))
    o_c = o_c[:, 0:LANES] / jnp.maximum(o_c[:, LANES:2 * LANES], 1e-30)
    imp_t = _dot_nt(ovl_ref[...], pc_ref[...])
    imp_t = imp_t[0:SEL_LANES] / jnp.maximum(imp_t[SEL_LANES:SEL_LANES + 1], 1e-30)
    imp_t = jnp.concatenate(
        [(imp_t[:, (4 * h) * tq:(4 * h + 1) * tq] + imp_t[:, (4 * h + 1) * tq:(4 * h + 2) * tq])
         + (imp_t[:, (4 * h + 2) * tq:(4 * h + 3) * tq] + imp_t[:, (4 * h + 3) * tq:(4 * h + 4) * tq])
         for h in range(N_NSA_KV_HEADS)], axis=1)
    sel = _select_blocks_t(imp_t, tpos2_t)

    block_bias = jnp.where(sel > 0.5, 0.0, NEG).astype(BF16)
    for r in range(N_NSA_HEADS):
        h = r // NSA_GROUP
        qse_ref[r * tq:(r + 1) * tq, LANES:2 * LANES] = block_bias[h * tq:(h + 1) * tq]

    def slc_tile(kt, diagonal):
        k0 = pl.multiple_of(kt * tk, tk)
        keys_t = jnp.concatenate([kv_ref[0, 0:LANES, pl.ds(k0, tk)], exp_ref[:, pl.ds(k0, tk)]], axis=0)
        s_ref[...] = _dot(qse_ref[...], keys_t)

        def causal(s, r0):
            qpos = q_start + r0 % tq + lax.broadcasted_iota(jnp.int32, s.shape, 0)
            return jnp.where(k0 + lax.broadcasted_iota(jnp.int32, s.shape, 1) <= qpos, s, NEG)

        _softmax_chunks(s_ref, p_ref, m_ref.at[0], a_ref, m_rows, causal if diagonal else (lambda s, r0: s),
                        unrolled=True)
        acc_ref[...] = (_lane_tile(a_ref[...], 2 * LANES) * acc_ref[...]
                        + _dot_nt(p_ref[...], _with_ones(kv_ref[0, LANES:2 * LANES, pl.ds(k0, tk)], 0)))

    def below_diagonal(kt, carry):
        slc_tile(kt, False)
        return carry

    last_tile = (q_start + tq - 1) // tk
    lax.fori_loop(0, last_tile, below_diagonal, 0)
    slc_tile(last_tile, True)
    o_s = acc_ref[:, 0:LANES] / acc_ref[:, LANES:2 * LANES]

    span = WINDOW + tq
    w0 = pl.multiple_of(jnp.maximum(q_start - WINDOW, 0), tq)
    sw_ref[...] = _dot(qse_ref[:, 0:LANES], kv_ref[0, 2 * LANES:3 * LANES, pl.ds(w0, span)])

    def windowed(s, r0):
        qpos = q_start + r0 % tq + lax.broadcasted_iota(jnp.int32, s.shape, 0)
        dist = qpos - (w0 + lax.broadcasted_iota(jnp.int32, s.shape, 1))
        return jnp.where(lax.bitcast_convert_type(dist, jnp.uint32) < WINDOW, s, NEG)

    _softmax_chunks(sw_ref, pw_ref, m_ref.at[1], a_ref, m_rows, windowed, unrolled=True)
    o_w = _dot_nt(pw_ref[...], _with_ones(kv_ref[0, 3 * LANES:4 * LANES, pl.ds(w0, span)], 0))
    o_w = o_w[:, 0:LANES] / o_w[:, LANES:2 * LANES]

    def store(g, val):
        o_ref[:, g * LANES:(g + 1) * LANES] = val.astype(BF16)

    _combine(o_c, o_s, o_w, gate_ref[...], tq, store)


def _nsa_prompt(qn, gates, cmp_kv, kv_t, ovl_t, expand, b, s, tq, tk):
    nq = s // tq
    nch = cmp_kv.shape[2]
    m_rows = N_NSA_HEADS * tq
    return pl.pallas_call(
        functools.partial(_nsa_prompt_kernel, tq=tq, tk=tk),
        grid=(b, nq),
        in_specs=[
            pl.BlockSpec((tq, NSA_WIDTH), lambda bi, qi: (bi * nq + qi, 0)),
            pl.BlockSpec((tq, LANES), lambda bi, qi: (bi * nq + qi, 0)),
            pl.BlockSpec((1, 2, nch, LANES), lambda bi, qi: (bi, 0, 0, 0)),
            pl.BlockSpec((1, 4 * LANES, s), lambda bi, qi: (bi, 0, 0)),
            _const_spec(ovl_t.shape), _const_spec(expand.shape),
        ],
        out_specs=pl.BlockSpec((tq, NSA_WIDTH), lambda bi, qi: (bi * nq + qi, 0)),
        out_shape=jax.ShapeDtypeStruct((b * s, NSA_WIDTH), BF16),
        scratch_shapes=[pltpu.VMEM((3, m_rows, LANES), F32),
                        pltpu.VMEM((m_rows, LANES), F32),
                        pltpu.VMEM((m_rows, 2 * LANES), F32),
                        pltpu.VMEM((m_rows, tk), F32), pltpu.VMEM((m_rows, tk), BF16),
                        pltpu.VMEM((m_rows, WINDOW + tq), F32), pltpu.VMEM((m_rows, WINDOW + tq), BF16),
                        pltpu.VMEM((m_rows, nch), F32), pltpu.VMEM((m_rows, nch), BF16),
                        pltpu.VMEM((m_rows, 2 * LANES), BF16)],
        compiler_params=_params("parallel", "arbitrary"),
        name="nsa_prompt",
    )(qn, gates, cmp_kv, kv_t, ovl_t, expand)


def _nsa_sample_kernel(pt_ref, *refs, n_pages, page, past, sq):
    page_refs = refs[:n_pages]
    (win_ref, nsanew_ref, winnew_ref, qn_ref, gate_ref, pa_ref, pb_ref, wa_ref, wb_ref, w2_ref, ovl_ref, exp_ref,
     o_ref, wout_ref, rows_ref) = refs[n_pages:]
    m_rows = N_NSA_HEADS * sq
    qs = _stack_queries(qn_ref[0], sq)
    tpos = past + lax.broadcasted_iota(jnp.int32, (m_rows, 1), 0) % sq
    tpos2 = past + lax.broadcasted_iota(jnp.int32, (N_NSA_KV_HEADS * sq, 1), 0) % sq

    def page_t(i, slot):
        return page_refs[i][0, slot].reshape(N_NSA_KV_HEADS * HEAD_DIM, page)

    cmp_kv = []
    for slot in range(2):
        for i in range(n_pages):
            rows_ref[slot, i * page:(i + 1) * page, :] = page_t(i, slot).T
        cmp_kv.append(_compress(rows_ref, slot, past // CMP_STRIDE, pa_ref[slot], pb_ref[slot], wa_ref[slot],
                                wb_ref[slot], w2_ref[slot]).astype(BF16))
    p_c = _cmp_probs(qs, cmp_kv[0], tpos)
    o_c = _dot(p_c, cmp_kv[1])
    imp = _per_kv_head(_dot(p_c, ovl_ref[...]), sq)
    imp = ((imp[:, 0] + imp[:, 1]) + (imp[:, 2] + imp[:, 3])).reshape(N_NSA_KV_HEADS * sq, SEL_LANES)
    sel = _select_blocks(imp, tpos2)

    nsanew = _pad_rows(nsanew_ref[0], page)
    n_keys = (n_pages + 1) * page
    s = jnp.concatenate([_dot(qs, page_t(i, 2).astype(BF16)) for i in range(n_pages)]
                        + [_dot_nt(qs, nsanew[:, 2 * LANES:3 * LANES].astype(BF16))], axis=1)
    picked = _dot(sel, exp_ref[...])
    kpos = lax.broadcasted_iota(jnp.int32, (1, n_keys), 1)
    bias = jnp.where(jnp.where(kpos <= tpos2, picked, 0.0) > 0.5, 0.0, NEG)
    s = (_per_kv_head(s, sq) + bias.reshape(N_NSA_KV_HEADS, 1, sq, n_keys)).reshape(m_rows, n_keys)
    m = jnp.max(s, axis=1, keepdims=True)
    p = jnp.exp(s - m)
    l = jnp.sum(p, axis=1, keepdims=True)
    pb = p.astype(BF16)
    acc = _dot(pb[:, past:n_keys], nsanew[:, 3 * LANES:4 * LANES].astype(BF16))
    for i in range(n_pages):
        acc = acc + _dot_nt(pb[:, i * page:(i + 1) * page], page_t(i, 3).astype(BF16))
    o_s = acc / l

    winnew = _pad_rows(winnew_ref[0], page)
    s = jnp.concatenate([_dot(qs, win_ref[0, 0:LANES, :].astype(BF16)),
                         _dot_nt(qs, winnew[:, 0:LANES].astype(BF16))], axis=1)
    col = lax.broadcasted_iota(jnp.int32, (1, WINDOW + page), 1)
    dist = tpos - (past - WINDOW + col)
    vis = (dist >= 0) & (dist < WINDOW)
    s = jnp.where(vis, s, NEG)
    m = jnp.max(s, axis=1, keepdims=True)
    p = jnp.where(vis, jnp.exp(s - m), 0.0)
    pb = p.astype(BF16)
    o_w = ((_dot_nt(pb[:, 0:WINDOW], win_ref[0, LANES:2 * LANES, :].astype(BF16))
            + _dot(pb[:, WINDOW:WINDOW + page], winnew[:, LANES:2 * LANES].astype(BF16)))
           / jnp.maximum(jnp.sum(p, axis=1, keepdims=True), 1e-30))

    def store(g, val):
        o_ref[0, :, g * LANES:(g + 1) * LANES] = val.astype(BF16)

    _combine(o_c, o_s, o_w, gate_ref[0], sq, store)

    extended = jnp.concatenate([win_ref[0], winnew.T], axis=1)
    wout_ref[0] = extended[:, sq:sq + WINDOW]


def _nsa_sample(cache_t, page_table, win_t, nsanew3, winnew3, qn3, gates3, cw, ovl, expand):
    db, sq, _ = qn3.shape
    n_pages = page_table.shape[1]
    page = cache_t.shape[-1]
    past = n_pages * page
    page_specs = [pl.BlockSpec((1,) + cache_t.shape[1:], functools.partial(lambda bi, pt, p: (pt[bi, p], 0, 0, 0, 0), p=p))
                  for p in range(n_pages)]
    per_b = lambda shape: pl.BlockSpec((1,) + shape, lambda bi, pt: (bi, 0, 0))
    grid_spec = pltpu.PrefetchScalarGridSpec(
        num_scalar_prefetch=1,
        grid=(db,),
        in_specs=page_specs + [
            per_b((2 * LANES, WINDOW)), per_b((sq, 4 * LANES)), per_b((sq, 2 * LANES)), per_b((sq, NSA_WIDTH)),
            per_b((sq, LANES)),
        ] + [_const_spec(a.shape) for a in cw] + [_const_spec(ovl.shape), _const_spec(expand.shape)],
        out_specs=(per_b((sq, NSA_WIDTH)), per_b((2 * LANES, WINDOW))),
        scratch_shapes=[pltpu.VMEM((2, past, LANES), F32)],
    )
    return pl.pallas_call(
        functools.partial(_nsa_sample_kernel, n_pages=n_pages, page=page, past=past, sq=sq),
        grid_spec=grid_spec,
        out_shape=(jax.ShapeDtypeStruct((db, sq, NSA_WIDTH), BF16),
                   jax.ShapeDtypeStruct((db, 2 * LANES, WINDOW), F32)),
        compiler_params=_params("parallel"),
        name="nsa_sample",
    )(page_table, *([cache_t] * n_pages), win_t, nsanew3, winnew3, qn3, gates3, *cw, ovl, expand)


def _mlp_kernel(x_ref, od_ref, on_ref, wo_ref, wu_ref, wd_ref, g1_ref, g2_ref, g3_ref, y_ref, *, ff_chunk):
    mix = _dot(od_ref[...], wo_ref[0:DIFF_WIDTH, :]) + _dot(on_ref[...], wo_ref[DIFF_WIDTH:DIFF_WIDTH + NSA_WIDTH, :])
    x1 = x_ref[...] + _rms(mix) * g1_ref[...]
    hm = (_rms(x1) * g2_ref[...]).astype(BF16)
    d_ff = wu_ref.shape[1]
    ff = jnp.zeros(x1.shape, F32)
    for c in range(d_ff // ff_chunk):
        u = jnp.maximum(_dot(hm, wu_ref[:, c * ff_chunk:(c + 1) * ff_chunk]), 0.0)
        ff = ff + _dot((u * u).astype(BF16), wd_ref[c * ff_chunk:(c + 1) * ff_chunk, :])
    y_ref[...] = x1 + _rms(ff) * g3_ref[...]


def _mlp(x2d, od, on, w_out, w_up, w_down, g1, g2, g3, tm):
    rows, d = x2d.shape
    row_spec = lambda n: pl.BlockSpec((tm, n), lambda i: (i, 0))
    resident = lambda a: pl.BlockSpec(a.shape, lambda i: (0, 0), pipeline_mode=pl.Buffered(1))
    return pl.pallas_call(
        functools.partial(_mlp_kernel, ff_chunk=1024),
        grid=(rows // tm,),
        in_specs=[row_spec(d), row_spec(DIFF_WIDTH), row_spec(NSA_WIDTH), resident(w_out), resident(w_up),
                  resident(w_down), _const_spec((1, d)), _const_spec((1, d)), _const_spec((1, d))],
        out_specs=row_spec(d),
        out_shape=jax.ShapeDtypeStruct((rows, d), F32),
        compiler_params=_params("parallel"),
        name="mlp",
    )(x2d, od, on, w_out, w_up, w_down, g1, g2, g3)


def _rope_tables(pos, reps):
    half = HEAD_DIM // 2
    inv = ROPE_THETA ** (-jnp.arange(half, dtype=F32) / half)
    ang = pos.astype(F32)[:, None] * inv[None, :]
    cos, sin = jnp.cos(ang), jnp.sin(ang)
    cos_t = jnp.tile(cos, (reps, LANES // half))
    sin_t = jnp.tile(jnp.concatenate([-sin, sin], axis=1), (reps, LANES // HEAD_DIM))
    return cos_t, sin_t


def _compress_weights(cmp_pos, cmp_w1, cmp_w2):
    eye = jnp.eye(N_NSA_KV_HEADS, dtype=F32)
    w1 = cmp_w1.reshape(2, CMP_BLOCK, HEAD_DIM, CMP_HIDDEN)

    def expand_w1(w):
        t = jnp.einsum('sldf,hg->slhdgf', w, eye)
        return t.reshape(2, CMP_STRIDE * LANES, N_NSA_KV_HEADS * CMP_HIDDEN).astype(BF16)

    def expand_pos(p):
        return jnp.tile(p[:, :, None, :], (1, 1, N_NSA_KV_HEADS, 1)).reshape(2, 1, CMP_STRIDE * LANES)

    w2 = jnp.einsum('sfd,hg->shfgd', cmp_w2, eye).reshape(2, N_NSA_KV_HEADS * CMP_HIDDEN, LANES).astype(BF16)
    return (expand_pos(cmp_pos[:, :CMP_STRIDE]), expand_pos(cmp_pos[:, CMP_STRIDE:]),
            expand_w1(w1[:, :CMP_STRIDE]), expand_w1(w1[:, CMP_STRIDE:]), w2)


def _selection_constants(n_chunks, n_keys, expand_rows):
    n = np.arange(n_chunks)
    m = np.arange(SEL_LANES)
    cs, ss = n * CMP_STRIDE, m * SEL_BLOCK
    ovl = (cs[:, None] < ss[None, :] + SEL_BLOCK) & (cs[:, None] + CMP_BLOCK > ss[None, :]) & (n[:, None] < n_chunks - 1)
    expand = (np.arange(n_keys)[None, :] // SEL_BLOCK) == np.arange(expand_rows)[:, None]
    return ovl, jnp.asarray(expand, BF16)


def _row_tile(rows, want):
    t = min(rows, want)
    assert rows % t == 0
    return t


def kernel(x_prompt, x_sample, cache_diff_kv, cache_nsa_kv, state_nsa_win_kv, page_table, w_in, w_out, w_up, w_down,
           g_pre_mix, g_post_mix, g_pre_mlp, g_post_mlp, lam_q1, lam_k1, lam_q2, lam_k2, diff_subln, cmp_pos,
           cmp_w1, cmp_w2):
    depth = w_in.shape[0]
    assert depth == 1, "one layer: the sample group's paged caches are read in place"
    b, s, d = x_prompt.shape
    db, sq, _ = x_sample.shape
    n_phys, page = cache_diff_kv.shape[1:3]
    n_pages = page_table.shape[1]
    past = n_pages * page
    assert s % (4 * LANES) == 0 and s >= WINDOW + LANES and s <= SEL_LANES * SEL_BLOCK
    assert past >= WINDOW and sq < CMP_STRIDE and sq <= page and past + sq <= SEL_LANES * SEL_BLOCK
    assert state_nsa_win_kv.shape[2] == WINDOW and page == LANES
    layer = 0
    lam_init = 0.8 - 0.6 * math.exp(-0.3 * layer)

    wl = w_in[layer]
    qn0 = 3 * DIFF_WIDTH
    qn_cols = np.concatenate([qn0 + h * HEAD_DIM + np.arange(HEAD_DIM) for h in _HEAD_PERM])
    w_qkv = wl[:, np.concatenate([np.arange(qn0), qn_cols])].astype(BF16)
    w_kv = wl[:, QKV_WIDTH:QKV_WIDTH + KV_SLOTS * LANES].astype(BF16)
    gate0 = QKV_WIDTH + KV_SLOTS * LANES
    gate_cols = np.asarray([gate0 + h * 3 + j for j in range(3) for h in _HEAD_PERM])
    w_gate = jnp.pad(wl[:, gate_cols], ((0, 0), (0, LANES - N_GATES))).astype(BF16)
    out_rows = np.concatenate([np.arange(DIFF_WIDTH)] + [DIFF_WIDTH + h * HEAD_DIM + np.arange(HEAD_DIM) for h in _HEAD_PERM])
    wo = w_out[layer][out_rows].astype(BF16)
    wu = w_up[layer].astype(BF16)
    wd = w_down[layer].astype(BF16)
    vec = lambda a: a[layer].reshape(1, -1)
    lams = (vec(lam_q1), vec(lam_k1), vec(lam_q2), vec(lam_k2))
    cw = _compress_weights(cmp_pos[layer], cmp_w1[layer], cmp_w2[layer])

    rows_p = b * s
    tm_p = _row_tile(s, 256)
    cos_p, sin_p = _rope_tables(jnp.arange(s), 1)
    xp = x_prompt.reshape(rows_p, d)
    qd, kvd, kvdb, qn, nsa_t, win_t, kvb_t, gates = _inproj(
        xp, vec(g_pre_mix), w_qkv, w_kv.T, w_gate, (cos_p, sin_p, cos_p.T, sin_p.T), tm_p, s // tm_p, b, True)
    od = _diff_prompt(qd, kvdb, lams, vec(diff_subln), b, s, _row_tile(s, 512), lam_init)
    cmp_kv = _compress_prompt(nsa_t, cw)
    ovl_p, exp_p = _selection_constants(s // CMP_STRIDE, s, LANES)
    ovl_rows = np.concatenate([ovl_p.T, np.ones((SUBLANES, ovl_p.shape[0])),
                               np.zeros((LANES - SEL_LANES - SUBLANES, ovl_p.shape[0]))], axis=0)
    on = _nsa_prompt(qn, gates, cmp_kv, kvb_t, jnp.asarray(ovl_rows, BF16), exp_p, b, s, LANES, 4 * LANES)
    yp = _mlp(xp, od, on, wo, wu, wd, vec(g_post_mix), vec(g_pre_mlp), vec(g_post_mlp), tm_p).reshape(b, s, d)
    p_diff = kvd.reshape(1, b, s, 2, N_DIFF_HEADS, 2 * HEAD_DIM)
    token_minor = lambda a, slots: jnp.transpose(
        a.reshape(a.shape[0], slots, N_NSA_KV_HEADS, HEAD_DIM, a.shape[-1]), (0, 4, 1, 2, 3))[None]
    p_nsa = token_minor(nsa_t, 4)
    p_win = token_minor(win_t[:, :, s - WINDOW:], 2)

    rows_s = db * sq
    tm_s = _row_tile(rows_s, 256)
    assert tm_s % sq == 0
    cos_s, sin_s = _rope_tables(past + jnp.arange(sq), tm_s // sq)
    xs = x_sample.reshape(rows_s, d)
    qd, kvd, qn, nsa4, win2, gates = _inproj(xs, vec(g_pre_mix), w_qkv, w_kv, w_gate, (cos_s, sin_s), tm_s, 1, db,
                                             False)
    r3 = lambda a: a.reshape(db, sq, a.shape[1])
    diff_rows = cache_diff_kv[layer].reshape(n_phys * page * DIFF_SLABS, LANES)
    od = _diff_sample(diff_rows, page_table, r3(qd), kvd, lams, vec(diff_subln), page, lam_init)
    nsa_cache_t = jnp.transpose(cache_nsa_kv[layer], (0, 2, 3, 4, 1))
    win_state_t = jnp.transpose(state_nsa_win_kv[layer], (0, 2, 3, 4, 1)).reshape(db, 2 * LANES, WINDOW)
    ovl_s, exp_s = _selection_constants(past // CMP_STRIDE, past + page, SEL_LANES)
    on, win_new_t = _nsa_sample(nsa_cache_t, page_table, win_state_t, r3(nsa4), r3(win2), r3(qn), r3(gates), cw,
                                jnp.asarray(ovl_s, BF16), exp_s)
    ys = _mlp(xs, od.reshape(rows_s, DIFF_WIDTH), on.reshape(rows_s, NSA_WIDTH), wo, wu, wd, vec(g_post_mix),
              vec(g_pre_mlp), vec(g_post_mlp), tm_s).reshape(db, sq, d)
    s_diff = kvd.reshape(1, db, sq, 2, N_DIFF_HEADS, 2 * HEAD_DIM)
    s_nsa = nsa4.reshape(1, db, sq, 4, N_NSA_KV_HEADS, HEAD_DIM)
    s_win = token_minor(win_new_t, 2)

    return yp, ys, p_diff, p_nsa, p_win, s_diff, s_nsa, s_win
```

```python
import functools
import math

import numpy as np
import jax
import jax.numpy as jnp
from jax import lax
from jax.experimental import pallas as pl
from jax.experimental.pallas import tpu as pltpu

HEAD_DIM = 64
N_DIFF_HEADS = 4
N_NSA_HEADS = 8
N_NSA_KV_HEADS = 2
NSA_GROUP = N_NSA_HEADS // N_NSA_KV_HEADS
CMP_BLOCK = 32
CMP_STRIDE = 16
CMP_HIDDEN = 128
SEL_BLOCK = 64
SEL_TOP_N = 16
WINDOW = 512
ROPE_THETA = 10000.0
NORM_EPS = 1e-6
DIFF_WIDTH = N_DIFF_HEADS * 2 * HEAD_DIM
NSA_WIDTH = N_NSA_HEADS * HEAD_DIM
N_GATES = 3 * N_NSA_HEADS
NEG = -1e9
BIG = 1e9
SCALE = HEAD_DIM ** -0.5

LANES = 128
SUBLANES = 8
SEL_LANES = 64
SOFTMAX_ROWS = 32
PROJ_ROWS = 256
DIFF_TILE = 512
NSA_QUERIES = LANES
NSA_KEYS = 4 * LANES
VMEM_LIMIT = 56 * 1024 * 1024
DIFF_SLABS = 2 * N_DIFF_HEADS
QKV_WIDTH = 3 * DIFF_WIDTH + NSA_WIDTH
KV_SLOTS = 6

F32 = jnp.float32
BF16 = jnp.bfloat16

_HEAD_PERM = tuple(g + NSA_GROUP * half for g in range(NSA_GROUP) for half in range(2))


def _dot(a, b):
    return jnp.dot(a, b, preferred_element_type=F32)


def _dot_nt(a, b):
    return lax.dot_general(a, b, (((1,), (1,)), ((), ())), preferred_element_type=F32)


def _rms(x):
    return x * lax.rsqrt(jnp.mean(x * x, axis=-1, keepdims=True) + NORM_EPS)


def _params(*sem):
    return pltpu.CompilerParams(dimension_semantics=sem, vmem_limit_bytes=VMEM_LIMIT)


def _const_spec(shape):
    nd = len(shape)
    return pl.BlockSpec(shape, lambda *_: (0,) * nd)


def _pad_rows(x, rows):
    return jnp.concatenate([x, jnp.zeros((rows - x.shape[0], x.shape[1]), x.dtype)], axis=0)


def _rotary(t, cos, sin, axis):
    idx = lax.broadcasted_iota(jnp.int32, t.shape, axis)
    first_half = (idx % HEAD_DIM) < HEAD_DIM // 2
    partner = jnp.where(first_half, pltpu.roll(t, LANES - HEAD_DIM // 2, axis), pltpu.roll(t, HEAD_DIM // 2, axis))
    return t * cos + partner * sin


def _inproj_kernel(*refs, kv_transposed):
    if kv_transposed:
        (x_ref, g_ref, w_ref, wkv_ref, wg_ref, cos_ref, sin_ref, cost_ref, sint_ref,
         qd_ref, kvd_ref, kvdb_ref, qn_ref, nsa_ref, win_ref, kvb_ref, gate_ref) = refs
    else:
        (x_ref, g_ref, w_ref, wkv_ref, wg_ref, cos_ref, sin_ref,
         qd_ref, kvd_ref, qn_ref, nsa_ref, win_ref, gate_ref) = refs
    tm = x_ref.shape[0]
    hb = (_rms(x_ref[...]) * g_ref[...]).astype(BF16)
    cos = cos_ref[...]
    sin = sin_ref[...]

    groups = [_dot(hb, w_ref[:, c0:c0 + DIFF_WIDTH]) for c0 in range(0, QKV_WIDTH, DIFF_WIDTH)]

    def proj(c0):
        return groups[c0 // DIFF_WIDTH][:, c0 % DIFF_WIDTH:c0 % DIFF_WIDTH + LANES]

    for j in range(N_DIFF_HEADS):
        qd_ref[:, j * LANES:(j + 1) * LANES] = (_rotary(proj(j * LANES), cos, sin, 1) * SCALE).astype(BF16)
        k = _rotary(proj(DIFF_WIDTH + j * LANES), cos, sin, 1)
        v = proj(2 * DIFF_WIDTH + j * LANES)
        kvd_ref[pl.ds(j, tm, stride=DIFF_SLABS), :] = k
        kvd_ref[pl.ds(N_DIFF_HEADS + j, tm, stride=DIFF_SLABS), :] = v
        if kv_transposed:
            kvdb_ref[:, j * LANES:(j + 1) * LANES] = k.astype(BF16)
            kvdb_ref[:, DIFF_WIDTH + j * LANES:DIFF_WIDTH + (j + 1) * LANES] = v.astype(BF16)
    for j in range(NSA_WIDTH // LANES):
        qn_ref[:, j * LANES:(j + 1) * LANES] = (_rotary(proj(3 * DIFF_WIDTH + j * LANES), cos, sin, 1)
                                                * SCALE).astype(BF16)
    gate_ref[...] = jax.nn.sigmoid(_dot(hb, wg_ref[...]))

    if kv_transposed:
        t_all = _dot_nt(wkv_ref[...], hb)
        cost = cost_ref[...]
        sint = sint_ref[...]
        for slot in range(KV_SLOTS):
            t = t_all[slot * LANES:(slot + 1) * LANES]
            if slot % 2 == 0:
                t = _rotary(t, cost, sint, 0)
            if slot < 4:
                nsa_ref[0, slot * LANES:(slot + 1) * LANES, :] = t
            else:
                win_ref[0, (slot - 4) * LANES:(slot - 3) * LANES, :] = t
            if slot >= 2:
                kvb_ref[0, (slot - 2) * LANES:(slot - 1) * LANES, :] = t.astype(BF16)
    else:
        t_all = _dot(hb, wkv_ref[...])
        for slot in range(KV_SLOTS):
            t = t_all[:, slot * LANES:(slot + 1) * LANES]
            if slot % 2 == 0:
                t = _rotary(t, cos, sin, 1)
            if slot < 4:
                nsa_ref[:, slot * LANES:(slot + 1) * LANES] = t
            else:
                win_ref[:, (slot - 4) * LANES:(slot - 3) * LANES] = t


def _inproj(x2d, g, w_qkv, w_kv, w_gate, tabs, tm, n_tab_blocks, batch, kv_transposed):
    rows, d = x2d.shape
    row_spec = lambda n: pl.BlockSpec((tm, n), lambda i: (i, 0))
    tab_spec = pl.BlockSpec((tm, LANES), lambda i: (i % n_tab_blocks, 0))
    in_specs = [row_spec(d), _const_spec((1, d)), _const_spec(w_qkv.shape), _const_spec(w_kv.shape),
                _const_spec(w_gate.shape), tab_spec, tab_spec]
    sds = jax.ShapeDtypeStruct
    interleaved = (sds((rows * DIFF_SLABS, LANES), F32), pl.BlockSpec((tm * DIFF_SLABS, LANES), lambda i: (i, 0)))
    if kv_transposed:
        seq = rows // batch
        nt = seq // tm
        tabt_spec = pl.BlockSpec((LANES, tm), lambda i: (0, i % nt))
        in_specs += [tabt_spec, tabt_spec]
        tspec = lambda n: pl.BlockSpec((1, n, tm), lambda i: (i // nt, 0, i % nt))
        outs = [
            (sds((rows, DIFF_WIDTH), BF16), row_spec(DIFF_WIDTH)),
            interleaved,
            (sds((rows, 2 * DIFF_WIDTH), BF16), row_spec(2 * DIFF_WIDTH)),
            (sds((rows, NSA_WIDTH), BF16), row_spec(NSA_WIDTH)),
            (sds((batch, 4 * LANES, seq), F32), tspec(4 * LANES)),
            (sds((batch, 2 * LANES, seq), F32), tspec(2 * LANES)),
            (sds((batch, 4 * LANES, seq), BF16), tspec(4 * LANES)),
            (sds((rows, LANES), F32), row_spec(LANES)),
        ]
    else:
        outs = [
            (sds((rows, DIFF_WIDTH), BF16), row_spec(DIFF_WIDTH)),
            interleaved,
            (sds((rows, NSA_WIDTH), BF16), row_spec(NSA_WIDTH)),
            (sds((rows, 4 * LANES), F32), row_spec(4 * LANES)),
            (sds((rows, 2 * LANES), F32), row_spec(2 * LANES)),
            (sds((rows, LANES), F32), row_spec(LANES)),
        ]
    return pl.pallas_call(
        functools.partial(_inproj_kernel, kv_transposed=kv_transposed),
        grid=(rows // tm,),
        in_specs=in_specs,
        out_specs=tuple(o[1] for o in outs),
        out_shape=tuple(o[0] for o in outs),
        compiler_params=_params("parallel"),
        name="inproj",
    )(x2d, g, w_qkv, w_kv, w_gate, *tabs)


def _lambda(lq1, lk1, lq2, lk2, lam_init):
    return (jnp.exp(jnp.sum(lq1[...] * lk1[...], axis=-1, keepdims=True))
            - jnp.exp(jnp.sum(lq2[...] * lk2[...], axis=-1, keepdims=True)) + lam_init)


def _split_components(q):
    lane = lax.broadcasted_iota(jnp.int32, q.shape, 1)
    low = lane < HEAD_DIM
    return jnp.concatenate([jnp.where(low, q, 0.0), jnp.where(low, 0.0, q)], axis=0).astype(BF16)


def _lane_tile(x, width):
    return jnp.concatenate([x] * (width // LANES), axis=1)


def _with_ones(v, axis):
    return jnp.concatenate([v, jnp.ones(v.shape, v.dtype)], axis=axis)


def _softmax_chunks(s_ref, p_ref, m_ref, a_ref, n_rows, adjust, unrolled):
    keys = s_ref.shape[1]

    def new_max(r0):
        rows = pl.ds(r0, SOFTMAX_ROWS)
        m_prev = m_ref[rows, :]
        m_new = jnp.maximum(m_prev, jnp.max(adjust(s_ref[rows, :], r0), axis=1, keepdims=True))
        a_ref[rows, :] = jnp.exp(m_prev - m_new)
        m_ref[rows, :] = m_new

    def exponentiate(r0):
        rows = pl.ds(r0, SOFTMAX_ROWS)
        p_ref[rows, :] = jnp.exp(adjust(s_ref[rows, :], r0) - _lane_tile(m_ref[rows, :], keys)).astype(BF16)

    for one_pass in (new_max, exponentiate):
        if unrolled:
            for c in range(n_rows // SOFTMAX_ROWS):
                one_pass(c * SOFTMAX_ROWS)
        else:
            def body(c, carry, one_pass=one_pass):
                one_pass(pl.multiple_of(c * SOFTMAX_ROWS, SOFTMAX_ROWS))
                return carry
            lax.fori_loop(0, n_rows // SOFTMAX_ROWS, body, 0)


def _diff_prompt_kernel(qi_ref, ki_ref, q_ref, k_ref, v_ref, lq1, lk1, lq2, lk2, g_ref, o_ref,
                        m_ref, a_ref, acc_ref, s_ref, p_ref, *, tile, lam_init):
    t = pl.program_id(1)
    qi = qi_ref[t]
    ki = ki_ref[t]
    n_rows = 2 * tile

    @pl.when(ki == 0)
    def _init():
        m_ref[...] = jnp.full(m_ref.shape, NEG, F32)
        acc_ref[...] = jnp.zeros(acc_ref.shape, F32)

    def causal(s, r0):
        row = r0 % tile + lax.broadcasted_iota(jnp.int32, s.shape, 0)
        return jnp.where(lax.broadcasted_iota(jnp.int32, s.shape, 1) <= row, s, NEG)

    def update(diagonal):
        for h in range(N_DIFF_HEADS):
            buf = h % 2
            q2 = _split_components(q_ref[:, h * LANES:(h + 1) * LANES].astype(F32))
            s_ref[buf] = _dot_nt(q2, k_ref[:, h * LANES:(h + 1) * LANES])
            _softmax_chunks(s_ref.at[buf], p_ref.at[buf], m_ref.at[h], a_ref.at[buf], n_rows,
                            causal if diagonal else (lambda s, r0: s), unrolled=True)
            acc_ref[h] = (_lane_tile(a_ref[buf], 2 * LANES) * acc_ref[h]
                          + _dot(p_ref[buf], _with_ones(v_ref[:, h * LANES:(h + 1) * LANES], 1)))

    @pl.when(ki < qi)
    def _below_diagonal():
        update(False)

    @pl.when(ki == qi)
    def _finish():
        update(True)
        lam = _lambda(lq1, lk1, lq2, lk2, lam_init)
        for h in range(N_DIFF_HEADS):
            o = acc_ref[h, :, 0:LANES] / acc_ref[h, :, LANES:2 * LANES]
            a = o[0:tile] - lam * o[tile:2 * tile]
            o_ref[:, h * LANES:(h + 1) * LANES] = (_rms(a) * g_ref[...] * (1.0 - lam_init)).astype(BF16)


def _diff_prompt(qd, kvdb, lams, subln, b, s, tile, lam_init):
    nq = s // tile
    pairs = [(qi, ki) for qi in range(nq) for ki in range(qi + 1)]
    qi_arr = jnp.asarray([p[0] for p in pairs], jnp.int32)
    ki_arr = jnp.asarray([p[1] for p in pairs], jnp.int32)
    grid_spec = pltpu.PrefetchScalarGridSpec(
        num_scalar_prefetch=2,
        grid=(b, len(pairs)),
        in_specs=[
            pl.BlockSpec((tile, DIFF_WIDTH), lambda bi, t, qa, ka: (bi * nq + qa[t], 0)),
            pl.BlockSpec((tile, DIFF_WIDTH), lambda bi, t, qa, ka: (bi * nq + ka[t], 0)),
            pl.BlockSpec((tile, DIFF_WIDTH), lambda bi, t, qa, ka: (bi * nq + ka[t], 1)),
        ] + [_const_spec((1, HEAD_DIM))] * 4 + [_const_spec((1, 2 * HEAD_DIM))],
        out_specs=pl.BlockSpec((tile, DIFF_WIDTH), lambda bi, t, qa, ka: (bi * nq + qa[t], 0)),
        scratch_shapes=[pltpu.VMEM((N_DIFF_HEADS, 2 * tile, LANES), F32),
                        pltpu.VMEM((2, 2 * tile, LANES), F32),
                        pltpu.VMEM((N_DIFF_HEADS, 2 * tile, 2 * LANES), F32),
                        pltpu.VMEM((2, 2 * tile, tile), F32),
                        pltpu.VMEM((2, 2 * tile, tile), BF16)],
    )
    return pl.pallas_call(
        functools.partial(_diff_prompt_kernel, tile=tile, lam_init=lam_init),
        grid_spec=grid_spec,
        out_shape=jax.ShapeDtypeStruct((b * s, DIFF_WIDTH), BF16),
        compiler_params=_params("parallel", "arbitrary"),
        name="diff_prompt",
    )(qi_arr, ki_arr, qd, kvdb, kvdb, *lams, subln)


def _diff_sample_kernel(pt_ref, *refs, n_pages, page, past, sq, lam_init):
    page_refs = refs[:n_pages]
    q_ref, new_ref, lq1, lk1, lq2, lk2, g_ref, o_ref, kv_ref = refs[n_pages:]
    n_cols = 2 * (n_pages + 1) * page
    n_rows = N_DIFF_HEADS * 2 * sq
    col = lax.broadcasted_iota(jnp.int32, (n_rows, n_cols), 1)
    tpos = past + lax.broadcasted_iota(jnp.int32, (n_rows, n_cols), 0) % sq
    visible = (col % 2 == 0) & (col // 2 <= tpos)
    lam = _lambda(lq1, lk1, lq2, lk2, lam_init)

    scores = []
    for h in range(N_DIFF_HEADS):
        for i, r in enumerate(page_refs):
            kv_ref[h, i * 2 * page:(i + 1) * 2 * page, :] = r[pl.ds(h, 2 * page, stride=N_DIFF_HEADS), :].astype(BF16)
        kv_ref[h, n_pages * 2 * page:n_cols, :] = _pad_rows(new_ref[pl.ds(h, 2 * sq, stride=N_DIFF_HEADS), :],
                                                            2 * page).astype(BF16)
        q2 = _split_components(q_ref[0, :, h * LANES:(h + 1) * LANES].astype(F32))
        scores.append(_dot_nt(q2, kv_ref[h]))
    s = jnp.where(visible, jnp.concatenate(scores, axis=0), NEG)
    m = jnp.max(s, axis=1, keepdims=True)
    p = jnp.where(visible, jnp.exp(s - m), 0.0)
    l = jnp.maximum(jnp.sum(p, axis=1, keepdims=True), 1e-30)
    p = pltpu.roll(p, 1, 1).astype(BF16)
    for h in range(N_DIFF_HEADS):
        rows = slice(h * 2 * sq, (h + 1) * 2 * sq)
        o = _dot(p[rows], kv_ref[h]) / l[rows]
        a = o[0:sq] - lam * o[sq:2 * sq]
        o_ref[0, :, h * LANES:(h + 1) * LANES] = (_rms(a) * g_ref[...] * (1.0 - lam_init)).astype(BF16)


def _diff_sample(cache_rows, page_table, qd3, kvd_rows, lams, subln, page, lam_init):
    db, sq, _ = qd3.shape
    n_pages = page_table.shape[1]
    past = n_pages * page
    page_specs = [pl.BlockSpec((page * DIFF_SLABS, LANES), functools.partial(lambda bi, pt, p: (pt[bi, p], 0), p=p))
                  for p in range(n_pages)]
    grid_spec = pltpu.PrefetchScalarGridSpec(
        num_scalar_prefetch=1,
        grid=(db,),
        in_specs=page_specs + [
            pl.BlockSpec((1, sq, DIFF_WIDTH), lambda bi, pt: (bi, 0, 0)),
            pl.BlockSpec((sq * DIFF_SLABS, LANES), lambda bi, pt: (bi, 0)),
        ] + [_const_spec((1, HEAD_DIM))] * 4 + [_const_spec((1, 2 * HEAD_DIM))],
        out_specs=pl.BlockSpec((1, sq, DIFF_WIDTH), lambda bi, pt: (bi, 0, 0)),
        scratch_shapes=[pltpu.VMEM((N_DIFF_HEADS, 2 * (n_pages + 1) * page, LANES), BF16)],
    )
    return pl.pallas_call(
        functools.partial(_diff_sample_kernel, n_pages=n_pages, page=page, past=past, sq=sq, lam_init=lam_init),
        grid_spec=grid_spec,
        out_shape=jax.ShapeDtypeStruct((db, sq, DIFF_WIDTH), BF16),
        compiler_params=_params("parallel"),
        name="diff_sample",
    )(page_table, *([cache_rows] * n_pages), qd3, kvd_rows, *lams, subln)


def _chunk_pitch(n_chunks):
    assert n_chunks % (2 * SUBLANES) == 0
    return n_chunks + SUBLANES


def _store_token_rows(rows_ref, slot, first_chunk, rows, n_chunks):
    for c in range(rows.shape[0] // CMP_STRIDE):
        rows_ref[slot, pl.ds(first_chunk + c, CMP_STRIDE, stride=_chunk_pitch(n_chunks)), :] = (
            rows[c * CMP_STRIDE:(c + 1) * CMP_STRIDE])


def _compress(rows_ref, slot, n_chunks, pos_a, pos_b, w_a, w_b, w2):
    pitch = _chunk_pitch(n_chunks)
    xs = jnp.concatenate([rows_ref[slot, j * pitch:j * pitch + n_chunks, :] for j in range(CMP_STRIDE)],
                         axis=1)
    a = _dot((xs + pos_a).astype(BF16), w_a)
    b = _dot((xs + pos_b).astype(BF16), w_b)
    hidden = jax.nn.gelu(a + pltpu.roll(b, n_chunks - 1, 0))
    out = _dot(hidden.astype(BF16), w2)
    rid = lax.broadcasted_iota(jnp.int32, out.shape, 0)
    return jnp.where(rid < n_chunks - 1, out, 0.0)


def _compress_prompt_kernel(x_ref, pa_ref, pb_ref, wa_ref, wb_ref, w2_ref, o_ref, rows_ref):
    seq = x_ref.shape[2]
    for slot in range(2):
        for c in range(seq // LANES):
            _store_token_rows(rows_ref, slot, c * (LANES // CMP_STRIDE),
                              x_ref[0, slot * LANES:(slot + 1) * LANES, c * LANES:(c + 1) * LANES].T,
                              seq // CMP_STRIDE)
        o_ref[0, slot] = _compress(rows_ref, slot, seq // CMP_STRIDE, pa_ref[slot], pb_ref[slot], wa_ref[slot],
                                   wb_ref[slot], w2_ref[slot]).astype(BF16)


def _compress_prompt(nsa_t, cw):
    b, _, seq = nsa_t.shape
    nch = seq // CMP_STRIDE
    return pl.pallas_call(
        _compress_prompt_kernel,
        grid=(b,),
        in_specs=[pl.BlockSpec((1, 2 * LANES, seq), lambda i: (i, 0, 0))] + [_const_spec(a.shape) for a in cw],
        out_specs=pl.BlockSpec((1, 2, nch, LANES), lambda i: (i, 0, 0, 0)),
        out_shape=jax.ShapeDtypeStruct((b, 2, nch, LANES), BF16),
        scratch_shapes=[pltpu.VMEM((2, CMP_STRIDE * _chunk_pitch(nch), LANES), F32)],
        compiler_params=_params("parallel"),
        name="compress_prompt",
    )(nsa_t, *cw)


def _cmp_probs(qs, kc, tpos):
    s = _dot_nt(qs, kc)
    cend = lax.broadcasted_iota(jnp.int32, s.shape, 1) * CMP_STRIDE + (CMP_BLOCK - 1)
    vis = cend <= tpos
    s = jnp.where(vis, s, NEG)
    m = jnp.max(s, axis=1, keepdims=True)
    p = jnp.where(vis, jnp.exp(s - m), 0.0)
    return (p / jnp.maximum(jnp.sum(p, axis=1, keepdims=True), 1e-30)).astype(BF16)


def _block_scores(imp, blk, cur):
    forced = (blk == 0) | (blk == cur) | (blk == cur - 1)
    return jnp.where(blk > cur, NEG, jnp.where(forced, BIG, imp))


def _select_blocks(imp, tpos2):
    blk = lax.broadcasted_iota(jnp.int32, imp.shape, 1)
    score = _block_scores(imp, blk, tpos2 // SEL_BLOCK)
    rank = jnp.zeros(imp.shape, F32)
    for j in range(SEL_LANES):
        sj = score[:, j:j + 1]
        tie = jnp.where(blk > j, 1.0, 0.0)
        rank = rank + jnp.where(sj > score, 1.0, 0.0) + jnp.where(sj == score, tie, 0.0)
    return jnp.where(rank < SEL_TOP_N, 1.0, 0.0).astype(BF16)


def _select_blocks_t(imp_t, tpos_t):
    n = imp_t.shape[1]
    blk = lax.broadcasted_iota(jnp.int32, imp_t.shape, 0)
    score = _block_scores(imp_t, blk, tpos_t // SEL_BLOCK)
    n_groups = SEL_LANES // SUBLANES
    groups = [score[g * SUBLANES:(g + 1) * SUBLANES] for g in range(n_groups)]
    ranks = [jnp.zeros((SUBLANES, n), F32) for _ in range(n_groups)]
    rid = lax.broadcasted_iota(jnp.int32, (SUBLANES, n), 0)
    for j in range(SEL_LANES):
        sj = score[j:j + 1]
        for g in range(n_groups):
            if g * SUBLANES > j:
                beats = jnp.where(sj >= groups[g], 1.0, 0.0)
            elif (g + 1) * SUBLANES - 1 < j:
                beats = jnp.where(sj > groups[g], 1.0, 0.0)
            else:
                beats = jnp.where(rid + g * SUBLANES > j, jnp.where(sj >= groups[g], 1.0, 0.0),
                                  jnp.where(sj > groups[g], 1.0, 0.0))
            ranks[g] = ranks[g] + beats
    sel_t = jnp.concatenate([jnp.where(r < SEL_TOP_N, 1.0, 0.0) for r in ranks]
                            + [jnp.zeros((LANES - SEL_LANES, n), F32)], axis=0)
    return sel_t.T


def _stack_queries(q, n_q):
    lane = lax.broadcasted_iota(jnp.int32, (n_q, LANES), 1)
    low = lane < HEAD_DIM
    groups = [q[:, g * LANES:(g + 1) * LANES].astype(F32) for g in range(NSA_GROUP)]
    return jnp.concatenate([jnp.where(low, t, 0.0) for t in groups] + [jnp.where(low, 0.0, t) for t in groups],
                           axis=0).astype(BF16)


def _combine(o_c, o_s, o_w, gates, n_q, store):
    lane = lax.broadcasted_iota(jnp.int32, (n_q, LANES), 1)
    low = lane < HEAD_DIM
    for g in range(NSA_GROUP):
        def pick(o):
            return jnp.where(low, o[g * n_q:(g + 1) * n_q], o[(NSA_GROUP + g) * n_q:(NSA_GROUP + g + 1) * n_q])

        def gate(j):
            c = j * N_NSA_HEADS + 2 * g
            return jnp.where(low, gates[:, c:c + 1], gates[:, c + 1:c + 2])

        store(g, gate(0) * pick(o_c) + gate(1) * pick(o_s) + gate(2) * pick(o_w))


def _per_kv_head(x, n_q):
    return x.reshape(N_NSA_KV_HEADS, NSA_GROUP, n_q, x.shape[-1])


def _nsa_prompt_kernel(qn_ref, gate_ref, cmp_ref, kv_ref, ovl_ref, exp_ref, o_ref,
                       m_ref, a_ref, acc_ref, s_ref, p_ref, sw_ref, pw_ref, sc_ref, pc_ref, qse_ref, *, tq, tk):
    q_start = pl.program_id(1) * tq
    m_rows = N_NSA_HEADS * tq
    qse_ref[:, 0:LANES] = _stack_queries(qn_ref[...], tq)
    tpos2_t = q_start + lax.broadcasted_iota(jnp.int32, (1, N_NSA_KV_HEADS * tq), 1) % tq
    m_ref[...] = jnp.full(m_ref.shape, NEG, F32)
    acc_ref[...] = jnp.zeros(acc_ref.shape, F32)

    m_ref[2] = jnp.full(m_ref.shape[1:], 0.5 * NEG, F32)
    sc_ref[...] = _dot_nt(qse_ref[:, 0:LANES], cmp_ref[0, 0])

    def block_ended(s, r0):
        qpos = q_start + r0 % tq + lax.broadcasted_iota(jnp.int32, s.shape, 0)
        cend = lax.broadcasted_iota(jnp.int32, s.shape, 1) * CMP_STRIDE + (CMP_BLOCK - 1)
        return jnp.where(cend <= qpos, s, NEG)

    _softmax_chunks(sc_ref, pc_ref, m_ref.at[2], a_ref, m_rows, block_ended, unrolled=True)
    o_c = _dot(pc_ref[...], _with_ones(cmp_ref[0, 1], 1))
    o_c = o_c[:, 0:LANES] / jnp.maximum(o_c[:, LANES:2 * LANES], 1e-30)
    imp_t = _dot_nt(ovl_ref[...], pc_ref[...])
    imp_t = imp_t[0:SEL_LANES] / jnp.maximum(imp_t[SEL_LANES:SEL_LANES + 1], 1e-30)
    imp_t = jnp.concatenate(
        [(imp_t[:, (4 * h) * tq:(4 * h + 1) * tq] + imp_t[:, (4 * h + 1) * tq:(4 * h + 2) * tq])
         + (imp_t[:, (4 * h + 2) * tq:(4 * h + 3) * tq] + imp_t[:, (4 * h + 3) * tq:(4 * h + 4) * tq])
         for h in range(N_NSA_KV_HEADS)], axis=1)
    sel = _select_blocks_t(imp_t, tpos2_t)

    block_bias = jnp.where(sel > 0.5, 0.0, NEG).astype(BF16)
    for r in range(N_NSA_HEADS):
        h = r // NSA_GROUP
        qse_ref[r * tq:(r + 1) * tq, LANES:2 * LANES] = block_bias[h * tq:(h + 1) * tq]

    def slc_tile(k0, width, diagonal):
        cols = pl.ds(k0, width)
        s_tile = s_ref.at[:, pl.ds(0, width)]
        p_tile = p_ref.at[:, pl.ds(0, width)]
        s_tile[...] = _dot(qse_ref[...], jnp.concatenate([kv_ref[0, 0:LANES, cols], exp_ref[:, cols]], axis=0))

        def causal(s, r0):
            qpos = q_start + r0 % tq + lax.broadcasted_iota(jnp.int32, s.shape, 0)
            return jnp.where(k0 + lax.broadcasted_iota(jnp.int32, s.shape, 1) <= qpos, s, NEG)

        _softmax_chunks(s_tile, p_tile, m_ref.at[0], a_ref, m_rows, causal if diagonal else (lambda s, r0: s),
                        unrolled=True)
        acc_ref[...] = (_lane_tile(a_ref[...], 2 * LANES) * acc_ref[...]
                        + _dot_nt(p_tile[...], _with_ones(kv_ref[0, LANES:2 * LANES, cols], 0)))

    def below_diagonal(kt, carry):
        slc_tile(pl.multiple_of(kt * tk, tk), tk, False)
        return carry

    last_tile = (q_start + tq - 1) // tk
    lax.fori_loop(0, last_tile, below_diagonal, 0)
    slc_tile(pl.multiple_of(last_tile * tk, tk), tk, True)
    o_s = acc_ref[:, 0:LANES] / acc_ref[:, LANES:2 * LANES]

    span = WINDOW + tq
    w0 = pl.multiple_of(jnp.maximum(q_start - WINDOW, 0), tq)
    sw_ref[...] = _dot(qse_ref[:, 0:LANES], kv_ref[0, 2 * LANES:3 * LANES, pl.ds(w0, span)])

    def windowed(s, r0):
        qpos = q_start + r0 % tq + lax.broadcasted_iota(jnp.int32, s.shape, 0)
        dist = qpos - (w0 + lax.broadcasted_iota(jnp.int32, s.shape, 1))
        return jnp.where(lax.bitcast_convert_type(dist, jnp.uint32) < WINDOW, s, NEG)

    _softmax_chunks(sw_ref, pw_ref, m_ref.at[1], a_ref, m_rows, windowed, unrolled=True)
    o_w = _dot_nt(pw_ref[...], _with_ones(kv_ref[0, 3 * LANES:4 * LANES, pl.ds(w0, span)], 0))
    o_w = o_w[:, 0:LANES] / o_w[:, LANES:2 * LANES]

    def store(g, val):
        o_ref[:, g * LANES:(g + 1) * LANES] = val.astype(BF16)

    _combine(o_c, o_s, o_w, gate_ref[...], tq, store)


def _nsa_prompt(qn, gates, cmp_kv, kv_t, ovl_t, expand, b, s, tq, tk):
    nq = s // tq
    nch = cmp_kv.shape[2]
    m_rows = N_NSA_HEADS * tq
    return pl.pallas_call(
        functools.partial(_nsa_prompt_kernel, tq=tq, tk=tk),
        grid=(b, nq),
        in_specs=[
            pl.BlockSpec((tq, NSA_WIDTH), lambda bi, qi: (bi * nq + qi, 0)),
            pl.BlockSpec((tq, LANES), lambda bi, qi: (bi * nq + qi, 0)),
            pl.BlockSpec((1, 2, nch, LANES), lambda bi, qi: (bi, 0, 0, 0)),
            pl.BlockSpec((1, 4 * LANES, s), lambda bi, qi: (bi, 0, 0)),
            _const_spec(ovl_t.shape), _const_spec(expand.shape),
        ],
        out_specs=pl.BlockSpec((tq, NSA_WIDTH), lambda bi, qi: (bi * nq + qi, 0)),
        out_shape=jax.ShapeDtypeStruct((b * s, NSA_WIDTH), BF16),
        scratch_shapes=[pltpu.VMEM((3, m_rows, LANES), F32),
                        pltpu.VMEM((m_rows, LANES), F32),
                        pltpu.VMEM((m_rows, 2 * LANES), F32),
                        pltpu.VMEM((m_rows, tk), F32), pltpu.VMEM((m_rows, tk), BF16),
                        pltpu.VMEM((m_rows, WINDOW + tq), F32), pltpu.VMEM((m_rows, WINDOW + tq), BF16),
                        pltpu.VMEM((m_rows, nch), F32), pltpu.VMEM((m_rows, nch), BF16),
                        pltpu.VMEM((m_rows, 2 * LANES), BF16)],
        compiler_params=_params("parallel", "arbitrary"),
        name="nsa_prompt",
    )(qn, gates, cmp_kv, kv_t, ovl_t, expand)


def _nsa_sample_kernel(pt_ref, *refs, n_pages, page, past, sq):
    page_refs = refs[:n_pages]
    (win_ref, nsanew_ref, winnew_ref, qn_ref, gate_ref, pa_ref, pb_ref, wa_ref, wb_ref, w2_ref, ovl_ref, exp_ref,
     o_ref, wout_ref, rows_ref) = refs[n_pages:]
    m_rows = N_NSA_HEADS * sq
    qs = _stack_queries(qn_ref[0], sq)
    tpos = past + lax.broadcasted_iota(jnp.int32, (m_rows, 1), 0) % sq
    tpos2 = past + lax.broadcasted_iota(jnp.int32, (N_NSA_KV_HEADS * sq, 1), 0) % sq

    def page_t(i, slot):
        return page_refs[i][0, slot].reshape(N_NSA_KV_HEADS * HEAD_DIM, page)

    cmp_kv = []
    for slot in range(2):
        for i in range(n_pages):
            _store_token_rows(rows_ref, slot, i * (page // CMP_STRIDE), page_t(i, slot).T, past // CMP_STRIDE)
        cmp_kv.append(_compress(rows_ref, slot, past // CMP_STRIDE, pa_ref[slot], pb_ref[slot], wa_ref[slot],
                                wb_ref[slot], w2_ref[slot]).astype(BF16))
    p_c = _cmp_probs(qs, cmp_kv[0], tpos)
    o_c = _dot(p_c, cmp_kv[1])
    imp = _per_kv_head(_dot(p_c, ovl_ref[...]), sq)
    imp = ((imp[:, 0] + imp[:, 1]) + (imp[:, 2] + imp[:, 3])).reshape(N_NSA_KV_HEADS * sq, SEL_LANES)
    sel = _select_blocks(imp, tpos2)

    nsanew = _pad_rows(nsanew_ref[0], page)
    n_keys = (n_pages + 1) * page
    s = jnp.concatenate([_dot(qs, page_t(i, 2).astype(BF16)) for i in range(n_pages)]
                        + [_dot_nt(qs, nsanew[:, 2 * LANES:3 * LANES].astype(BF16))], axis=1)
    picked = _dot(sel, exp_ref[...])
    kpos = lax.broadcasted_iota(jnp.int32, (1, n_keys), 1)
    bias = jnp.where(jnp.where(kpos <= tpos2, picked, 0.0) > 0.5, 0.0, NEG)
    s = (_per_kv_head(s, sq) + bias.reshape(N_NSA_KV_HEADS, 1, sq, n_keys)).reshape(m_rows, n_keys)
    m = jnp.max(s, axis=1, keepdims=True)
    p = jnp.exp(s - m)
    l = jnp.sum(p, axis=1, keepdims=True)
    pb = p.astype(BF16)
    acc = _dot(pb[:, past:n_keys], nsanew[:, 3 * LANES:4 * LANES].astype(BF16))
    for i in range(n_pages):
        acc = acc + _dot_nt(pb[:, i * page:(i + 1) * page], page_t(i, 3).astype(BF16))
    o_s = acc / l

    winnew = _pad_rows(winnew_ref[0], page)
    s = jnp.concatenate([_dot(qs, win_ref[0, 0:LANES, :].astype(BF16)),
                         _dot_nt(qs, winnew[:, 0:LANES].astype(BF16))], axis=1)
    col = lax.broadcasted_iota(jnp.int32, (1, WINDOW + page), 1)
    dist = tpos - (past - WINDOW + col)
    vis = (dist >= 0) & (dist < WINDOW)
    s = jnp.where(vis, s, NEG)
    m = jnp.max(s, axis=1, keepdims=True)
    p = jnp.where(vis, jnp.exp(s - m), 0.0)
    pb = p.astype(BF16)
    o_w = ((_dot_nt(pb[:, 0:WINDOW], win_ref[0, LANES:2 * LANES, :].astype(BF16))
            + _dot(pb[:, WINDOW:WINDOW + page], winnew[:, LANES:2 * LANES].astype(BF16)))
           / jnp.maximum(jnp.sum(p, axis=1, keepdims=True), 1e-30))

    def store(g, val):
        o_ref[0, :, g * LANES:(g + 1) * LANES] = val.astype(BF16)

    _combine(o_c, o_s, o_w, gate_ref[0], sq, store)

    extended = jnp.concatenate([win_ref[0], winnew.T], axis=1)
    wout_ref[0] = extended[:, sq:sq + WINDOW]


def _nsa_sample(cache_t, page_table, win_t, nsanew3, winnew3, qn3, gates3, cw, ovl, expand):
    db, sq, _ = qn3.shape
    n_pages = page_table.shape[1]
    page = cache_t.shape[-1]
    past = n_pages * page
    page_specs = [pl.BlockSpec((1,) + cache_t.shape[1:], functools.partial(lambda bi, pt, p: (pt[bi, p], 0, 0, 0, 0), p=p))
                  for p in range(n_pages)]
    per_b = lambda shape: pl.BlockSpec((1,) + shape, lambda bi, pt: (bi, 0, 0))
    grid_spec = pltpu.PrefetchScalarGridSpec(
        num_scalar_prefetch=1,
        grid=(db,),
        in_specs=page_specs + [
            per_b((2 * LANES, WINDOW)), per_b((sq, 4 * LANES)), per_b((sq, 2 * LANES)), per_b((sq, NSA_WIDTH)),
            per_b((sq, LANES)),
        ] + [_const_spec(a.shape) for a in cw] + [_const_spec(ovl.shape), _const_spec(expand.shape)],
        out_specs=(per_b((sq, NSA_WIDTH)), per_b((2 * LANES, WINDOW))),
        scratch_shapes=[pltpu.VMEM((2, CMP_STRIDE * _chunk_pitch(past // CMP_STRIDE), LANES), F32)],
    )
    return pl.pallas_call(
        functools.partial(_nsa_sample_kernel, n_pages=n_pages, page=page, past=past, sq=sq),
        grid_spec=grid_spec,
        out_shape=(jax.ShapeDtypeStruct((db, sq, NSA_WIDTH), BF16),
                   jax.ShapeDtypeStruct((db, 2 * LANES, WINDOW), F32)),
        compiler_params=_params("parallel"),
        name="nsa_sample",
    )(page_table, *([cache_t] * n_pages), win_t, nsanew3, winnew3, qn3, gates3, *cw, ovl, expand)


def _mlp_kernel(x_ref, od_ref, on_ref, wo_ref, wu_ref, wd_ref, g1_ref, g2_ref, g3_ref, y_ref, *, ff_chunk):
    mix = _dot(od_ref[...], wo_ref[0:DIFF_WIDTH, :]) + _dot(on_ref[...], wo_ref[DIFF_WIDTH:DIFF_WIDTH + NSA_WIDTH, :])
    x1 = x_ref[...] + _rms(mix) * g1_ref[...]
    hm = (_rms(x1) * g2_ref[...]).astype(BF16)
    d_ff = wu_ref.shape[1]
    ff = jnp.zeros(x1.shape, F32)
    for c in range(d_ff // ff_chunk):
        u = jnp.maximum(_dot(hm, wu_ref[:, c * ff_chunk:(c + 1) * ff_chunk]), 0.0)
        ff = ff + _dot((u * u).astype(BF16), wd_ref[c * ff_chunk:(c + 1) * ff_chunk, :])
    y_ref[...] = x1 + _rms(ff) * g3_ref[...]


def _mlp(x2d, od, on, w_out, w_up, w_down, g1, g2, g3, tm):
    rows, d = x2d.shape
    row_spec = lambda n: pl.BlockSpec((tm, n), lambda i: (i, 0))
    resident = lambda a: pl.BlockSpec(a.shape, lambda i: (0, 0), pipeline_mode=pl.Buffered(1))
    return pl.pallas_call(
        functools.partial(_mlp_kernel, ff_chunk=1024),
        grid=(rows // tm,),
        in_specs=[row_spec(d), row_spec(DIFF_WIDTH), row_spec(NSA_WIDTH), resident(w_out), resident(w_up),
                  resident(w_down), _const_spec((1, d)), _const_spec((1, d)), _const_spec((1, d))],
        out_specs=row_spec(d),
        out_shape=jax.ShapeDtypeStruct((rows, d), F32),
        compiler_params=_params("parallel"),
        name="mlp",
    )(x2d, od, on, w_out, w_up, w_down, g1, g2, g3)


def _rope_tables(pos, reps):
    half = HEAD_DIM // 2
    inv = ROPE_THETA ** (-jnp.arange(half, dtype=F32) / half)
    ang = pos.astype(F32)[:, None] * inv[None, :]
    cos, sin = jnp.cos(ang), jnp.sin(ang)
    cos_t = jnp.tile(cos, (reps, LANES // half))
    sin_t = jnp.tile(jnp.concatenate([-sin, sin], axis=1), (reps, LANES // HEAD_DIM))
    return cos_t, sin_t


def _compress_weights(cmp_pos, cmp_w1, cmp_w2):
    eye = jnp.eye(N_NSA_KV_HEADS, dtype=F32)
    w1 = cmp_w1.reshape(2, CMP_BLOCK, HEAD_DIM, CMP_HIDDEN)

    def expand_w1(w):
        t = jnp.einsum('sldf,hg->slhdgf', w, eye)
        return t.reshape(2, CMP_STRIDE * LANES, N_NSA_KV_HEADS * CMP_HIDDEN).astype(BF16)

    def expand_pos(p):
        return jnp.tile(p[:, :, None, :], (1, 1, N_NSA_KV_HEADS, 1)).reshape(2, 1, CMP_STRIDE * LANES)

    w2 = jnp.einsum('sfd,hg->shfgd', cmp_w2, eye).reshape(2, N_NSA_KV_HEADS * CMP_HIDDEN, LANES).astype(BF16)
    return (expand_pos(cmp_pos[:, :CMP_STRIDE]), expand_pos(cmp_pos[:, CMP_STRIDE:]),
            expand_w1(w1[:, :CMP_STRIDE]), expand_w1(w1[:, CMP_STRIDE:]), w2)


def _selection_constants(n_chunks, n_keys, expand_rows):
    n = np.arange(n_chunks)
    m = np.arange(SEL_LANES)
    cs, ss = n * CMP_STRIDE, m * SEL_BLOCK
    ovl = (cs[:, None] < ss[None, :] + SEL_BLOCK) & (cs[:, None] + CMP_BLOCK > ss[None, :]) & (n[:, None] < n_chunks - 1)
    expand = (np.arange(n_keys)[None, :] // SEL_BLOCK) == np.arange(expand_rows)[:, None]
    return ovl, jnp.asarray(expand, BF16)


def _row_tile(rows, want):
    t = min(rows, want)
    assert rows % t == 0
    return t


def kernel(x_prompt, x_sample, cache_diff_kv, cache_nsa_kv, state_nsa_win_kv, page_table, w_in, w_out, w_up, w_down,
           g_pre_mix, g_post_mix, g_pre_mlp, g_post_mlp, lam_q1, lam_k1, lam_q2, lam_k2, diff_subln, cmp_pos,
           cmp_w1, cmp_w2):
    depth = w_in.shape[0]
    assert depth == 1, "one layer: the sample group's paged caches are read in place"
    b, s, d = x_prompt.shape
    db, sq, _ = x_sample.shape
    n_phys, page = cache_diff_kv.shape[1:3]
    n_pages = page_table.shape[1]
    past = n_pages * page
    assert s % (4 * LANES) == 0 and s >= WINDOW + LANES and s <= SEL_LANES * SEL_BLOCK
    assert past >= WINDOW and sq < CMP_STRIDE and sq <= page and past + sq <= SEL_LANES * SEL_BLOCK
    assert state_nsa_win_kv.shape[2] == WINDOW and page == LANES
    layer = 0
    lam_init = 0.8 - 0.6 * math.exp(-0.3 * layer)

    wl = w_in[layer]
    qn0 = 3 * DIFF_WIDTH
    qn_cols = np.concatenate([qn0 + h * HEAD_DIM + np.arange(HEAD_DIM) for h in _HEAD_PERM])
    w_qkv = wl[:, np.concatenate([np.arange(qn0), qn_cols])].astype(BF16)
    w_kv = wl[:, QKV_WIDTH:QKV_WIDTH + KV_SLOTS * LANES].astype(BF16)
    gate0 = QKV_WIDTH + KV_SLOTS * LANES
    gate_cols = np.asarray([gate0 + h * 3 + j for j in range(3) for h in _HEAD_PERM])
    w_gate = jnp.pad(wl[:, gate_cols], ((0, 0), (0, LANES - N_GATES))).astype(BF16)
    out_rows = np.concatenate([np.arange(DIFF_WIDTH)] + [DIFF_WIDTH + h * HEAD_DIM + np.arange(HEAD_DIM) for h in _HEAD_PERM])
    wo = w_out[layer][out_rows].astype(BF16)
    wu = w_up[layer].astype(BF16)
    wd = w_down[layer].astype(BF16)
    vec = lambda a: a[layer].reshape(1, -1)
    lams = (vec(lam_q1), vec(lam_k1), vec(lam_q2), vec(lam_k2))
    cw = _compress_weights(cmp_pos[layer], cmp_w1[layer], cmp_w2[layer])

    rows_p = b * s
    tm_p = _row_tile(s, PROJ_ROWS)
    cos_p, sin_p = _rope_tables(jnp.arange(s), 1)
    xp = x_prompt.reshape(rows_p, d)
    qd, kvd, kvdb, qn, nsa_t, win_t, kvb_t, gates = _inproj(
        xp, vec(g_pre_mix), w_qkv, w_kv.T, w_gate, (cos_p, sin_p, cos_p.T, sin_p.T), tm_p, s // tm_p, b, True)
    od = _diff_prompt(qd, kvdb, lams, vec(diff_subln), b, s, _row_tile(s, DIFF_TILE), lam_init)
    cmp_kv = _compress_prompt(nsa_t, cw)
    ovl_p, exp_p = _selection_constants(s // CMP_STRIDE, s, LANES)
    ovl_rows = np.concatenate([ovl_p.T, np.ones((SUBLANES, ovl_p.shape[0])),
                               np.zeros((LANES - SEL_LANES - SUBLANES, ovl_p.shape[0]))], axis=0)
    on = _nsa_prompt(qn, gates, cmp_kv, kvb_t, jnp.asarray(ovl_rows, BF16), exp_p, b, s, NSA_QUERIES, NSA_KEYS)
    yp = _mlp(xp, od, on, wo, wu, wd, vec(g_post_mix), vec(g_pre_mlp), vec(g_post_mlp), tm_p).reshape(b, s, d)
    p_diff = kvd.reshape(1, b, s, 2, N_DIFF_HEADS, 2 * HEAD_DIM)
    token_minor = lambda a, slots: jnp.transpose(
        a.reshape(a.shape[0], slots, N_NSA_KV_HEADS, HEAD_DIM, a.shape[-1]), (0, 4, 1, 2, 3))[None]
    p_nsa = token_minor(nsa_t, 4)
    p_win = token_minor(win_t[:, :, s - WINDOW:], 2)

    rows_s = db * sq
    tm_s = _row_tile(rows_s, PROJ_ROWS)
    assert tm_s % sq == 0
    cos_s, sin_s = _rope_tables(past + jnp.arange(sq), tm_s // sq)
    xs = x_sample.reshape(rows_s, d)
    qd, kvd, qn, nsa4, win2, gates = _inproj(xs, vec(g_pre_mix), w_qkv, w_kv, w_gate, (cos_s, sin_s), tm_s, 1, db,
                                             False)
    r3 = lambda a: a.reshape(db, sq, a.shape[1])
    diff_rows = cache_diff_kv[layer].reshape(n_phys * page * DIFF_SLABS, LANES)
    od = _diff_sample(diff_rows, page_table, r3(qd), kvd, lams, vec(diff_subln), page, lam_init)
    nsa_cache_t = jnp.transpose(cache_nsa_kv[layer], (0, 2, 3, 4, 1))
    win_state_t = jnp.transpose(state_nsa_win_kv[layer], (0, 2, 3, 4, 1)).reshape(db, 2 * LANES, WINDOW)
    ovl_s, exp_s = _selection_constants(past // CMP_STRIDE, past + page, SEL_LANES)
    on, win_new_t = _nsa_sample(nsa_cache_t, page_table, win_state_t, r3(nsa4), r3(win2), r3(qn), r3(gates), cw,
                                jnp.asarray(ovl_s, BF16), exp_s)
    ys = _mlp(xs, od.reshape(rows_s, DIFF_WIDTH), on.reshape(rows_s, NSA_WIDTH), wo, wu, wd, vec(g_post_mix),
              vec(g_pre_mlp), vec(g_post_mlp), tm_s).reshape(db, sq, d)
    s_diff = kvd.reshape(1, db, sq, 2, N_DIFF_HEADS, 2 * HEAD_DIM)
    s_nsa = nsa4.reshape(1, db, sq, 4, N_NSA_KV_HEADS, HEAD_DIM)
    s_win = token_minor(win_new_t, 2)

    return yp, ys, p_diff, p_nsa, p_win, s_diff, s_nsa, s_win
```

```python
import functools
import itertools
import math

import numpy as np
import jax
import jax.numpy as jnp
from jax import lax
from jax.experimental import pallas as pl
from jax.experimental.pallas import tpu as pltpu

HEAD_DIM = 64
N_DIFF_HEADS = 4
N_NSA_HEADS = 8
N_NSA_KV_HEADS = 2
NSA_GROUP = N_NSA_HEADS // N_NSA_KV_HEADS
CMP_BLOCK = 32
CMP_STRIDE = 16
CMP_HIDDEN = 128
SEL_BLOCK = 64
SEL_TOP_N = 16
WINDOW = 512
ROPE_THETA = 10000.0
NORM_EPS = 1e-6
DIFF_WIDTH = N_DIFF_HEADS * 2 * HEAD_DIM
NSA_WIDTH = N_NSA_HEADS * HEAD_DIM
N_GATES = 3 * N_NSA_HEADS
NEG = -1e9
BIG = 1e9
SCALE = HEAD_DIM ** -0.5

LANES = 128
SUBLANES = 8
SEL_LANES = 64
SOFTMAX_ROWS = 32
PROJ_ROWS = 256
DIFF_TILE = 512
NSA_QUERIES = LANES
NSA_KEYS = 4 * LANES
SAMPLE_GROUP = 2
VMEM_LIMIT = 56 * 1024 * 1024
DIFF_SLABS = 2 * N_DIFF_HEADS
QKV_WIDTH = 3 * DIFF_WIDTH + NSA_WIDTH
KV_SLOTS = 6

F32 = jnp.float32
BF16 = jnp.bfloat16

_HEAD_PERM = tuple(g + NSA_GROUP * half for g in range(NSA_GROUP) for half in range(2))


def _dot(a, b):
    return jnp.dot(a, b, preferred_element_type=F32)


def _dot_nt(a, b):
    return lax.dot_general(a, b, (((1,), (1,)), ((), ())), preferred_element_type=F32)


def _rms(x):
    return x * lax.rsqrt(jnp.mean(x * x, axis=-1, keepdims=True) + NORM_EPS)


def _params(*sem):
    return pltpu.CompilerParams(dimension_semantics=sem, vmem_limit_bytes=VMEM_LIMIT)


def _const_spec(shape):
    nd = len(shape)
    return pl.BlockSpec(shape, lambda *_: (0,) * nd)


def _pad_rows(x, rows):
    return jnp.concatenate([x, jnp.zeros((rows - x.shape[0], x.shape[1]), x.dtype)], axis=0)


def _rotary(t, cos, sin, axis):
    idx = lax.broadcasted_iota(jnp.int32, t.shape, axis)
    first_half = (idx % HEAD_DIM) < HEAD_DIM // 2
    partner = jnp.where(first_half, pltpu.roll(t, LANES - HEAD_DIM // 2, axis), pltpu.roll(t, HEAD_DIM // 2, axis))
    return t * cos + partner * sin


def _inproj_kernel(*refs, kv_transposed):
    if kv_transposed:
        (x_ref, g_ref, w_ref, wkv_ref, wg_ref, cos_ref, sin_ref, cost_ref, sint_ref,
         qd_ref, kvd_ref, kvdb_ref, qn_ref, nsa_ref, win_ref, kvb_ref, gate_ref) = refs
    else:
        (x_ref, g_ref, w_ref, wkv_ref, wg_ref, cos_ref, sin_ref,
         qd_ref, kvd_ref, qn_ref, nsa_ref, win_ref, gate_ref) = refs
    tm = x_ref.shape[0]
    hb = (_rms(x_ref[...]) * g_ref[...]).astype(BF16)
    cos = cos_ref[...]
    sin = sin_ref[...]

    groups = [_dot(hb, w_ref[:, c0:c0 + DIFF_WIDTH]) for c0 in range(0, QKV_WIDTH, DIFF_WIDTH)]

    def proj(c0):
        return groups[c0 // DIFF_WIDTH][:, c0 % DIFF_WIDTH:c0 % DIFF_WIDTH + LANES]

    for j in range(N_DIFF_HEADS):
        qd_ref[:, j * LANES:(j + 1) * LANES] = (_rotary(proj(j * LANES), cos, sin, 1) * SCALE).astype(BF16)
        k = _rotary(proj(DIFF_WIDTH + j * LANES), cos, sin, 1)
        v = proj(2 * DIFF_WIDTH + j * LANES)
        kvd_ref[pl.ds(j, tm, stride=DIFF_SLABS), :] = k
        kvd_ref[pl.ds(N_DIFF_HEADS + j, tm, stride=DIFF_SLABS), :] = v
        if kv_transposed:
            kvdb_ref[:, j * LANES:(j + 1) * LANES] = k.astype(BF16)
            kvdb_ref[:, DIFF_WIDTH + j * LANES:DIFF_WIDTH + (j + 1) * LANES] = v.astype(BF16)
    for j in range(NSA_WIDTH // LANES):
        qn_ref[:, j * LANES:(j + 1) * LANES] = (_rotary(proj(3 * DIFF_WIDTH + j * LANES), cos, sin, 1)
                                                * SCALE).astype(BF16)
    gate_ref[...] = jax.nn.sigmoid(_dot(hb, wg_ref[...]))

    if kv_transposed:
        t_all = _dot_nt(wkv_ref[...], hb)
        cost = cost_ref[...]
        sint = sint_ref[...]
        for slot in range(KV_SLOTS):
            t = t_all[slot * LANES:(slot + 1) * LANES]
            if slot % 2 == 0:
                t = _rotary(t, cost, sint, 0)
            if slot < 4:
                nsa_ref[0, slot * LANES:(slot + 1) * LANES, :] = t
            else:
                win_ref[0, (slot - 4) * LANES:(slot - 3) * LANES, :] = t
            if slot >= 2:
                kvb_ref[0, (slot - 2) * LANES:(slot - 1) * LANES, :] = t.astype(BF16)
    else:
        t_all = _dot(hb, wkv_ref[...])
        for slot in range(KV_SLOTS):
            t = t_all[:, slot * LANES:(slot + 1) * LANES]
            if slot % 2 == 0:
                t = _rotary(t, cos, sin, 1)
            if slot < 4:
                nsa_ref[:, slot * LANES:(slot + 1) * LANES] = t
            else:
                win_ref[:, (slot - 4) * LANES:(slot - 3) * LANES] = t


def _inproj(x2d, g, w_qkv, w_kv, w_gate, tabs, tm, n_tab_blocks, batch, kv_transposed):
    rows, d = x2d.shape
    row_spec = lambda n: pl.BlockSpec((tm, n), lambda i: (i, 0))
    tab_spec = pl.BlockSpec((tm, LANES), lambda i: (i % n_tab_blocks, 0))
    in_specs = [row_spec(d), _const_spec((1, d)), _const_spec(w_qkv.shape), _const_spec(w_kv.shape),
                _const_spec(w_gate.shape), tab_spec, tab_spec]
    sds = jax.ShapeDtypeStruct
    interleaved = (sds((rows * DIFF_SLABS, LANES), F32), pl.BlockSpec((tm * DIFF_SLABS, LANES), lambda i: (i, 0)))
    if kv_transposed:
        seq = rows // batch
        nt = seq // tm
        tabt_spec = pl.BlockSpec((LANES, tm), lambda i: (0, i % nt))
        in_specs += [tabt_spec, tabt_spec]
        tspec = lambda n: pl.BlockSpec((1, n, tm), lambda i: (i // nt, 0, i % nt))
        outs = [
            (sds((rows, DIFF_WIDTH), BF16), row_spec(DIFF_WIDTH)),
            interleaved,
            (sds((rows, 2 * DIFF_WIDTH), BF16), row_spec(2 * DIFF_WIDTH)),
            (sds((rows, NSA_WIDTH), BF16), row_spec(NSA_WIDTH)),
            (sds((batch, 4 * LANES, seq), F32), tspec(4 * LANES)),
            (sds((batch, 2 * LANES, seq), F32), tspec(2 * LANES)),
            (sds((batch, 4 * LANES, seq), BF16), tspec(4 * LANES)),
            (sds((rows, LANES), F32), row_spec(LANES)),
        ]
    else:
        outs = [
            (sds((rows, DIFF_WIDTH), BF16), row_spec(DIFF_WIDTH)),
            interleaved,
            (sds((rows, NSA_WIDTH), BF16), row_spec(NSA_WIDTH)),
            (sds((rows, 4 * LANES), F32), row_spec(4 * LANES)),
            (sds((rows, 2 * LANES), F32), row_spec(2 * LANES)),
            (sds((rows, LANES), F32), row_spec(LANES)),
        ]
    return pl.pallas_call(
        functools.partial(_inproj_kernel, kv_transposed=kv_transposed),
        grid=(rows // tm,),
        in_specs=in_specs,
        out_specs=tuple(o[1] for o in outs),
        out_shape=tuple(o[0] for o in outs),
        compiler_params=_params("parallel"),
        name="inproj",
    )(x2d, g, w_qkv, w_kv, w_gate, *tabs)


def _lambda(lq1, lk1, lq2, lk2, lam_init):
    return (jnp.exp(jnp.sum(lq1[...] * lk1[...], axis=-1, keepdims=True))
            - jnp.exp(jnp.sum(lq2[...] * lk2[...], axis=-1, keepdims=True)) + lam_init)


def _split_components(q):
    lane = lax.broadcasted_iota(jnp.int32, q.shape, 1)
    low = lane < HEAD_DIM
    return jnp.concatenate([jnp.where(low, q, 0.0), jnp.where(low, 0.0, q)], axis=0).astype(BF16)


def _lane_tile(x, width):
    return jnp.concatenate([x] * (width // LANES), axis=1)


def _with_ones(v, axis):
    return jnp.concatenate([v, jnp.ones(v.shape, v.dtype)], axis=axis)


def _softmax_chunks(s_ref, p_ref, m_ref, a_ref, n_rows, adjust, unrolled):
    keys = s_ref.shape[1]

    def new_max(r0):
        rows = pl.ds(r0, SOFTMAX_ROWS)
        m_prev = m_ref[rows, :]
        m_new = jnp.maximum(m_prev, jnp.max(adjust(s_ref[rows, :], r0), axis=1, keepdims=True))
        if a_ref is not None:
            a_ref[rows, :] = jnp.exp(m_prev - m_new)
        m_ref[rows, :] = m_new

    def exponentiate(r0):
        rows = pl.ds(r0, SOFTMAX_ROWS)
        p_ref[rows, :] = jnp.exp(adjust(s_ref[rows, :], r0) - _lane_tile(m_ref[rows, :], keys)).astype(BF16)

    for one_pass in (new_max, exponentiate):
        if unrolled:
            for c in range(n_rows // SOFTMAX_ROWS):
                one_pass(c * SOFTMAX_ROWS)
        else:
            def body(c, carry, one_pass=one_pass):
                one_pass(pl.multiple_of(c * SOFTMAX_ROWS, SOFTMAX_ROWS))
                return carry
            lax.fori_loop(0, n_rows // SOFTMAX_ROWS, body, 0)


def _diff_prompt_kernel(qi_ref, ki_ref, q_ref, k_ref, v_ref, lq1, lk1, lq2, lk2, g_ref, o_ref,
                        m_ref, a_ref, acc_ref, s_ref, p_ref, *, tile, lam_init):
    t = pl.program_id(1)
    qi = qi_ref[t]
    ki = ki_ref[t]
    n_rows = 2 * tile

    @pl.when(ki == 0)
    def _init():
        m_ref[...] = jnp.full(m_ref.shape, NEG, F32)
        acc_ref[...] = jnp.zeros(acc_ref.shape, F32)

    def causal(s, r0):
        row = r0 % tile + lax.broadcasted_iota(jnp.int32, s.shape, 0)
        return jnp.where(lax.broadcasted_iota(jnp.int32, s.shape, 1) <= row, s, NEG)

    def update(diagonal):
        for h in range(N_DIFF_HEADS):
            buf = h % 2
            q2 = _split_components(q_ref[:, h * LANES:(h + 1) * LANES].astype(F32))
            s_ref[buf] = _dot_nt(q2, k_ref[:, h * LANES:(h + 1) * LANES])
            _softmax_chunks(s_ref.at[buf], p_ref.at[buf], m_ref.at[h], a_ref.at[buf], n_rows,
                            causal if diagonal else (lambda s, r0: s), unrolled=True)
            acc_ref[h] = (_lane_tile(a_ref[buf], 2 * LANES) * acc_ref[h]
                          + _dot(p_ref[buf], _with_ones(v_ref[:, h * LANES:(h + 1) * LANES], 1)))

    @pl.when(ki < qi)
    def _below_diagonal():
        update(False)

    @pl.when(ki == qi)
    def _finish():
        update(True)
        lam = _lambda(lq1, lk1, lq2, lk2, lam_init)
        for h in range(N_DIFF_HEADS):
            o = acc_ref[h, :, 0:LANES] / acc_ref[h, :, LANES:2 * LANES]
            a = o[0:tile] - lam * o[tile:2 * tile]
            o_ref[:, h * LANES:(h + 1) * LANES] = (_rms(a) * g_ref[...] * (1.0 - lam_init)).astype(BF16)


def _diff_prompt(qd, kvdb, lams, subln, b, s, tile, lam_init):
    nq = s // tile
    pairs = [(qi, ki) for qi in range(nq) for ki in range(qi + 1)]
    qi_arr = jnp.asarray([p[0] for p in pairs], jnp.int32)
    ki_arr = jnp.asarray([p[1] for p in pairs], jnp.int32)
    grid_spec = pltpu.PrefetchScalarGridSpec(
        num_scalar_prefetch=2,
        grid=(b, len(pairs)),
        in_specs=[
            pl.BlockSpec((tile, DIFF_WIDTH), lambda bi, t, qa, ka: (bi * nq + qa[t], 0)),
            pl.BlockSpec((tile, DIFF_WIDTH), lambda bi, t, qa, ka: (bi * nq + ka[t], 0)),
            pl.BlockSpec((tile, DIFF_WIDTH), lambda bi, t, qa, ka: (bi * nq + ka[t], 1)),
        ] + [_const_spec((1, HEAD_DIM))] * 4 + [_const_spec((1, 2 * HEAD_DIM))],
        out_specs=pl.BlockSpec((tile, DIFF_WIDTH), lambda bi, t, qa, ka: (bi * nq + qa[t], 0)),
        scratch_shapes=[pltpu.VMEM((N_DIFF_HEADS, 2 * tile, LANES), F32),
                        pltpu.VMEM((2, 2 * tile, LANES), F32),
                        pltpu.VMEM((N_DIFF_HEADS, 2 * tile, 2 * LANES), F32),
                        pltpu.VMEM((2, 2 * tile, tile), F32),
                        pltpu.VMEM((2, 2 * tile, tile), BF16)],
    )
    return pl.pallas_call(
        functools.partial(_diff_prompt_kernel, tile=tile, lam_init=lam_init),
        grid_spec=grid_spec,
        out_shape=jax.ShapeDtypeStruct((b * s, DIFF_WIDTH), BF16),
        compiler_params=_params("parallel", "arbitrary"),
        name="diff_prompt",
    )(qi_arr, ki_arr, qd, kvdb, kvdb, *lams, subln)


def _diff_sample_kernel(pt_ref, *refs, n_pages, page, past, sq, lam_init):
    page_refs = refs[:n_pages]
    q_ref, new_ref, lq1, lk1, lq2, lk2, g_ref, o_ref, kv_ref = refs[n_pages:]
    n_cols = 2 * (n_pages + 1) * page
    n_rows = N_DIFF_HEADS * 2 * sq
    col = lax.broadcasted_iota(jnp.int32, (n_rows, n_cols), 1)
    tpos = past + lax.broadcasted_iota(jnp.int32, (n_rows, n_cols), 0) % sq
    visible = (col % 2 == 0) & (col // 2 <= tpos)
    lam = _lambda(lq1, lk1, lq2, lk2, lam_init)

    scores = []
    for h in range(N_DIFF_HEADS):
        for i, r in enumerate(page_refs):
            kv_ref[h, i * 2 * page:(i + 1) * 2 * page, :] = r[pl.ds(h, 2 * page, stride=N_DIFF_HEADS), :].astype(BF16)
        kv_ref[h, n_pages * 2 * page:n_cols, :] = _pad_rows(new_ref[pl.ds(h, 2 * sq, stride=N_DIFF_HEADS), :],
                                                            2 * page).astype(BF16)
        q2 = _split_components(q_ref[0, :, h * LANES:(h + 1) * LANES].astype(F32))
        scores.append(_dot_nt(q2, kv_ref[h]))
    s = jnp.where(visible, jnp.concatenate(scores, axis=0), NEG)
    m = jnp.max(s, axis=1, keepdims=True)
    p = jnp.where(visible, jnp.exp(s - m), 0.0)
    l = jnp.maximum(jnp.sum(p, axis=1, keepdims=True), 1e-30)
    p = pltpu.roll(p, 1, 1).astype(BF16)
    for h in range(N_DIFF_HEADS):
        rows = slice(h * 2 * sq, (h + 1) * 2 * sq)
        o = _dot(p[rows], kv_ref[h]) / l[rows]
        a = o[0:sq] - lam * o[sq:2 * sq]
        o_ref[0, :, h * LANES:(h + 1) * LANES] = (_rms(a) * g_ref[...] * (1.0 - lam_init)).astype(BF16)


def _diff_sample(cache_rows, page_table, qd3, kvd_rows, lams, subln, page, lam_init):
    db, sq, _ = qd3.shape
    n_pages = page_table.shape[1]
    past = n_pages * page
    page_specs = [pl.BlockSpec((page * DIFF_SLABS, LANES), functools.partial(lambda bi, pt, p: (pt[bi, p], 0), p=p))
                  for p in range(n_pages)]
    grid_spec = pltpu.PrefetchScalarGridSpec(
        num_scalar_prefetch=1,
        grid=(db,),
        in_specs=page_specs + [
            pl.BlockSpec((1, sq, DIFF_WIDTH), lambda bi, pt: (bi, 0, 0)),
            pl.BlockSpec((sq * DIFF_SLABS, LANES), lambda bi, pt: (bi, 0)),
        ] + [_const_spec((1, HEAD_DIM))] * 4 + [_const_spec((1, 2 * HEAD_DIM))],
        out_specs=pl.BlockSpec((1, sq, DIFF_WIDTH), lambda bi, pt: (bi, 0, 0)),
        scratch_shapes=[pltpu.VMEM((N_DIFF_HEADS, 2 * (n_pages + 1) * page, LANES), BF16)],
    )
    return pl.pallas_call(
        functools.partial(_diff_sample_kernel, n_pages=n_pages, page=page, past=past, sq=sq, lam_init=lam_init),
        grid_spec=grid_spec,
        out_shape=jax.ShapeDtypeStruct((db, sq, DIFF_WIDTH), BF16),
        compiler_params=_params("parallel"),
        name="diff_sample",
    )(page_table, *([cache_rows] * n_pages), qd3, kvd_rows, *lams, subln)


def _chunk_pitch(n_chunks):
    assert n_chunks % (2 * SUBLANES) == 0
    return n_chunks + SUBLANES


def _store_token_rows(rows_ref, slot, first_chunk, rows, n_chunks):
    for c in range(rows.shape[0] // CMP_STRIDE):
        rows_ref[slot, pl.ds(first_chunk + c, CMP_STRIDE, stride=_chunk_pitch(n_chunks)), :] = (
            rows[c * CMP_STRIDE:(c + 1) * CMP_STRIDE])


def _compress(rows_ref, slot, n_chunks, pos_a, pos_b, w_a, w_b, w2):
    pitch = _chunk_pitch(n_chunks)
    xs = jnp.concatenate([rows_ref[slot, j * pitch:j * pitch + n_chunks, :] for j in range(CMP_STRIDE)],
                         axis=1)
    a = _dot((xs + pos_a).astype(BF16), w_a)
    b = _dot((xs + pos_b).astype(BF16), w_b)
    hidden = jax.nn.gelu(a + pltpu.roll(b, n_chunks - 1, 0))
    out = _dot(hidden.astype(BF16), w2)
    rid = lax.broadcasted_iota(jnp.int32, out.shape, 0)
    return jnp.where(rid < n_chunks - 1, out, 0.0)


def _compress_prompt_kernel(x_ref, pa_ref, pb_ref, wa_ref, wb_ref, w2_ref, o_ref, rows_ref):
    seq = x_ref.shape[2]
    for slot in range(2):
        for c in range(seq // LANES):
            _store_token_rows(rows_ref, slot, c * (LANES // CMP_STRIDE),
                              x_ref[0, slot * LANES:(slot + 1) * LANES, c * LANES:(c + 1) * LANES].T,
                              seq // CMP_STRIDE)
        o_ref[0, slot] = _compress(rows_ref, slot, seq // CMP_STRIDE, pa_ref[slot], pb_ref[slot], wa_ref[slot],
                                   wb_ref[slot], w2_ref[slot]).astype(BF16)


def _compress_prompt(nsa_t, cw):
    b, _, seq = nsa_t.shape
    nch = seq // CMP_STRIDE
    return pl.pallas_call(
        _compress_prompt_kernel,
        grid=(b,),
        in_specs=[pl.BlockSpec((1, 2 * LANES, seq), lambda i: (i, 0, 0))] + [_const_spec(a.shape) for a in cw],
        out_specs=pl.BlockSpec((1, 2, nch, LANES), lambda i: (i, 0, 0, 0)),
        out_shape=jax.ShapeDtypeStruct((b, 2, nch, LANES), BF16),
        scratch_shapes=[pltpu.VMEM((2, CMP_STRIDE * _chunk_pitch(nch), LANES), F32)],
        compiler_params=_params("parallel"),
        name="compress_prompt",
    )(nsa_t, *cw)


def _cmp_probs(qs, kc, tpos):
    s = _dot_nt(qs, kc)
    cend = lax.broadcasted_iota(jnp.int32, s.shape, 1) * CMP_STRIDE + (CMP_BLOCK - 1)
    vis = cend <= tpos
    s = jnp.where(vis, s, NEG)
    m = jnp.max(s, axis=1, keepdims=True)
    p = jnp.where(vis, jnp.exp(s - m), 0.0)
    return (p / jnp.maximum(jnp.sum(p, axis=1, keepdims=True), 1e-30)).astype(BF16)


def _block_scores(imp, blk, cur):
    forced = (blk == 0) | (blk == cur) | (blk == cur - 1)
    return jnp.where(blk > cur, NEG, jnp.where(forced, BIG, imp))


def _select_blocks(imp, tpos2):
    blk = lax.broadcasted_iota(jnp.int32, imp.shape, 1)
    score = _block_scores(imp, blk, tpos2 // SEL_BLOCK)
    rank = jnp.zeros(imp.shape, F32)
    for j in range(SEL_LANES):
        sj = score[:, j:j + 1]
        tie = jnp.where(blk > j, 1.0, 0.0)
        rank = rank + jnp.where(sj > score, 1.0, 0.0) + jnp.where(sj == score, tie, 0.0)
    return jnp.where(rank < SEL_TOP_N, 1.0, 0.0).astype(BF16)


def _select_blocks_t(imp_t, tpos_t):
    n = imp_t.shape[1]
    blk = lax.broadcasted_iota(jnp.int32, imp_t.shape, 0)
    score = _block_scores(imp_t, blk, tpos_t // SEL_BLOCK)
    n_groups = SEL_LANES // SUBLANES
    groups = [score[g * SUBLANES:(g + 1) * SUBLANES] for g in range(n_groups)]
    ranks = [jnp.zeros((SUBLANES, n), F32) for _ in range(n_groups)]
    rid = lax.broadcasted_iota(jnp.int32, (SUBLANES, n), 0)
    for j in range(SEL_LANES):
        sj = score[j:j + 1]
        for g in range(n_groups):
            if g * SUBLANES > j:
                beats = jnp.where(sj >= groups[g], 1.0, 0.0)
            elif (g + 1) * SUBLANES - 1 < j:
                beats = jnp.where(sj > groups[g], 1.0, 0.0)
            else:
                beats = jnp.where(rid + g * SUBLANES > j, jnp.where(sj >= groups[g], 1.0, 0.0),
                                  jnp.where(sj > groups[g], 1.0, 0.0))
            ranks[g] = ranks[g] + beats
    sel_t = jnp.concatenate([jnp.where(r < SEL_TOP_N, 1.0, 0.0) for r in ranks]
                            + [jnp.zeros((LANES - SEL_LANES, n), F32)], axis=0)
    return sel_t.T


def _stack_queries(q, n_q):
    lane = lax.broadcasted_iota(jnp.int32, (n_q, LANES), 1)
    low = lane < HEAD_DIM
    groups = [q[:, g * LANES:(g + 1) * LANES].astype(F32) for g in range(NSA_GROUP)]
    return jnp.concatenate([jnp.where(low, t, 0.0) for t in groups] + [jnp.where(low, 0.0, t) for t in groups],
                           axis=0).astype(BF16)


def _combine(o_c, o_s, o_w, gates, n_q, store):
    lane = lax.broadcasted_iota(jnp.int32, (n_q, LANES), 1)
    low = lane < HEAD_DIM
    for g in range(NSA_GROUP):
        def pick(o):
            return jnp.where(low, o[g * n_q:(g + 1) * n_q], o[(NSA_GROUP + g) * n_q:(NSA_GROUP + g + 1) * n_q])

        def gate(j):
            c = j * N_NSA_HEADS + 2 * g
            return jnp.where(low, gates[:, c:c + 1], gates[:, c + 1:c + 2])

        store(g, gate(0) * pick(o_c) + gate(1) * pick(o_s) + gate(2) * pick(o_w))


def _per_kv_head(x, n_q):
    return x.reshape(N_NSA_KV_HEADS, NSA_GROUP, n_q, x.shape[-1])


def _nsa_prompt_kernel(qn_ref, gate_ref, cmp_ref, kv_ref, ovl_ref, exp_ref, o_ref,
                       m_ref, a_ref, acc_ref, s_ref, p_ref, sw_ref, pw_ref, sc_ref, pc_ref, qse_ref, *, tq, tk):
    q_start = pl.program_id(1) * tq
    m_rows = N_NSA_HEADS * tq
    qse_ref[:, 0:LANES] = _stack_queries(qn_ref[...], tq)
    tpos2_t = q_start + lax.broadcasted_iota(jnp.int32, (1, N_NSA_KV_HEADS * tq), 1) % tq
    m_ref[...] = jnp.full(m_ref.shape, NEG, F32)
    acc_ref[...] = jnp.zeros(acc_ref.shape, F32)

    m_ref[2] = jnp.full(m_ref.shape[1:], 0.5 * NEG, F32)
    sc_ref[...] = _dot_nt(qse_ref[:, 0:LANES], cmp_ref[0, 0])

    def block_ended(s, r0):
        qpos = q_start + r0 % tq + lax.broadcasted_iota(jnp.int32, s.shape, 0)
        cend = lax.broadcasted_iota(jnp.int32, s.shape, 1) * CMP_STRIDE + (CMP_BLOCK - 1)
        return jnp.where(cend <= qpos, s, NEG)

    _softmax_chunks(sc_ref, pc_ref, m_ref.at[2], None, m_rows, block_ended, unrolled=True)
    o_c = _dot(pc_ref[...], _with_ones(cmp_ref[0, 1], 1))
    o_c = o_c[:, 0:LANES] / jnp.maximum(o_c[:, LANES:2 * LANES], 1e-30)
    imp_t = _dot_nt(ovl_ref[...], pc_ref[...])
    imp_t = imp_t[0:SEL_LANES] / jnp.maximum(imp_t[SEL_LANES:SEL_LANES + 1], 1e-30)
    imp_t = jnp.concatenate(
        [(imp_t[:, (4 * h) * tq:(4 * h + 1) * tq] + imp_t[:, (4 * h + 1) * tq:(4 * h + 2) * tq])
         + (imp_t[:, (4 * h + 2) * tq:(4 * h + 3) * tq] + imp_t[:, (4 * h + 3) * tq:(4 * h + 4) * tq])
         for h in range(N_NSA_KV_HEADS)], axis=1)
    sel = _select_blocks_t(imp_t, tpos2_t)

    block_bias = jnp.where(sel > 0.5, 0.0, NEG).astype(BF16)
    for r in range(N_NSA_HEADS):
        h = r // NSA_GROUP
        qse_ref[r * tq:(r + 1) * tq, LANES:2 * LANES] = block_bias[h * tq:(h + 1) * tq]

    def slc_tile(k0, width, diagonal):
        cols = pl.ds(k0, width)
        s_tile = s_ref.at[:, pl.ds(0, width)]
        p_tile = p_ref.at[:, pl.ds(0, width)]
        s_tile[...] = _dot(qse_ref[...], jnp.concatenate([kv_ref[0, 0:LANES, cols], exp_ref[:, cols]], axis=0))

        def causal(s, r0):
            qpos = q_start + r0 % tq + lax.broadcasted_iota(jnp.int32, s.shape, 0)
            return jnp.where(k0 + lax.broadcasted_iota(jnp.int32, s.shape, 1) <= qpos, s, NEG)

        _softmax_chunks(s_tile, p_tile, m_ref.at[0], a_ref, m_rows, causal if diagonal else (lambda s, r0: s),
                        unrolled=True)
        acc_ref[...] = (_lane_tile(a_ref[...], 2 * LANES) * acc_ref[...]
                        + _dot_nt(p_tile[...], _with_ones(kv_ref[0, LANES:2 * LANES, cols], 0)))

    def below_diagonal(kt, carry):
        slc_tile(pl.multiple_of(kt * tk, tk), tk, False)
        return carry

    last_tile = (q_start + tq - 1) // tk
    lax.fori_loop(0, last_tile, below_diagonal, 0)
    slc_tile(pl.multiple_of(last_tile * tk, tk), tk, True)
    o_s = acc_ref[:, 0:LANES] / acc_ref[:, LANES:2 * LANES]

    span = WINDOW + tq
    w0 = pl.multiple_of(jnp.maximum(q_start - WINDOW, 0), tq)
    sw_ref[...] = _dot(qse_ref[:, 0:LANES], kv_ref[0, 2 * LANES:3 * LANES, pl.ds(w0, span)])

    def windowed(s, r0):
        qpos = q_start + r0 % tq + lax.broadcasted_iota(jnp.int32, s.shape, 0)
        dist = qpos - (w0 + lax.broadcasted_iota(jnp.int32, s.shape, 1))
        return jnp.where(lax.bitcast_convert_type(dist, jnp.uint32) < WINDOW, s, NEG)

    _softmax_chunks(sw_ref, pw_ref, m_ref.at[1], a_ref, m_rows, windowed, unrolled=True)
    o_w = _dot_nt(pw_ref[...], _with_ones(kv_ref[0, 3 * LANES:4 * LANES, pl.ds(w0, span)], 0))
    o_w = o_w[:, 0:LANES] / o_w[:, LANES:2 * LANES]

    def store(g, val):
        o_ref[:, g * LANES:(g + 1) * LANES] = val.astype(BF16)

    _combine(o_c, o_s, o_w, gate_ref[...], tq, store)


def _nsa_prompt(qn, gates, cmp_kv, kv_t, ovl_t, expand, b, s, tq, tk):
    nq = s // tq
    nch = cmp_kv.shape[2]
    m_rows = N_NSA_HEADS * tq
    return pl.pallas_call(
        functools.partial(_nsa_prompt_kernel, tq=tq, tk=tk),
        grid=(b, nq),
        in_specs=[
            pl.BlockSpec((tq, NSA_WIDTH), lambda bi, qi: (bi * nq + qi, 0)),
            pl.BlockSpec((tq, LANES), lambda bi, qi: (bi * nq + qi, 0)),
            pl.BlockSpec((1, 2, nch, LANES), lambda bi, qi: (bi, 0, 0, 0)),
            pl.BlockSpec((1, 4 * LANES, s), lambda bi, qi: (bi, 0, 0)),
            _const_spec(ovl_t.shape), _const_spec(expand.shape),
        ],
        out_specs=pl.BlockSpec((tq, NSA_WIDTH), lambda bi, qi: (bi * nq + qi, 0)),
        out_shape=jax.ShapeDtypeStruct((b * s, NSA_WIDTH), BF16),
        scratch_shapes=[pltpu.VMEM((3, m_rows, LANES), F32),
                        pltpu.VMEM((m_rows, LANES), F32),
                        pltpu.VMEM((m_rows, 2 * LANES), F32),
                        pltpu.VMEM((m_rows, tk), F32), pltpu.VMEM((m_rows, tk), BF16),
                        pltpu.VMEM((m_rows, WINDOW + tq), F32), pltpu.VMEM((m_rows, WINDOW + tq), BF16),
                        pltpu.VMEM((m_rows, nch), F32), pltpu.VMEM((m_rows, nch), BF16),
                        pltpu.VMEM((m_rows, 2 * LANES), BF16)],
        compiler_params=_params("parallel", "arbitrary"),
        name="nsa_prompt",
    )(qn, gates, cmp_kv, kv_t, ovl_t, expand)


def _nsa_sample_kernel(pt_ref, *refs, n_pages, group, **static):
    page_refs = refs[:group * n_pages]
    per_batch = refs[group * n_pages:group * n_pages + 5]
    consts = refs[group * n_pages + 5:-3]
    o_ref, wout_ref, rows_ref = refs[-3:]
    bodies = []
    for u in range(group):
        one = pl.ds(u, 1)
        bodies.append(_nsa_sample_one(page_refs[u * n_pages:(u + 1) * n_pages], *[r.at[one] for r in per_batch],
                                      *consts, o_ref.at[one], wout_ref.at[one], rows_ref.at[pl.ds(2 * u, 2)],
                                      n_pages=n_pages, **static))
    for _ in itertools.zip_longest(*bodies):
        pass


def _nsa_sample_one(page_refs, win_ref, nsanew_ref, winnew_ref, qn_ref, gate_ref, pa_ref, pb_ref, wa_ref, wb_ref,
                    w2_ref, ovl_ref, exp_ref, o_ref, wout_ref, rows_ref, *, n_pages, page, past, sq):
    m_rows = N_NSA_HEADS * sq
    qs = _stack_queries(qn_ref[0], sq)
    tpos = past + lax.broadcasted_iota(jnp.int32, (m_rows, 1), 0) % sq
    tpos2 = past + lax.broadcasted_iota(jnp.int32, (N_NSA_KV_HEADS * sq, 1), 0) % sq

    def page_t(i, slot):
        return page_refs[i][0, slot].reshape(N_NSA_KV_HEADS * HEAD_DIM, page)

    cmp_kv = []
    for slot in range(2):
        for i in range(n_pages):
            _store_token_rows(rows_ref, slot, i * (page // CMP_STRIDE), page_t(i, slot).T, past // CMP_STRIDE)
        yield
        cmp_kv.append(_compress(rows_ref, slot, past // CMP_STRIDE, pa_ref[slot], pb_ref[slot], wa_ref[slot],
                                wb_ref[slot], w2_ref[slot]).astype(BF16))
        yield
    p_c = _cmp_probs(qs, cmp_kv[0], tpos)
    o_c = _dot(p_c, cmp_kv[1])
    imp = _per_kv_head(_dot(p_c, ovl_ref[...]), sq)
    imp = ((imp[:, 0] + imp[:, 1]) + (imp[:, 2] + imp[:, 3])).reshape(N_NSA_KV_HEADS * sq, SEL_LANES)
    yield
    sel = _select_blocks(imp, tpos2)
    yield

    nsanew = _pad_rows(nsanew_ref[0], page)
    n_keys = (n_pages + 1) * page
    s = jnp.concatenate([_dot(qs, page_t(i, 2).astype(BF16)) for i in range(n_pages)]
                        + [_dot_nt(qs, nsanew[:, 2 * LANES:3 * LANES].astype(BF16))], axis=1)
    yield
    picked = _dot(sel, exp_ref[...])
    kpos = lax.broadcasted_iota(jnp.int32, (1, n_keys), 1)
    bias = jnp.where(jnp.where(kpos <= tpos2, picked, 0.0) > 0.5, 0.0, NEG)
    s = (_per_kv_head(s, sq) + bias.reshape(N_NSA_KV_HEADS, 1, sq, n_keys)).reshape(m_rows, n_keys)
    m = jnp.max(s, axis=1, keepdims=True)
    p = jnp.exp(s - m)
    l = jnp.sum(p, axis=1, keepdims=True)
    pb = p.astype(BF16)
    acc = _dot(pb[:, past:n_keys], nsanew[:, 3 * LANES:4 * LANES].astype(BF16))
    for i in range(n_pages):
        acc = acc + _dot_nt(pb[:, i * page:(i + 1) * page], page_t(i, 3).astype(BF16))
    o_s = acc / l
    yield

    winnew = _pad_rows(winnew_ref[0], page)
    s = jnp.concatenate([_dot(qs, win_ref[0, 0:LANES, :].astype(BF16)),
                         _dot_nt(qs, winnew[:, 0:LANES].astype(BF16))], axis=1)
    col = lax.broadcasted_iota(jnp.int32, (1, WINDOW + page), 1)
    dist = tpos - (past - WINDOW + col)
    vis = (dist >= 0) & (dist < WINDOW)
    s = jnp.where(vis, s, NEG)
    m = jnp.max(s, axis=1, keepdims=True)
    p = jnp.where(vis, jnp.exp(s - m), 0.0)
    pb = p.astype(BF16)
    o_w = ((_dot_nt(pb[:, 0:WINDOW], win_ref[0, LANES:2 * LANES, :].astype(BF16))
            + _dot(pb[:, WINDOW:WINDOW + page], winnew[:, LANES:2 * LANES].astype(BF16)))
           / jnp.maximum(jnp.sum(p, axis=1, keepdims=True), 1e-30))

    def store(g, val):
        o_ref[0, :, g * LANES:(g + 1) * LANES] = val.astype(BF16)

    yield
    _combine(o_c, o_s, o_w, gate_ref[0], sq, store)

    extended = jnp.concatenate([win_ref[0], winnew.T], axis=1)
    wout_ref[0] = extended[:, sq:sq + WINDOW]


def _nsa_sample(cache_t, page_table, win_t, nsanew3, winnew3, qn3, gates3, cw, ovl, expand):
    db, sq, _ = qn3.shape
    n_pages = page_table.shape[1]
    page = cache_t.shape[-1]
    past = n_pages * page
    group = SAMPLE_GROUP if db % SAMPLE_GROUP == 0 else 1
    page_specs = [pl.BlockSpec((1,) + cache_t.shape[1:],
                               functools.partial(lambda bi, pt, u, p: (pt[bi * group + u, p], 0, 0, 0, 0), u=u, p=p))
                  for u in range(group) for p in range(n_pages)]
    per_b = lambda shape: pl.BlockSpec((group,) + shape, lambda bi, pt: (bi, 0, 0))
    grid_spec = pltpu.PrefetchScalarGridSpec(
        num_scalar_prefetch=1,
        grid=(db // group,),
        in_specs=page_specs + [
            per_b((2 * LANES, WINDOW)), per_b((sq, 4 * LANES)), per_b((sq, 2 * LANES)), per_b((sq, NSA_WIDTH)),
            per_b((sq, LANES)),
        ] + [_const_spec(a.shape) for a in cw] + [_const_spec(ovl.shape), _const_spec(expand.shape)],
        out_specs=(per_b((sq, NSA_WIDTH)), per_b((2 * LANES, WINDOW))),
        scratch_shapes=[pltpu.VMEM((2 * group, CMP_STRIDE * _chunk_pitch(past // CMP_STRIDE), LANES), F32)],
    )
    return pl.pallas_call(
        functools.partial(_nsa_sample_kernel, n_pages=n_pages, group=group, page=page, past=past, sq=sq),
        grid_spec=grid_spec,
        out_shape=(jax.ShapeDtypeStruct((db, sq, NSA_WIDTH), BF16),
                   jax.ShapeDtypeStruct((db, 2 * LANES, WINDOW), F32)),
        compiler_params=_params("parallel"),
        name="nsa_sample",
    )(page_table, *([cache_t] * (group * n_pages)), win_t, nsanew3, winnew3, qn3, gates3, *cw, ovl, expand)


def _mlp_kernel(x_ref, od_ref, on_ref, wo_ref, wu_ref, wd_ref, g1_ref, g2_ref, g3_ref, y_ref, *, ff_chunk):
    mix = _dot(od_ref[...], wo_ref[0:DIFF_WIDTH, :]) + _dot(on_ref[...], wo_ref[DIFF_WIDTH:DIFF_WIDTH + NSA_WIDTH, :])
    x1 = x_ref[...] + _rms(mix) * g1_ref[...]
    hm = (_rms(x1) * g2_ref[...]).astype(BF16)
    d_ff = wu_ref.shape[1]
    ff = jnp.zeros(x1.shape, F32)
    for c in range(d_ff // ff_chunk):
        u = jnp.maximum(_dot(hm, wu_ref[:, c * ff_chunk:(c + 1) * ff_chunk]), 0.0)
        ff = ff + _dot((u * u).astype(BF16), wd_ref[c * ff_chunk:(c + 1) * ff_chunk, :])
    y_ref[...] = x1 + _rms(ff) * g3_ref[...]


def _mlp(x2d, od, on, w_out, w_up, w_down, g1, g2, g3, tm):
    rows, d = x2d.shape
    row_spec = lambda n: pl.BlockSpec((tm, n), lambda i: (i, 0))
    resident = lambda a: pl.BlockSpec(a.shape, lambda i: (0, 0), pipeline_mode=pl.Buffered(1))
    return pl.pallas_call(
        functools.partial(_mlp_kernel, ff_chunk=1024),
        grid=(rows // tm,),
        in_specs=[row_spec(d), row_spec(DIFF_WIDTH), row_spec(NSA_WIDTH), resident(w_out), resident(w_up),
                  resident(w_down), _const_spec((1, d)), _const_spec((1, d)), _const_spec((1, d))],
        out_specs=row_spec(d),
        out_shape=jax.ShapeDtypeStruct((rows, d), F32),
        compiler_params=_params("parallel"),
        name="mlp",
    )(x2d, od, on, w_out, w_up, w_down, g1, g2, g3)


def _rope_tables(pos, reps):
    half = HEAD_DIM // 2
    inv = ROPE_THETA ** (-jnp.arange(half, dtype=F32) / half)
    ang = pos.astype(F32)[:, None] * inv[None, :]
    cos, sin = jnp.cos(ang), jnp.sin(ang)
    cos_t = jnp.tile(cos, (reps, LANES // half))
    sin_t = jnp.tile(jnp.concatenate([-sin, sin], axis=1), (reps, LANES // HEAD_DIM))
    return cos_t, sin_t


def _compress_weights(cmp_pos, cmp_w1, cmp_w2):
    eye = jnp.eye(N_NSA_KV_HEADS, dtype=F32)
    w1 = cmp_w1.reshape(2, CMP_BLOCK, HEAD_DIM, CMP_HIDDEN)

    def expand_w1(w):
        t = jnp.einsum('sldf,hg->slhdgf', w, eye)
        return t.reshape(2, CMP_STRIDE * LANES, N_NSA_KV_HEADS * CMP_HIDDEN).astype(BF16)

    def expand_pos(p):
        return jnp.tile(p[:, :, None, :], (1, 1, N_NSA_KV_HEADS, 1)).reshape(2, 1, CMP_STRIDE * LANES)

    w2 = jnp.einsum('sfd,hg->shfgd', cmp_w2, eye).reshape(2, N_NSA_KV_HEADS * CMP_HIDDEN, LANES).astype(BF16)
    return (expand_pos(cmp_pos[:, :CMP_STRIDE]), expand_pos(cmp_pos[:, CMP_STRIDE:]),
            expand_w1(w1[:, :CMP_STRIDE]), expand_w1(w1[:, CMP_STRIDE:]), w2)


def _selection_constants(n_chunks, n_keys, expand_rows):
    n = np.arange(n_chunks)
    m = np.arange(SEL_LANES)
    cs, ss = n * CMP_STRIDE, m * SEL_BLOCK
    ovl = (cs[:, None] < ss[None, :] + SEL_BLOCK) & (cs[:, None] + CMP_BLOCK > ss[None, :]) & (n[:, None] < n_chunks - 1)
    expand = (np.arange(n_keys)[None, :] // SEL_BLOCK) == np.arange(expand_rows)[:, None]
    return ovl, jnp.asarray(expand, BF16)


def _row_tile(rows, want):
    t = min(rows, want)
    assert rows % t == 0
    return t


def kernel(x_prompt, x_sample, cache_diff_kv, cache_nsa_kv, state_nsa_win_kv, page_table, w_in, w_out, w_up, w_down,
           g_pre_mix, g_post_mix, g_pre_mlp, g_post_mlp, lam_q1, lam_k1, lam_q2, lam_k2, diff_subln, cmp_pos,
           cmp_w1, cmp_w2):
    depth = w_in.shape[0]
    assert depth == 1, "one layer: the sample group's paged caches are read in place"
    b, s, d = x_prompt.shape
    db, sq, _ = x_sample.shape
    n_phys, page = cache_diff_kv.shape[1:3]
    n_pages = page_table.shape[1]
    past = n_pages * page
    assert s % (4 * LANES) == 0 and s >= WINDOW + LANES and s <= SEL_LANES * SEL_BLOCK
    assert past >= WINDOW and sq < CMP_STRIDE and sq <= page and past + sq <= SEL_LANES * SEL_BLOCK
    assert state_nsa_win_kv.shape[2] == WINDOW and page == LANES
    layer = 0
    lam_init = 0.8 - 0.6 * math.exp(-0.3 * layer)

    wl = w_in[layer]
    qn0 = 3 * DIFF_WIDTH
    qn_cols = np.concatenate([qn0 + h * HEAD_DIM + np.arange(HEAD_DIM) for h in _HEAD_PERM])
    w_qkv = wl[:, np.concatenate([np.arange(qn0), qn_cols])].astype(BF16)
    w_kv = wl[:, QKV_WIDTH:QKV_WIDTH + KV_SLOTS * LANES].astype(BF16)
    gate0 = QKV_WIDTH + KV_SLOTS * LANES
    gate_cols = np.asarray([gate0 + h * 3 + j for j in range(3) for h in _HEAD_PERM])
    w_gate = jnp.pad(wl[:, gate_cols], ((0, 0), (0, LANES - N_GATES))).astype(BF16)
    out_rows = np.concatenate([np.arange(DIFF_WIDTH)] + [DIFF_WIDTH + h * HEAD_DIM + np.arange(HEAD_DIM) for h in _HEAD_PERM])
    wo = w_out[layer][out_rows].astype(BF16)
    wu = w_up[layer].astype(BF16)
    wd = w_down[layer].astype(BF16)
    vec = lambda a: a[layer].reshape(1, -1)
    lams = (vec(lam_q1), vec(lam_k1), vec(lam_q2), vec(lam_k2))
    cw = _compress_weights(cmp_pos[layer], cmp_w1[layer], cmp_w2[layer])

    rows_p = b * s
    tm_p = _row_tile(s, PROJ_ROWS)
    cos_p, sin_p = _rope_tables(jnp.arange(s), 1)
    xp = x_prompt.reshape(rows_p, d)
    qd, kvd, kvdb, qn, nsa_t, win_t, kvb_t, gates = _inproj(
        xp, vec(g_pre_mix), w_qkv, w_kv.T, w_gate, (cos_p, sin_p, cos_p.T, sin_p.T), tm_p, s // tm_p, b, True)
    od = _diff_prompt(qd, kvdb, lams, vec(diff_subln), b, s, _row_tile(s, DIFF_TILE), lam_init)
    cmp_kv = _compress_prompt(nsa_t, cw)
    ovl_p, exp_p = _selection_constants(s // CMP_STRIDE, s, LANES)
    ovl_rows = np.concatenate([ovl_p.T, np.ones((SUBLANES, ovl_p.shape[0])),
                               np.zeros((LANES - SEL_LANES - SUBLANES, ovl_p.shape[0]))], axis=0)
    on = _nsa_prompt(qn, gates, cmp_kv, kvb_t, jnp.asarray(ovl_rows, BF16), exp_p, b, s, NSA_QUERIES, NSA_KEYS)
    yp = _mlp(xp, od, on, wo, wu, wd, vec(g_post_mix), vec(g_pre_mlp), vec(g_post_mlp), tm_p).reshape(b, s, d)
    p_diff = kvd.reshape(1, b, s, 2, N_DIFF_HEADS, 2 * HEAD_DIM)
    token_minor = lambda a, slots: jnp.transpose(
        a.reshape(a.shape[0], slots, N_NSA_KV_HEADS, HEAD_DIM, a.shape[-1]), (0, 4, 1, 2, 3))[None]
    p_nsa = token_minor(nsa_t, 4)
    p_win = token_minor(win_t[:, :, s - WINDOW:], 2)

    rows_s = db * sq
    tm_s = _row_tile(rows_s, PROJ_ROWS)
    assert tm_s % sq == 0
    cos_s, sin_s = _rope_tables(past + jnp.arange(sq), tm_s // sq)
    xs = x_sample.reshape(rows_s, d)
    qd, kvd, qn, nsa4, win2, gates = _inproj(xs, vec(g_pre_mix), w_qkv, w_kv, w_gate, (cos_s, sin_s), tm_s, 1, db,
                                             False)
    r3 = lambda a: a.reshape(db, sq, a.shape[1])
    diff_rows = cache_diff_kv[layer].reshape(n_phys * page * DIFF_SLABS, LANES)
    od = _diff_sample(diff_rows, page_table, r3(qd), kvd, lams, vec(diff_subln), page, lam_init)
    nsa_cache_t = jnp.transpose(cache_nsa_kv[layer], (0, 2, 3, 4, 1))
    win_state_t = jnp.transpose(state_nsa_win_kv[layer], (0, 2, 3, 4, 1)).reshape(db, 2 * LANES, WINDOW)
    ovl_s, exp_s = _selection_constants(past // CMP_STRIDE, past + page, SEL_LANES)
    on, win_new_t = _nsa_sample(nsa_cache_t, page_table, win_state_t, r3(nsa4), r3(win2), r3(qn), r3(gates), cw,
                                jnp.asarray(ovl_s, BF16), exp_s)
    ys = _mlp(xs, od.reshape(rows_s, DIFF_WIDTH), on.reshape(rows_s, NSA_WIDTH), wo, wu, wd, vec(g_post_mix),
              vec(g_pre_mlp), vec(g_post_mlp), tm_s).reshape(db, sq, d)
    s_diff = kvd.reshape(1, db, sq, 2, N_DIFF_HEADS, 2 * HEAD_DIM)
    s_nsa = nsa4.reshape(1, db, sq, 4, N_NSA_KV_HEADS, HEAD_DIM)
    s_win = token_minor(win_new_t, 2)

    return yp, ys, p_diff, p_nsa, p_win, s_diff, s_nsa, s_win
```

```python
import functools
import itertools
import math

import numpy as np
import jax
import jax.numpy as jnp
from jax import lax
from jax.experimental import pallas as pl
from jax.experimental.pallas import tpu as pltpu

HEAD_DIM = 64
N_DIFF_HEADS = 4
N_NSA_HEADS = 8
N_NSA_KV_HEADS = 2
NSA_GROUP = N_NSA_HEADS // N_NSA_KV_HEADS
CMP_BLOCK = 32
CMP_STRIDE = 16
CMP_HIDDEN = 128
SEL_BLOCK = 64
SEL_TOP_N = 16
WINDOW = 512
ROPE_THETA = 10000.0
NORM_EPS = 1e-6
DIFF_WIDTH = N_DIFF_HEADS * 2 * HEAD_DIM
NSA_WIDTH = N_NSA_HEADS * HEAD_DIM
N_GATES = 3 * N_NSA_HEADS
NEG = -1e9
BIG = 1e9
SCALE = HEAD_DIM ** -0.5

LANES = 128
SUBLANES = 8
SEL_LANES = 64
SOFTMAX_ROWS = 32
PROJ_ROWS = 256
DIFF_TILE = 512
NSA_QUERIES = LANES
NSA_KEYS = 4 * LANES
NSA_TILES = 2
SAMPLE_GROUP = 2
VMEM_LIMIT = 56 * 1024 * 1024
DIFF_SLABS = 2 * N_DIFF_HEADS
QKV_WIDTH = 3 * DIFF_WIDTH + NSA_WIDTH
KV_SLOTS = 6

F32 = jnp.float32
BF16 = jnp.bfloat16

_HEAD_PERM = tuple(g + NSA_GROUP * half for g in range(NSA_GROUP) for half in range(2))


def _dot(a, b):
    return jnp.dot(a, b, preferred_element_type=F32)


def _dot_nt(a, b):
    return lax.dot_general(a, b, (((1,), (1,)), ((), ())), preferred_element_type=F32)


def _rms(x):
    return x * lax.rsqrt(jnp.mean(x * x, axis=-1, keepdims=True) + NORM_EPS)


def _params(*sem):
    return pltpu.CompilerParams(dimension_semantics=sem, vmem_limit_bytes=VMEM_LIMIT)


def _const_spec(shape):
    nd = len(shape)
    return pl.BlockSpec(shape, lambda *_: (0,) * nd)


def _pad_rows(x, rows):
    return jnp.concatenate([x, jnp.zeros((rows - x.shape[0], x.shape[1]), x.dtype)], axis=0)


def _rotary(t, cos, sin, axis):
    idx = lax.broadcasted_iota(jnp.int32, t.shape, axis)
    first_half = (idx % HEAD_DIM) < HEAD_DIM // 2
    partner = jnp.where(first_half, pltpu.roll(t, LANES - HEAD_DIM // 2, axis), pltpu.roll(t, HEAD_DIM // 2, axis))
    return t * cos + partner * sin


def _inproj_kernel(*refs, kv_transposed):
    if kv_transposed:
        (x_ref, g_ref, w_ref, wkv_ref, wg_ref, cos_ref, sin_ref, cost_ref, sint_ref,
         qd_ref, kvd_ref, kvdb_ref, qn_ref, nsa_ref, win_ref, kvb_ref, gate_ref) = refs
    else:
        (x_ref, g_ref, w_ref, wkv_ref, wg_ref, cos_ref, sin_ref,
         qd_ref, kvd_ref, qn_ref, nsa_ref, win_ref, gate_ref) = refs
    tm = x_ref.shape[0]
    hb = (_rms(x_ref[...]) * g_ref[...]).astype(BF16)
    cos = cos_ref[...]
    sin = sin_ref[...]

    groups = [_dot(hb, w_ref[:, c0:c0 + DIFF_WIDTH]) for c0 in range(0, QKV_WIDTH, DIFF_WIDTH)]

    def proj(c0):
        return groups[c0 // DIFF_WIDTH][:, c0 % DIFF_WIDTH:c0 % DIFF_WIDTH + LANES]

    for j in range(N_DIFF_HEADS):
        qd_ref[:, j * LANES:(j + 1) * LANES] = (_rotary(proj(j * LANES), cos, sin, 1) * SCALE).astype(BF16)
        k = _rotary(proj(DIFF_WIDTH + j * LANES), cos, sin, 1)
        v = proj(2 * DIFF_WIDTH + j * LANES)
        kvd_ref[pl.ds(j, tm, stride=DIFF_SLABS), :] = k
        kvd_ref[pl.ds(N_DIFF_HEADS + j, tm, stride=DIFF_SLABS), :] = v
        if kv_transposed:
            kvdb_ref[:, j * LANES:(j + 1) * LANES] = k.astype(BF16)
            kvdb_ref[:, DIFF_WIDTH + j * LANES:DIFF_WIDTH + (j + 1) * LANES] = v.astype(BF16)
    for j in range(NSA_WIDTH // LANES):
        qn_ref[:, j * LANES:(j + 1) * LANES] = (_rotary(proj(3 * DIFF_WIDTH + j * LANES), cos, sin, 1)
                                                * SCALE).astype(BF16)
    gate_ref[...] = jax.nn.sigmoid(_dot(hb, wg_ref[...]))

    if kv_transposed:
        t_all = _dot_nt(wkv_ref[...], hb)
        cost = cost_ref[...]
        sint = sint_ref[...]
        for slot in range(KV_SLOTS):
            t = t_all[slot * LANES:(slot + 1) * LANES]
            if slot % 2 == 0:
                t = _rotary(t, cost, sint, 0)
            if slot < 4:
                nsa_ref[0, slot * LANES:(slot + 1) * LANES, :] = t
            else:
                win_ref[0, (slot - 4) * LANES:(slot - 3) * LANES, :] = t
            if slot >= 2:
                kvb_ref[0, (slot - 2) * LANES:(slot - 1) * LANES, :] = t.astype(BF16)
    else:
        t_all = _dot(hb, wkv_ref[...])
        for slot in range(KV_SLOTS):
            t = t_all[:, slot * LANES:(slot + 1) * LANES]
            if slot % 2 == 0:
                t = _rotary(t, cos, sin, 1)
            if slot < 4:
                nsa_ref[:, slot * LANES:(slot + 1) * LANES] = t
            else:
                win_ref[:, (slot - 4) * LANES:(slot - 3) * LANES] = t


def _inproj(x2d, g, w_qkv, w_kv, w_gate, tabs, tm, n_tab_blocks, batch, kv_transposed):
    rows, d = x2d.shape
    row_spec = lambda n: pl.BlockSpec((tm, n), lambda i: (i, 0))
    tab_spec = pl.BlockSpec((tm, LANES), lambda i: (i % n_tab_blocks, 0))
    in_specs = [row_spec(d), _const_spec((1, d)), _const_spec(w_qkv.shape), _const_spec(w_kv.shape),
                _const_spec(w_gate.shape), tab_spec, tab_spec]
    sds = jax.ShapeDtypeStruct
    interleaved = (sds((rows * DIFF_SLABS, LANES), F32), pl.BlockSpec((tm * DIFF_SLABS, LANES), lambda i: (i, 0)))
    if kv_transposed:
        seq = rows // batch
        nt = seq // tm
        tabt_spec = pl.BlockSpec((LANES, tm), lambda i: (0, i % nt))
        in_specs += [tabt_spec, tabt_spec]
        tspec = lambda n: pl.BlockSpec((1, n, tm), lambda i: (i // nt, 0, i % nt))
        outs = [
            (sds((rows, DIFF_WIDTH), BF16), row_spec(DIFF_WIDTH)),
            interleaved,
            (sds((rows, 2 * DIFF_WIDTH), BF16), row_spec(2 * DIFF_WIDTH)),
            (sds((rows, NSA_WIDTH), BF16), row_spec(NSA_WIDTH)),
            (sds((batch, 4 * LANES, seq), F32), tspec(4 * LANES)),
            (sds((batch, 2 * LANES, seq), F32), tspec(2 * LANES)),
            (sds((batch, 4 * LANES, seq), BF16), tspec(4 * LANES)),
            (sds((rows, LANES), F32), row_spec(LANES)),
        ]
    else:
        outs = [
            (sds((rows, DIFF_WIDTH), BF16), row_spec(DIFF_WIDTH)),
            interleaved,
            (sds((rows, NSA_WIDTH), BF16), row_spec(NSA_WIDTH)),
            (sds((rows, 4 * LANES), F32), row_spec(4 * LANES)),
            (sds((rows, 2 * LANES), F32), row_spec(2 * LANES)),
            (sds((rows, LANES), F32), row_spec(LANES)),
        ]
    return pl.pallas_call(
        functools.partial(_inproj_kernel, kv_transposed=kv_transposed),
        grid=(rows // tm,),
        in_specs=in_specs,
        out_specs=tuple(o[1] for o in outs),
        out_shape=tuple(o[0] for o in outs),
        compiler_params=_params("parallel"),
        name="inproj",
    )(x2d, g, w_qkv, w_kv, w_gate, *tabs)


def _lambda(lq1, lk1, lq2, lk2, lam_init):
    return (jnp.exp(jnp.sum(lq1[...] * lk1[...], axis=-1, keepdims=True))
            - jnp.exp(jnp.sum(lq2[...] * lk2[...], axis=-1, keepdims=True)) + lam_init)


def _split_components(q):
    lane = lax.broadcasted_iota(jnp.int32, q.shape, 1)
    low = lane < HEAD_DIM
    return jnp.concatenate([jnp.where(low, q, 0.0), jnp.where(low, 0.0, q)], axis=0).astype(BF16)


def _lane_tile(x, width):
    return jnp.concatenate([x] * (width // LANES), axis=1)


def _with_ones(v, axis):
    return jnp.concatenate([v, jnp.ones(v.shape, v.dtype)], axis=axis)


def _softmax_chunks(s_ref, p_ref, m_ref, a_ref, n_rows, adjust, unrolled):
    keys = s_ref.shape[1]

    def new_max(r0):
        rows = pl.ds(r0, SOFTMAX_ROWS)
        m_prev = m_ref[rows, :]
        m_new = jnp.maximum(m_prev, jnp.max(adjust(s_ref[rows, :], r0), axis=1, keepdims=True))
        if a_ref is not None:
            a_ref[rows, :] = jnp.exp(m_prev - m_new)
        m_ref[rows, :] = m_new

    def exponentiate(r0):
        rows = pl.ds(r0, SOFTMAX_ROWS)
        p_ref[rows, :] = jnp.exp(adjust(s_ref[rows, :], r0) - _lane_tile(m_ref[rows, :], keys)).astype(BF16)

    for one_pass in (new_max, exponentiate):
        if unrolled:
            for c in range(n_rows // SOFTMAX_ROWS):
                one_pass(c * SOFTMAX_ROWS)
        else:
            def body(c, carry, one_pass=one_pass):
                one_pass(pl.multiple_of(c * SOFTMAX_ROWS, SOFTMAX_ROWS))
                return carry
            lax.fori_loop(0, n_rows // SOFTMAX_ROWS, body, 0)


def _diff_prompt_kernel(qi_ref, ki_ref, q_ref, k_ref, v_ref, lq1, lk1, lq2, lk2, g_ref, o_ref,
                        m_ref, a_ref, acc_ref, s_ref, p_ref, *, tile, lam_init):
    t = pl.program_id(1)
    qi = qi_ref[t]
    ki = ki_ref[t]
    n_rows = 2 * tile

    @pl.when(ki == 0)
    def _init():
        m_ref[...] = jnp.full(m_ref.shape, NEG, F32)
        acc_ref[...] = jnp.zeros(acc_ref.shape, F32)

    def causal(s, r0):
        row = r0 % tile + lax.broadcasted_iota(jnp.int32, s.shape, 0)
        return jnp.where(lax.broadcasted_iota(jnp.int32, s.shape, 1) <= row, s, NEG)

    def update(diagonal):
        for h in range(N_DIFF_HEADS):
            buf = h % 2
            q2 = _split_components(q_ref[:, h * LANES:(h + 1) * LANES].astype(F32))
            s_ref[buf] = _dot_nt(q2, k_ref[:, h * LANES:(h + 1) * LANES])
            _softmax_chunks(s_ref.at[buf], p_ref.at[buf], m_ref.at[h], a_ref.at[buf], n_rows,
                            causal if diagonal else (lambda s, r0: s), unrolled=True)
            acc_ref[h] = (_lane_tile(a_ref[buf], 2 * LANES) * acc_ref[h]
                          + _dot(p_ref[buf], _with_ones(v_ref[:, h * LANES:(h + 1) * LANES], 1)))

    @pl.when(ki < qi)
    def _below_diagonal():
        update(False)

    @pl.when(ki == qi)
    def _finish():
        update(True)
        lam = _lambda(lq1, lk1, lq2, lk2, lam_init)
        for h in range(N_DIFF_HEADS):
            o = acc_ref[h, :, 0:LANES] / acc_ref[h, :, LANES:2 * LANES]
            a = o[0:tile] - lam * o[tile:2 * tile]
            o_ref[:, h * LANES:(h + 1) * LANES] = (_rms(a) * g_ref[...] * (1.0 - lam_init)).astype(BF16)


def _diff_prompt(qd, kvdb, lams, subln, b, s, tile, lam_init):
    nq = s // tile
    pairs = [(qi, ki) for qi in range(nq) for ki in range(qi + 1)]
    qi_arr = jnp.asarray([p[0] for p in pairs], jnp.int32)
    ki_arr = jnp.asarray([p[1] for p in pairs], jnp.int32)
    grid_spec = pltpu.PrefetchScalarGridSpec(
        num_scalar_prefetch=2,
        grid=(b, len(pairs)),
        in_specs=[
            pl.BlockSpec((tile, DIFF_WIDTH), lambda bi, t, qa, ka: (bi * nq + qa[t], 0)),
            pl.BlockSpec((tile, DIFF_WIDTH), lambda bi, t, qa, ka: (bi * nq + ka[t], 0)),
            pl.BlockSpec((tile, DIFF_WIDTH), lambda bi, t, qa, ka: (bi * nq + ka[t], 1)),
        ] + [_const_spec((1, HEAD_DIM))] * 4 + [_const_spec((1, 2 * HEAD_DIM))],
        out_specs=pl.BlockSpec((tile, DIFF_WIDTH), lambda bi, t, qa, ka: (bi * nq + qa[t], 0)),
        scratch_shapes=[pltpu.VMEM((N_DIFF_HEADS, 2 * tile, LANES), F32),
                        pltpu.VMEM((2, 2 * tile, LANES), F32),
                        pltpu.VMEM((N_DIFF_HEADS, 2 * tile, 2 * LANES), F32),
                        pltpu.VMEM((2, 2 * tile, tile), F32),
                        pltpu.VMEM((2, 2 * tile, tile), BF16)],
    )
    return pl.pallas_call(
        functools.partial(_diff_prompt_kernel, tile=tile, lam_init=lam_init),
        grid_spec=grid_spec,
        out_shape=jax.ShapeDtypeStruct((b * s, DIFF_WIDTH), BF16),
        compiler_params=_params("parallel", "arbitrary"),
        name="diff_prompt",
    )(qi_arr, ki_arr, qd, kvdb, kvdb, *lams, subln)


def _diff_sample_kernel(pt_ref, *refs, n_pages, group, **static):
    page_refs = refs[:group * n_pages]
    q_ref, new_ref, lq1, lk1, lq2, lk2, g_ref, o_ref, kv_ref = refs[group * n_pages:]
    bodies = [_diff_sample_one(u, page_refs[u * n_pages:(u + 1) * n_pages], q_ref, new_ref, lq1, lk1, lq2, lk2, g_ref,
                               o_ref, kv_ref, n_pages=n_pages, **static) for u in range(group)]
    for _ in itertools.zip_longest(*bodies):
        pass


def _diff_sample_one(u, page_refs, q_ref, new_ref, lq1, lk1, lq2, lk2, g_ref, o_ref, kv_ref, *, n_pages, page, past, sq,
                     lam_init):
    n_cols = 2 * (n_pages + 1) * page
    n_rows = N_DIFF_HEADS * 2 * sq
    col = lax.broadcasted_iota(jnp.int32, (n_rows, n_cols), 1)
    tpos = past + lax.broadcasted_iota(jnp.int32, (n_rows, n_cols), 0) % sq
    visible = (col % 2 == 0) & (col // 2 <= tpos)
    lam = _lambda(lq1, lk1, lq2, lk2, lam_init)
    new0 = u * sq * DIFF_SLABS

    scores = []
    for h in range(N_DIFF_HEADS):
        kv = kv_ref.at[u * N_DIFF_HEADS + h]
        for i, r in enumerate(page_refs):
            kv[i * 2 * page:(i + 1) * 2 * page, :] = r[pl.ds(h, 2 * page, stride=N_DIFF_HEADS), :].astype(BF16)
        kv[n_pages * 2 * page:n_cols, :] = _pad_rows(new_ref[pl.ds(new0 + h, 2 * sq, stride=N_DIFF_HEADS), :],
                                                     2 * page).astype(BF16)
        q2 = _split_components(q_ref[u, :, h * LANES:(h + 1) * LANES].astype(F32))
        scores.append(_dot_nt(q2, kv[...]))
        yield
    s = jnp.where(visible, jnp.concatenate(scores, axis=0), NEG)
    m = jnp.max(s, axis=1, keepdims=True)
    p = jnp.where(visible, jnp.exp(s - m), 0.0)
    l = jnp.maximum(jnp.sum(p, axis=1, keepdims=True), 1e-30)
    p = pltpu.roll(p, 1, 1).astype(BF16)
    yield
    for h in range(N_DIFF_HEADS):
        rows = slice(h * 2 * sq, (h + 1) * 2 * sq)
        o = _dot(p[rows], kv_ref[u * N_DIFF_HEADS + h]) / l[rows]
        a = o[0:sq] - lam * o[sq:2 * sq]
        o_ref[u, :, h * LANES:(h + 1) * LANES] = (_rms(a) * g_ref[...] * (1.0 - lam_init)).astype(BF16)
        yield


def _diff_sample(cache_rows, page_table, qd3, kvd_rows, lams, subln, page, lam_init):
    db, sq, _ = qd3.shape
    n_pages = page_table.shape[1]
    past = n_pages * page
    group = SAMPLE_GROUP if db % SAMPLE_GROUP == 0 else 1
    page_specs = [pl.BlockSpec((page * DIFF_SLABS, LANES),
                               functools.partial(lambda bi, pt, u, p: (pt[bi * group + u, p], 0), u=u, p=p))
                  for u in range(group) for p in range(n_pages)]
    grid_spec = pltpu.PrefetchScalarGridSpec(
        num_scalar_prefetch=1,
        grid=(db // group,),
        in_specs=page_specs + [
            pl.BlockSpec((group, sq, DIFF_WIDTH), lambda bi, pt: (bi, 0, 0)),
            pl.BlockSpec((group * sq * DIFF_SLABS, LANES), lambda bi, pt: (bi, 0)),
        ] + [_const_spec((1, HEAD_DIM))] * 4 + [_const_spec((1, 2 * HEAD_DIM))],
        out_specs=pl.BlockSpec((group, sq, DIFF_WIDTH), lambda bi, pt: (bi, 0, 0)),
        scratch_shapes=[pltpu.VMEM((group * N_DIFF_HEADS, 2 * (n_pages + 1) * page, LANES), BF16)],
    )
    return pl.pallas_call(
        functools.partial(_diff_sample_kernel, n_pages=n_pages, group=group, page=page, past=past, sq=sq,
                          lam_init=lam_init),
        grid_spec=grid_spec,
        out_shape=jax.ShapeDtypeStruct((db, sq, DIFF_WIDTH), BF16),
        compiler_params=_params("parallel"),
        name="diff_sample",
    )(page_table, *([cache_rows] * (group * n_pages)), qd3, kvd_rows, *lams, subln)


def _chunk_pitch(n_chunks):
    assert n_chunks % (2 * SUBLANES) == 0
    return n_chunks + SUBLANES


def _store_token_rows(rows_ref, slot, first_chunk, rows, n_chunks):
    for c in range(rows.shape[0] // CMP_STRIDE):
        rows_ref[slot, pl.ds(first_chunk + c, CMP_STRIDE, stride=_chunk_pitch(n_chunks)), :] = (
            rows[c * CMP_STRIDE:(c + 1) * CMP_STRIDE])


def _compress(rows_ref, slot, n_chunks, pos_a, pos_b, w_a, w_b, w2):
    pitch = _chunk_pitch(n_chunks)
    xs = jnp.concatenate([rows_ref[slot, j * pitch:j * pitch + n_chunks, :] for j in range(CMP_STRIDE)],
                         axis=1)
    a = _dot((xs + pos_a).astype(BF16), w_a)
    b = _dot((xs + pos_b).astype(BF16), w_b)
    hidden = jax.nn.gelu(a + pltpu.roll(b, n_chunks - 1, 0))
    out = _dot(hidden.astype(BF16), w2)
    rid = lax.broadcasted_iota(jnp.int32, out.shape, 0)
    return jnp.where(rid < n_chunks - 1, out, 0.0)


def _compress_prompt_kernel(x_ref, pa_ref, pb_ref, wa_ref, wb_ref, w2_ref, o_ref, rows_ref):
    seq = x_ref.shape[2]
    for slot in range(2):
        for c in range(seq // LANES):
            _store_token_rows(rows_ref, slot, c * (LANES // CMP_STRIDE),
                              x_ref[0, slot * LANES:(slot + 1) * LANES, c * LANES:(c + 1) * LANES].T,
                              seq // CMP_STRIDE)
        o_ref[0, slot] = _compress(rows_ref, slot, seq // CMP_STRIDE, pa_ref[slot], pb_ref[slot], wa_ref[slot],
                                   wb_ref[slot], w2_ref[slot]).astype(BF16)


def _compress_prompt(nsa_t, cw):
    b, _, seq = nsa_t.shape
    nch = seq // CMP_STRIDE
    return pl.pallas_call(
        _compress_prompt_kernel,
        grid=(b,),
        in_specs=[pl.BlockSpec((1, 2 * LANES, seq), lambda i: (i, 0, 0))] + [_const_spec(a.shape) for a in cw],
        out_specs=pl.BlockSpec((1, 2, nch, LANES), lambda i: (i, 0, 0, 0)),
        out_shape=jax.ShapeDtypeStruct((b, 2, nch, LANES), BF16),
        scratch_shapes=[pltpu.VMEM((2, CMP_STRIDE * _chunk_pitch(nch), LANES), F32)],
        compiler_params=_params("parallel"),
        name="compress_prompt",
    )(nsa_t, *cw)


def _cmp_probs(qs, kc, tpos):
    s = _dot_nt(qs, kc)
    cend = lax.broadcasted_iota(jnp.int32, s.shape, 1) * CMP_STRIDE + (CMP_BLOCK - 1)
    vis = cend <= tpos
    s = jnp.where(vis, s, NEG)
    m = jnp.max(s, axis=1, keepdims=True)
    p = jnp.where(vis, jnp.exp(s - m), 0.0)
    return (p / jnp.maximum(jnp.sum(p, axis=1, keepdims=True), 1e-30)).astype(BF16)


def _block_scores(imp, blk, cur):
    forced = (blk == 0) | (blk == cur) | (blk == cur - 1)
    return jnp.where(blk > cur, NEG, jnp.where(forced, BIG, imp))


def _select_blocks(imp, tpos2):
    blk = lax.broadcasted_iota(jnp.int32, imp.shape, 1)
    score = _block_scores(imp, blk, tpos2 // SEL_BLOCK)
    rank = jnp.zeros(imp.shape, F32)
    for j in range(SEL_LANES):
        sj = score[:, j:j + 1]
        tie = jnp.where(blk > j, 1.0, 0.0)
        rank = rank + jnp.where(sj > score, 1.0, 0.0) + jnp.where(sj == score, tie, 0.0)
    return jnp.where(rank < SEL_TOP_N, 1.0, 0.0).astype(BF16)


def _select_blocks_t(imp_t, tpos_t):
    n = imp_t.shape[1]
    blk = lax.broadcasted_iota(jnp.int32, imp_t.shape, 0)
    score = _block_scores(imp_t, blk, tpos_t // SEL_BLOCK)
    n_groups = SEL_LANES // SUBLANES
    groups = [score[g * SUBLANES:(g + 1) * SUBLANES] for g in range(n_groups)]
    ranks = [jnp.zeros((SUBLANES, n), F32) for _ in range(n_groups)]
    rid = lax.broadcasted_iota(jnp.int32, (SUBLANES, n), 0)
    for j in range(SEL_LANES):
        sj = score[j:j + 1]
        for g in range(n_groups):
            if g * SUBLANES > j:
                beats = jnp.where(sj >= groups[g], 1.0, 0.0)
            elif (g + 1) * SUBLANES - 1 < j:
                beats = jnp.where(sj > groups[g], 1.0, 0.0)
            else:
                beats = jnp.where(rid + g * SUBLANES > j, jnp.where(sj >= groups[g], 1.0, 0.0),
                                  jnp.where(sj > groups[g], 1.0, 0.0))
            ranks[g] = ranks[g] + beats
    sel_t = jnp.concatenate([jnp.where(r < SEL_TOP_N, 1.0, 0.0) for r in ranks]
                            + [jnp.zeros((LANES - SEL_LANES, n), F32)], axis=0)
    return sel_t.T


def _stack_queries(q, n_q):
    lane = lax.broadcasted_iota(jnp.int32, (n_q, LANES), 1)
    low = lane < HEAD_DIM
    groups = [q[:, g * LANES:(g + 1) * LANES].astype(F32) for g in range(NSA_GROUP)]
    return jnp.concatenate([jnp.where(low, t, 0.0) for t in groups] + [jnp.where(low, 0.0, t) for t in groups],
                           axis=0).astype(BF16)


def _combine(o_c, o_s, o_w, gates, n_q, store):
    lane = lax.broadcasted_iota(jnp.int32, (n_q, LANES), 1)
    low = lane < HEAD_DIM
    for g in range(NSA_GROUP):
        def pick(o):
            return jnp.where(low, o[g * n_q:(g + 1) * n_q], o[(NSA_GROUP + g) * n_q:(NSA_GROUP + g + 1) * n_q])

        def gate(j):
            c = j * N_NSA_HEADS + 2 * g
            return jnp.where(low, gates[:, c:c + 1], gates[:, c + 1:c + 2])

        store(g, gate(0) * pick(o_c) + gate(1) * pick(o_s) + gate(2) * pick(o_w))


def _per_kv_head(x, n_q):
    return x.reshape(N_NSA_KV_HEADS, NSA_GROUP, n_q, x.shape[-1])


def _in_turn(bodies):
    for _ in itertools.zip_longest(*bodies):
        pass


def _nsa_prompt_kernel(qn_ref, gate_ref, cmp_ref, kv_ref, ovl_ref, exp_ref, o_ref,
                       m_ref, a_ref, acc_ref, s_ref, p_ref, sw_ref, pw_ref, sc_ref, pc_ref, qse_ref, *, tq, tk, group):
    m_rows = N_NSA_HEADS * tq
    step_start = pl.program_id(1) * (group * tq)
    o_cmp = [None] * group

    def head(u):
        q_start = step_start + u * tq
        qse, m, sc, pc = qse_ref.at[u], m_ref.at[u], sc_ref.at[u], pc_ref.at[u]
        qse[:, 0:LANES] = _stack_queries(qn_ref[u * tq:(u + 1) * tq, :], tq)
        tpos2_t = q_start + lax.broadcasted_iota(jnp.int32, (1, N_NSA_KV_HEADS * tq), 1) % tq
        m[...] = jnp.full(m.shape, NEG, F32)
        acc_ref[u] = jnp.zeros(acc_ref.shape[1:], F32)

        m[2] = jnp.full(m.shape[1:], 0.5 * NEG, F32)
        sc[...] = _dot_nt(qse[:, 0:LANES], cmp_ref[0, 0])
        yield

        def block_ended(s, r0):
            qpos = q_start + r0 % tq + lax.broadcasted_iota(jnp.int32, s.shape, 0)
            cend = lax.broadcasted_iota(jnp.int32, s.shape, 1) * CMP_STRIDE + (CMP_BLOCK - 1)
            return jnp.where(cend <= qpos, s, NEG)

        _softmax_chunks(sc, pc, m.at[2], None, m_rows, block_ended, unrolled=True)
        yield
        o_c = _dot(pc[...], _with_ones(cmp_ref[0, 1], 1))
        o_cmp[u] = o_c[:, 0:LANES] / jnp.maximum(o_c[:, LANES:2 * LANES], 1e-30)
        imp_t = _dot_nt(ovl_ref[...], pc[...])
        imp_t = imp_t[0:SEL_LANES] / jnp.maximum(imp_t[SEL_LANES:SEL_LANES + 1], 1e-30)
        imp_t = jnp.concatenate(
            [(imp_t[:, (4 * h) * tq:(4 * h + 1) * tq] + imp_t[:, (4 * h + 1) * tq:(4 * h + 2) * tq])
             + (imp_t[:, (4 * h + 2) * tq:(4 * h + 3) * tq] + imp_t[:, (4 * h + 3) * tq:(4 * h + 4) * tq])
             for h in range(N_NSA_KV_HEADS)], axis=1)
        yield
        sel = _select_blocks_t(imp_t, tpos2_t)
        yield
        block_bias = jnp.where(sel > 0.5, 0.0, NEG).astype(BF16)
        for r in range(N_NSA_HEADS):
            h = r // NSA_GROUP
            qse[r * tq:(r + 1) * tq, LANES:2 * LANES] = block_bias[h * tq:(h + 1) * tq]

    def slc_tile(u, k0, diagonal):
        q_start = step_start + u * tq
        cols = pl.ds(k0, tk)
        s_ref[u] = _dot(qse_ref[u], jnp.concatenate([kv_ref[0, 0:LANES, cols], exp_ref[:, cols]], axis=0))
        yield

        def causal(s, r0):
            qpos = q_start + r0 % tq + lax.broadcasted_iota(jnp.int32, s.shape, 0)
            return jnp.where(k0 + lax.broadcasted_iota(jnp.int32, s.shape, 1) <= qpos, s, NEG)

        _softmax_chunks(s_ref.at[u], p_ref.at[u], m_ref.at[u].at[0], a_ref.at[u], m_rows,
                        causal if diagonal else (lambda s, r0: s), unrolled=True)
        yield
        acc_ref[u] = (_lane_tile(a_ref[u], 2 * LANES) * acc_ref[u]
                      + _dot_nt(p_ref[u], _with_ones(kv_ref[0, LANES:2 * LANES, cols], 0)))

    def tail(u):
        q_start = step_start + u * tq
        o_s = acc_ref[u, :, 0:LANES] / acc_ref[u, :, LANES:2 * LANES]
        span = WINDOW + tq
        w0 = pl.multiple_of(jnp.maximum(q_start - WINDOW, 0), tq)
        sw_ref[u] = _dot(qse_ref[u, :, 0:LANES], kv_ref[0, 2 * LANES:3 * LANES, pl.ds(w0, span)])
        yield

        def windowed(s, r0):
            qpos = q_start + r0 % tq + lax.broadcasted_iota(jnp.int32, s.shape, 0)
            dist = qpos - (w0 + lax.broadcasted_iota(jnp.int32, s.shape, 1))
            return jnp.where(lax.bitcast_convert_type(dist, jnp.uint32) < WINDOW, s, NEG)

        _softmax_chunks(sw_ref.at[u], pw_ref.at[u], m_ref.at[u].at[1], None, m_rows, windowed, unrolled=True)
        yield
        o_w = _dot_nt(pw_ref[u], _with_ones(kv_ref[0, 3 * LANES:4 * LANES, pl.ds(w0, span)], 0))
        o_w = o_w[:, 0:LANES] / o_w[:, LANES:2 * LANES]
        yield

        def store(g, val):
            o_ref[u * tq:(u + 1) * tq, g * LANES:(g + 1) * LANES] = val.astype(BF16)

        _combine(o_cmp[u], o_s, o_w, gate_ref[u * tq:(u + 1) * tq, :], tq, store)

    tiles = range(group)
    _in_turn([head(u) for u in tiles])

    def below_diagonal(kt, carry):
        _in_turn([slc_tile(u, pl.multiple_of(kt * tk, tk), False) for u in tiles])
        return carry

    last_tile = step_start // tk
    lax.fori_loop(0, last_tile, below_diagonal, 0)
    _in_turn([slc_tile(u, pl.multiple_of(last_tile * tk, tk), True) for u in tiles])
    _in_turn([tail(u) for u in tiles])


def _nsa_prompt(qn, gates, cmp_kv, kv_t, ovl_t, expand, b, s, tq, tk, group):
    assert tk % (group * tq) == 0 and s % (group * tq) == 0
    rows = group * tq
    nq = s // rows
    nch = cmp_kv.shape[2]
    m_rows = N_NSA_HEADS * tq
    tile = lambda *shape: pltpu.VMEM((group,) + shape, F32)
    tile_bf16 = lambda *shape: pltpu.VMEM((group,) + shape, BF16)
    return pl.pallas_call(
        functools.partial(_nsa_prompt_kernel, tq=tq, tk=tk, group=group),
        grid=(b, nq),
        in_specs=[
            pl.BlockSpec((rows, NSA_WIDTH), lambda bi, qi: (bi * nq + qi, 0)),
            pl.BlockSpec((rows, LANES), lambda bi, qi: (bi * nq + qi, 0)),
            pl.BlockSpec((1, 2, nch, LANES), lambda bi, qi: (bi, 0, 0, 0)),
            pl.BlockSpec((1, 4 * LANES, s), lambda bi, qi: (bi, 0, 0)),
            _const_spec(ovl_t.shape), _const_spec(expand.shape),
        ],
        out_specs=pl.BlockSpec((rows, NSA_WIDTH), lambda bi, qi: (bi * nq + qi, 0)),
        out_shape=jax.ShapeDtypeStruct((b * s, NSA_WIDTH), BF16),
        scratch_shapes=[tile(3, m_rows, LANES),
                        tile(m_rows, LANES),
                        tile(m_rows, 2 * LANES),
                        tile(m_rows, tk), tile_bf16(m_rows, tk),
                        tile(m_rows, WINDOW + tq), tile_bf16(m_rows, WINDOW + tq),
                        tile(m_rows, nch), tile_bf16(m_rows, nch),
                        tile_bf16(m_rows, 2 * LANES)],
        compiler_params=_params("parallel", "arbitrary"),
        name="nsa_prompt",
    )(qn, gates, cmp_kv, kv_t, ovl_t, expand)


def _nsa_sample_kernel(pt_ref, *refs, n_pages, group, **static):
    page_refs = refs[:group * n_pages]
    per_batch = refs[group * n_pages:group * n_pages + 5]
    consts = refs[group * n_pages + 5:-3]
    o_ref, wout_ref, rows_ref = refs[-3:]
    bodies = []
    for u in range(group):
        one = pl.ds(u, 1)
        bodies.append(_nsa_sample_one(page_refs[u * n_pages:(u + 1) * n_pages], *[r.at[one] for r in per_batch],
                                      *consts, o_ref.at[one], wout_ref.at[one], rows_ref.at[pl.ds(2 * u, 2)],
                                      n_pages=n_pages, **static))
    for _ in itertools.zip_longest(*bodies):
        pass


def _nsa_sample_one(page_refs, win_ref, nsanew_ref, winnew_ref, qn_ref, gate_ref, pa_ref, pb_ref, wa_ref, wb_ref,
                    w2_ref, ovl_ref, exp_ref, o_ref, wout_ref, rows_ref, *, n_pages, page, past, sq):
    m_rows = N_NSA_HEADS * sq
    qs = _stack_queries(qn_ref[0], sq)
    tpos = past + lax.broadcasted_iota(jnp.int32, (m_rows, 1), 0) % sq
    tpos2 = past + lax.broadcasted_iota(jnp.int32, (N_NSA_KV_HEADS * sq, 1), 0) % sq

    def page_t(i, slot):
        return page_refs[i][0, slot].reshape(N_NSA_KV_HEADS * HEAD_DIM, page)

    cmp_kv = []
    for slot in range(2):
        for i in range(n_pages):
            _store_token_rows(rows_ref, slot, i * (page // CMP_STRIDE), page_t(i, slot).T, past // CMP_STRIDE)
        yield
        cmp_kv.append(_compress(rows_ref, slot, past // CMP_STRIDE, pa_ref[slot], pb_ref[slot], wa_ref[slot],
                                wb_ref[slot], w2_ref[slot]).astype(BF16))
        yield
    p_c = _cmp_probs(qs, cmp_kv[0], tpos)
    o_c = _dot(p_c, cmp_kv[1])
    imp = _per_kv_head(_dot(p_c, ovl_ref[...]), sq)
    imp = ((imp[:, 0] + imp[:, 1]) + (imp[:, 2] + imp[:, 3])).reshape(N_NSA_KV_HEADS * sq, SEL_LANES)
    yield
    sel = _select_blocks(imp, tpos2)
    yield

    nsanew = _pad_rows(nsanew_ref[0], page)
    n_keys = (n_pages + 1) * page
    s = jnp.concatenate([_dot(qs, page_t(i, 2).astype(BF16)) for i in range(n_pages)]
                        + [_dot_nt(qs, nsanew[:, 2 * LANES:3 * LANES].astype(BF16))], axis=1)
    yield
    picked = _dot(sel, exp_ref[...])
    kpos = lax.broadcasted_iota(jnp.int32, (1, n_keys), 1)
    bias = jnp.where(jnp.where(kpos <= tpos2, picked, 0.0) > 0.5, 0.0, NEG)
    s = (_per_kv_head(s, sq) + bias.reshape(N_NSA_KV_HEADS, 1, sq, n_keys)).reshape(m_rows, n_keys)
    m = jnp.max(s, axis=1, keepdims=True)
    p = jnp.exp(s - m)
    l = jnp.sum(p, axis=1, keepdims=True)
    pb = p.astype(BF16)
    acc = _dot(pb[:, past:n_keys], nsanew[:, 3 * LANES:4 * LANES].astype(BF16))
    for i in range(n_pages):
        acc = acc + _dot_nt(pb[:, i * page:(i + 1) * page], page_t(i, 3).astype(BF16))
    o_s = acc / l
    yield

    winnew = _pad_rows(winnew_ref[0], page)
    s = jnp.concatenate([_dot(qs, win_ref[0, 0:LANES, :].astype(BF16)),
                         _dot_nt(qs, winnew[:, 0:LANES].astype(BF16))], axis=1)
    col = lax.broadcasted_iota(jnp.int32, (1, WINDOW + page), 1)
    dist = tpos - (past - WINDOW + col)
    vis = (dist >= 0) & (dist < WINDOW)
    s = jnp.where(vis, s, NEG)
    m = jnp.max(s, axis=1, keepdims=True)
    p = jnp.where(vis, jnp.exp(s - m), 0.0)
    pb = p.astype(BF16)
    o_w = ((_dot_nt(pb[:, 0:WINDOW], win_ref[0, LANES:2 * LANES, :].astype(BF16))
            + _dot(pb[:, WINDOW:WINDOW + page], winnew[:, LANES:2 * LANES].astype(BF16)))
           / jnp.maximum(jnp.sum(p, axis=1, keepdims=True), 1e-30))

    def store(g, val):
        o_ref[0, :, g * LANES:(g + 1) * LANES] = val.astype(BF16)

    yield
    _combine(o_c, o_s, o_w, gate_ref[0], sq, store)

    extended = jnp.concatenate([win_ref[0], winnew.T], axis=1)
    wout_ref[0] = extended[:, sq:sq + WINDOW]


def _nsa_sample(cache_t, page_table, win_t, nsanew3, winnew3, qn3, gates3, cw, ovl, expand):
    db, sq, _ = qn3.shape
    n_pages = page_table.shape[1]
    page = cache_t.shape[-1]
    past = n_pages * page
    group = SAMPLE_GROUP if db % SAMPLE_GROUP == 0 else 1
    page_specs = [pl.BlockSpec((1,) + cache_t.shape[1:],
                               functools.partial(lambda bi, pt, u, p: (pt[bi * group + u, p], 0, 0, 0, 0), u=u, p=p))
                  for u in range(group) for p in range(n_pages)]
    per_b = lambda shape: pl.BlockSpec((group,) + shape, lambda bi, pt: (bi, 0, 0))
    grid_spec = pltpu.PrefetchScalarGridSpec(
        num_scalar_prefetch=1,
        grid=(db // group,),
        in_specs=page_specs + [
            per_b((2 * LANES, WINDOW)), per_b((sq, 4 * LANES)), per_b((sq, 2 * LANES)), per_b((sq, NSA_WIDTH)),
            per_b((sq, LANES)),
        ] + [_const_spec(a.shape) for a in cw] + [_const_spec(ovl.shape), _const_spec(expand.shape)],
        out_specs=(per_b((sq, NSA_WIDTH)), per_b((2 * LANES, WINDOW))),
        scratch_shapes=[pltpu.VMEM((2 * group, CMP_STRIDE * _chunk_pitch(past // CMP_STRIDE), LANES), F32)],
    )
    return pl.pallas_call(
        functools.partial(_nsa_sample_kernel, n_pages=n_pages, group=group, page=page, past=past, sq=sq),
        grid_spec=grid_spec,
        out_shape=(jax.ShapeDtypeStruct((db, sq, NSA_WIDTH), BF16),
                   jax.ShapeDtypeStruct((db, 2 * LANES, WINDOW), F32)),
        compiler_params=_params("parallel"),
        name="nsa_sample",
    )(page_table, *([cache_t] * (group * n_pages)), win_t, nsanew3, winnew3, qn3, gates3, *cw, ovl, expand)


def _mlp_kernel(x_ref, od_ref, on_ref, wo_ref, wu_ref, wd_ref, g1_ref, g2_ref, g3_ref, y_ref, *, ff_chunk):
    mix = _dot(od_ref[...], wo_ref[0:DIFF_WIDTH, :]) + _dot(on_ref[...], wo_ref[DIFF_WIDTH:DIFF_WIDTH + NSA_WIDTH, :])
    x1 = x_ref[...] + _rms(mix) * g1_ref[...]
    hm = (_rms(x1) * g2_ref[...]).astype(BF16)
    d_ff = wu_ref.shape[1]
    ff = jnp.zeros(x1.shape, F32)
    for c in range(d_ff // ff_chunk):
        u = jnp.maximum(_dot(hm, wu_ref[:, c * ff_chunk:(c + 1) * ff_chunk]), 0.0)
        ff = ff + _dot((u * u).astype(BF16), wd_ref[c * ff_chunk:(c + 1) * ff_chunk, :])
    y_ref[...] = x1 + _rms(ff) * g3_ref[...]


def _mlp(x2d, od, on, w_out, w_up, w_down, g1, g2, g3, tm):
    rows, d = x2d.shape
    row_spec = lambda n: pl.BlockSpec((tm, n), lambda i: (i, 0))
    resident = lambda a: pl.BlockSpec(a.shape, lambda i: (0, 0), pipeline_mode=pl.Buffered(1))
    return pl.pallas_call(
        functools.partial(_mlp_kernel, ff_chunk=1024),
        grid=(rows // tm,),
        in_specs=[row_spec(d), row_spec(DIFF_WIDTH), row_spec(NSA_WIDTH), resident(w_out), resident(w_up),
                  resident(w_down), _const_spec((1, d)), _const_spec((1, d)), _const_spec((1, d))],
        out_specs=row_spec(d),
        out_shape=jax.ShapeDtypeStruct((rows, d), F32),
        compiler_params=_params("parallel"),
        name="mlp",
    )(x2d, od, on, w_out, w_up, w_down, g1, g2, g3)


def _rope_tables(pos, reps):
    half = HEAD_DIM // 2
    inv = ROPE_THETA ** (-jnp.arange(half, dtype=F32) / half)
    ang = pos.astype(F32)[:, None] * inv[None, :]
    cos, sin = jnp.cos(ang), jnp.sin(ang)
    cos_t = jnp.tile(cos, (reps, LANES // half))
    sin_t = jnp.tile(jnp.concatenate([-sin, sin], axis=1), (reps, LANES // HEAD_DIM))
    return cos_t, sin_t


def _compress_weights(cmp_pos, cmp_w1, cmp_w2):
    eye = jnp.eye(N_NSA_KV_HEADS, dtype=F32)
    w1 = cmp_w1.reshape(2, CMP_BLOCK, HEAD_DIM, CMP_HIDDEN)

    def expand_w1(w):
        t = jnp.einsum('sldf,hg->slhdgf', w, eye)
        return t.reshape(2, CMP_STRIDE * LANES, N_NSA_KV_HEADS * CMP_HIDDEN).astype(BF16)

    def expand_pos(p):
        return jnp.tile(p[:, :, None, :], (1, 1, N_NSA_KV_HEADS, 1)).reshape(2, 1, CMP_STRIDE * LANES)

    w2 = jnp.einsum('sfd,hg->shfgd', cmp_w2, eye).reshape(2, N_NSA_KV_HEADS * CMP_HIDDEN, LANES).astype(BF16)
    return (expand_pos(cmp_pos[:, :CMP_STRIDE]), expand_pos(cmp_pos[:, CMP_STRIDE:]),
            expand_w1(w1[:, :CMP_STRIDE]), expand_w1(w1[:, CMP_STRIDE:]), w2)


def _selection_constants(n_chunks, n_keys, expand_rows):
    n = np.arange(n_chunks)
    m = np.arange(SEL_LANES)
    cs, ss = n * CMP_STRIDE, m * SEL_BLOCK
    ovl = (cs[:, None] < ss[None, :] + SEL_BLOCK) & (cs[:, None] + CMP_BLOCK > ss[None, :]) & (n[:, None] < n_chunks - 1)
    expand = (np.arange(n_keys)[None, :] // SEL_BLOCK) == np.arange(expand_rows)[:, None]
    return ovl, jnp.asarray(expand, BF16)


def _row_tile(rows, want):
    t = min(rows, want)
    assert rows % t == 0
    return t


def kernel(x_prompt, x_sample, cache_diff_kv, cache_nsa_kv, state_nsa_win_kv, page_table, w_in, w_out, w_up, w_down,
           g_pre_mix, g_post_mix, g_pre_mlp, g_post_mlp, lam_q1, lam_k1, lam_q2, lam_k2, diff_subln, cmp_pos,
           cmp_w1, cmp_w2):
    depth = w_in.shape[0]
    assert depth == 1, "one layer: the sample group's paged caches are read in place"
    b, s, d = x_prompt.shape
    db, sq, _ = x_sample.shape
    n_phys, page = cache_diff_kv.shape[1:3]
    n_pages = page_table.shape[1]
    past = n_pages * page
    assert s % (4 * LANES) == 0 and s >= WINDOW + LANES and s <= SEL_LANES * SEL_BLOCK
    assert past >= WINDOW and sq < CMP_STRIDE and sq <= page and past + sq <= SEL_LANES * SEL_BLOCK
    assert state_nsa_win_kv.shape[2] == WINDOW and page == LANES
    layer = 0
    lam_init = 0.8 - 0.6 * math.exp(-0.3 * layer)

    wl = w_in[layer]
    qn0 = 3 * DIFF_WIDTH
    qn_cols = np.concatenate([qn0 + h * HEAD_DIM + np.arange(HEAD_DIM) for h in _HEAD_PERM])
    w_qkv = wl[:, np.concatenate([np.arange(qn0), qn_cols])].astype(BF16)
    w_kv = wl[:, QKV_WIDTH:QKV_WIDTH + KV_SLOTS * LANES].astype(BF16)
    gate0 = QKV_WIDTH + KV_SLOTS * LANES
    gate_cols = np.asarray([gate0 + h * 3 + j for j in range(3) for h in _HEAD_PERM])
    w_gate = jnp.pad(wl[:, gate_cols], ((0, 0), (0, LANES - N_GATES))).astype(BF16)
    out_rows = np.concatenate([np.arange(DIFF_WIDTH)] + [DIFF_WIDTH + h * HEAD_DIM + np.arange(HEAD_DIM) for h in _HEAD_PERM])
    wo = w_out[layer][out_rows].astype(BF16)
    wu = w_up[layer].astype(BF16)
    wd = w_down[layer].astype(BF16)
    vec = lambda a: a[layer].reshape(1, -1)
    lams = (vec(lam_q1), vec(lam_k1), vec(lam_q2), vec(lam_k2))
    cw = _compress_weights(cmp_pos[layer], cmp_w1[layer], cmp_w2[layer])

    rows_p = b * s
    tm_p = _row_tile(s, PROJ_ROWS)
    cos_p, sin_p = _rope_tables(jnp.arange(s), 1)
    xp = x_prompt.reshape(rows_p, d)
    qd, kvd, kvdb, qn, nsa_t, win_t, kvb_t, gates = _inproj(
        xp, vec(g_pre_mix), w_qkv, w_kv.T, w_gate, (cos_p, sin_p, cos_p.T, sin_p.T), tm_p, s // tm_p, b, True)
    od = _diff_prompt(qd, kvdb, lams, vec(diff_subln), b, s, _row_tile(s, DIFF_TILE), lam_init)
    cmp_kv = _compress_prompt(nsa_t, cw)
    ovl_p, exp_p = _selection_constants(s // CMP_STRIDE, s, LANES)
    ovl_rows = np.concatenate([ovl_p.T, np.ones((SUBLANES, ovl_p.shape[0])),
                               np.zeros((LANES - SEL_LANES - SUBLANES, ovl_p.shape[0]))], axis=0)
    on = _nsa_prompt(qn, gates, cmp_kv, kvb_t, jnp.asarray(ovl_rows, BF16), exp_p, b, s, NSA_QUERIES, NSA_KEYS,
                     NSA_TILES)
    yp = _mlp(xp, od, on, wo, wu, wd, vec(g_post_mix), vec(g_pre_mlp), vec(g_post_mlp), tm_p).reshape(b, s, d)
    p_diff = kvd.reshape(1, b, s, 2, N_DIFF_HEADS, 2 * HEAD_DIM)
    token_minor = lambda a, slots: jnp.transpose(
        a.reshape(a.shape[0], slots, N_NSA_KV_HEADS, HEAD_DIM, a.shape[-1]), (0, 4, 1, 2, 3))[None]
    p_nsa = token_minor(nsa_t, 4)
    p_win = token_minor(win_t[:, :, s - WINDOW:], 2)

    rows_s = db * sq
    tm_s = _row_tile(rows_s, PROJ_ROWS)
    assert tm_s % sq == 0
    cos_s, sin_s = _rope_tables(past + jnp.arange(sq), tm_s // sq)
    xs = x_sample.reshape(rows_s, d)
    qd, kvd, qn, nsa4, win2, gates = _inproj(xs, vec(g_pre_mix), w_qkv, w_kv, w_gate, (cos_s, sin_s), tm_s, 1, db,
                                             False)
    r3 = lambda a: a.reshape(db, sq, a.shape[1])
    diff_rows = cache_diff_kv[layer].reshape(n_phys * page * DIFF_SLABS, LANES)
    od = _diff_sample(diff_rows, page_table, r3(qd), kvd, lams, vec(diff_subln), page, lam_init)
    nsa_cache_t = jnp.transpose(cache_nsa_kv[layer], (0, 2, 3, 4, 1))
    win_state_t = jnp.transpose(state_nsa_win_kv[layer], (0, 2, 3, 4, 1)).reshape(db, 2 * LANES, WINDOW)
    ovl_s, exp_s = _selection_constants(past // CMP_STRIDE, past + page, SEL_LANES)
    on, win_new_t = _nsa_sample(nsa_cache_t, page_table, win_state_t, r3(nsa4), r3(win2), r3(qn), r3(gates), cw,
                                jnp.asarray(ovl_s, BF16), exp_s)
    ys = _mlp(xs, od.reshape(rows_s, DIFF_WIDTH), on.reshape(rows_s, NSA_WIDTH), wo, wu, wd, vec(g_post_mix),
              vec(g_pre_mlp), vec(g_post_mlp), tm_s).reshape(db, sq, d)
    s_diff = kvd.reshape(1, db, sq, 2, N_DIFF_HEADS, 2 * HEAD_DIM)
    s_nsa = nsa4.reshape(1, db, sq, 4, N_NSA_KV_HEADS, HEAD_DIM)
    s_win = token_minor(win_new_t, 2)

    return yp, ys, p_diff, p_nsa, p_win, s_diff, s_nsa, s_win
```

```python
import functools
import itertools
import math

import numpy as np
import jax
import jax.numpy as jnp
from jax import lax
from jax.experimental import pallas as pl
from jax.experimental.pallas import tpu as pltpu

HEAD_DIM = 64
N_DIFF_HEADS = 4
N_NSA_HEADS = 8
N_NSA_KV_HEADS = 2
NSA_GROUP = N_NSA_HEADS // N_NSA_KV_HEADS
CMP_BLOCK = 32
CMP_STRIDE = 16
CMP_HIDDEN = 128
SEL_BLOCK = 64
SEL_TOP_N = 16
WINDOW = 512
ROPE_THETA = 10000.0
NORM_EPS = 1e-6
DIFF_WIDTH = N_DIFF_HEADS * 2 * HEAD_DIM
NSA_WIDTH = N_NSA_HEADS * HEAD_DIM
N_GATES = 3 * N_NSA_HEADS
NEG = -1e9
BIG = 1e9
SCALE = HEAD_DIM ** -0.5

LANES = 128
SUBLANES = 8
SEL_LANES = 64
SOFTMAX_ROWS = 32
PROJ_ROWS = 512
DIFF_TILE = 512
NSA_QUERIES = LANES
NSA_KEYS = 4 * LANES
NSA_TILES = 2
SAMPLE_GROUP = 2
VMEM_LIMIT = 56 * 1024 * 1024
DIFF_SLABS = 2 * N_DIFF_HEADS
QKV_WIDTH = 3 * DIFF_WIDTH + NSA_WIDTH
KV_SLOTS = 6

F32 = jnp.float32
BF16 = jnp.bfloat16

_HEAD_PERM = tuple(g + NSA_GROUP * half for g in range(NSA_GROUP) for half in range(2))


def _dot(a, b):
    return jnp.dot(a, b, preferred_element_type=F32)


def _dot_nt(a, b):
    return lax.dot_general(a, b, (((1,), (1,)), ((), ())), preferred_element_type=F32)


def _rms(x):
    return x * lax.rsqrt(jnp.mean(x * x, axis=-1, keepdims=True) + NORM_EPS)


def _params(*sem):
    return pltpu.CompilerParams(dimension_semantics=sem, vmem_limit_bytes=VMEM_LIMIT)


def _const_spec(shape):
    nd = len(shape)
    return pl.BlockSpec(shape, lambda *_: (0,) * nd)


def _pad_rows(x, rows):
    return jnp.concatenate([x, jnp.zeros((rows - x.shape[0], x.shape[1]), x.dtype)], axis=0)


def _rotary(t, cos, sin, axis):
    idx = lax.broadcasted_iota(jnp.int32, t.shape, axis)
    first_half = (idx % HEAD_DIM) < HEAD_DIM // 2
    partner = jnp.where(first_half, pltpu.roll(t, LANES - HEAD_DIM // 2, axis), pltpu.roll(t, HEAD_DIM // 2, axis))
    return t * cos + partner * sin


def _inproj_kernel(*refs, kv_transposed):
    if kv_transposed:
        (x_ref, g_ref, w_ref, wkv_ref, wg_ref, cos_ref, sin_ref, cost_ref, sint_ref,
         qd_ref, kvd_ref, kvdb_ref, qn_ref, nsa_ref, win_ref, kvb_ref, gate_ref) = refs
    else:
        (x_ref, g_ref, w_ref, wkv_ref, wg_ref, cos_ref, sin_ref,
         qd_ref, kvd_ref, qn_ref, nsa_ref, win_ref, gate_ref) = refs
    tm = x_ref.shape[0]
    hb = (_rms(x_ref[...]) * g_ref[...]).astype(BF16)
    cos = cos_ref[...]
    sin = sin_ref[...]

    groups = [_dot(hb, w_ref[:, c0:c0 + DIFF_WIDTH]) for c0 in range(0, QKV_WIDTH, DIFF_WIDTH)]

    def proj(c0):
        return groups[c0 // DIFF_WIDTH][:, c0 % DIFF_WIDTH:c0 % DIFF_WIDTH + LANES]

    for j in range(N_DIFF_HEADS):
        qd_ref[:, j * LANES:(j + 1) * LANES] = (_rotary(proj(j * LANES), cos, sin, 1) * SCALE).astype(BF16)
        k = _rotary(proj(DIFF_WIDTH + j * LANES), cos, sin, 1)
        v = proj(2 * DIFF_WIDTH + j * LANES)
        kvd_ref[pl.ds(j, tm, stride=DIFF_SLABS), :] = k
        kvd_ref[pl.ds(N_DIFF_HEADS + j, tm, stride=DIFF_SLABS), :] = v
        if kv_transposed:
            kvdb_ref[:, j * LANES:(j + 1) * LANES] = k.astype(BF16)
            kvdb_ref[:, DIFF_WIDTH + j * LANES:DIFF_WIDTH + (j + 1) * LANES] = v.astype(BF16)
    for j in range(NSA_WIDTH // LANES):
        qn_ref[:, j * LANES:(j + 1) * LANES] = (_rotary(proj(3 * DIFF_WIDTH + j * LANES), cos, sin, 1)
                                                * SCALE).astype(BF16)
    gate_ref[...] = jax.nn.sigmoid(_dot(hb, wg_ref[...]))

    if kv_transposed:
        t_all = _dot_nt(wkv_ref[...], hb)
        cost = cost_ref[...]
        sint = sint_ref[...]
        for slot in range(KV_SLOTS):
            t = t_all[slot * LANES:(slot + 1) * LANES]
            if slot % 2 == 0:
                t = _rotary(t, cost, sint, 0)
            if slot < 4:
                nsa_ref[0, slot * LANES:(slot + 1) * LANES, :] = t
            else:
                win_ref[0, (slot - 4) * LANES:(slot - 3) * LANES, :] = t
            if slot >= 2:
                kvb_ref[0, (slot - 2) * LANES:(slot - 1) * LANES, :] = t.astype(BF16)
    else:
        t_all = _dot(hb, wkv_ref[...])
        for slot in range(KV_SLOTS):
            t = t_all[:, slot * LANES:(slot + 1) * LANES]
            if slot % 2 == 0:
                t = _rotary(t, cos, sin, 1)
            if slot < 4:
                nsa_ref[:, slot * LANES:(slot + 1) * LANES] = t
            else:
                win_ref[:, (slot - 4) * LANES:(slot - 3) * LANES] = t


def _inproj(x2d, g, w_qkv, w_kv, w_gate, tabs, tm, n_tab_blocks, batch, kv_transposed):
    rows, d = x2d.shape
    row_spec = lambda n: pl.BlockSpec((tm, n), lambda i: (i, 0))
    tab_spec = pl.BlockSpec((tm, LANES), lambda i: (i % n_tab_blocks, 0))
    in_specs = [row_spec(d), _const_spec((1, d)), _const_spec(w_qkv.shape), _const_spec(w_kv.shape),
                _const_spec(w_gate.shape), tab_spec, tab_spec]
    sds = jax.ShapeDtypeStruct
    interleaved = (sds((rows * DIFF_SLABS, LANES), F32), pl.BlockSpec((tm * DIFF_SLABS, LANES), lambda i: (i, 0)))
    if kv_transposed:
        seq = rows // batch
        nt = seq // tm
        tabt_spec = pl.BlockSpec((LANES, tm), lambda i: (0, i % nt))
        in_specs += [tabt_spec, tabt_spec]
        tspec = lambda n: pl.BlockSpec((1, n, tm), lambda i: (i // nt, 0, i % nt))
        outs = [
            (sds((rows, DIFF_WIDTH), BF16), row_spec(DIFF_WIDTH)),
            interleaved,
            (sds((rows, 2 * DIFF_WIDTH), BF16), row_spec(2 * DIFF_WIDTH)),
            (sds((rows, NSA_WIDTH), BF16), row_spec(NSA_WIDTH)),
            (sds((batch, 4 * LANES, seq), F32), tspec(4 * LANES)),
            (sds((batch, 2 * LANES, seq), F32), tspec(2 * LANES)),
            (sds((batch, 4 * LANES, seq), BF16), tspec(4 * LANES)),
            (sds((rows, LANES), F32), row_spec(LANES)),
        ]
    else:
        outs = [
            (sds((rows, DIFF_WIDTH), BF16), row_spec(DIFF_WIDTH)),
            interleaved,
            (sds((rows, NSA_WIDTH), BF16), row_spec(NSA_WIDTH)),
            (sds((rows, 4 * LANES), F32), row_spec(4 * LANES)),
            (sds((rows, 2 * LANES), F32), row_spec(2 * LANES)),
            (sds((rows, LANES), F32), row_spec(LANES)),
        ]
    return pl.pallas_call(
        functools.partial(_inproj_kernel, kv_transposed=kv_transposed),
        grid=(rows // tm,),
        in_specs=in_specs,
        out_specs=tuple(o[1] for o in outs),
        out_shape=tuple(o[0] for o in outs),
        compiler_params=_params("parallel"),
        name="inproj",
    )(x2d, g, w_qkv, w_kv, w_gate, *tabs)


def _lambda(lq1, lk1, lq2, lk2, lam_init):
    return (jnp.exp(jnp.sum(lq1[...] * lk1[...], axis=-1, keepdims=True))
            - jnp.exp(jnp.sum(lq2[...] * lk2[...], axis=-1, keepdims=True)) + lam_init)


def _split_components(q):
    lane = lax.broadcasted_iota(jnp.int32, q.shape, 1)
    low = lane < HEAD_DIM
    return jnp.concatenate([jnp.where(low, q, 0.0), jnp.where(low, 0.0, q)], axis=0).astype(BF16)


def _lane_tile(x, width):
    return jnp.concatenate([x] * (width // LANES), axis=1)


def _with_ones(v, axis):
    return jnp.concatenate([v, jnp.ones(v.shape, v.dtype)], axis=axis)


def _softmax_chunks(s_ref, p_ref, m_ref, a_ref, n_rows, adjust, unrolled):
    keys = s_ref.shape[1]

    def new_max(r0):
        rows = pl.ds(r0, SOFTMAX_ROWS)
        m_prev = m_ref[rows, :]
        m_new = jnp.maximum(m_prev, jnp.max(adjust(s_ref[rows, :], r0), axis=1, keepdims=True))
        if a_ref is not None:
            a_ref[rows, :] = jnp.exp(m_prev - m_new)
        m_ref[rows, :] = m_new

    def exponentiate(r0):
        rows = pl.ds(r0, SOFTMAX_ROWS)
        p_ref[rows, :] = jnp.exp(adjust(s_ref[rows, :], r0) - _lane_tile(m_ref[rows, :], keys)).astype(BF16)

    for one_pass in (new_max, exponentiate):
        if unrolled:
            for c in range(n_rows // SOFTMAX_ROWS):
                one_pass(c * SOFTMAX_ROWS)
        else:
            def body(c, carry, one_pass=one_pass):
                one_pass(pl.multiple_of(c * SOFTMAX_ROWS, SOFTMAX_ROWS))
                return carry
            lax.fori_loop(0, n_rows // SOFTMAX_ROWS, body, 0)


def _diff_prompt_kernel(qi_ref, ki_ref, q_ref, k_ref, v_ref, lq1, lk1, lq2, lk2, g_ref, o_ref,
                        m_ref, a_ref, acc_ref, s_ref, p_ref, *, tile, lam_init):
    t = pl.program_id(1)
    qi = qi_ref[t]
    ki = ki_ref[t]
    n_rows = 2 * tile

    @pl.when(ki == 0)
    def _init():
        m_ref[...] = jnp.full(m_ref.shape, NEG, F32)
        acc_ref[...] = jnp.zeros(acc_ref.shape, F32)

    def causal(s, r0):
        row = r0 % tile + lax.broadcasted_iota(jnp.int32, s.shape, 0)
        return jnp.where(lax.broadcasted_iota(jnp.int32, s.shape, 1) <= row, s, NEG)

    def update(diagonal):
        for h in range(N_DIFF_HEADS):
            buf = h % 2
            q2 = _split_components(q_ref[:, h * LANES:(h + 1) * LANES].astype(F32))
            s_ref[buf] = _dot_nt(q2, k_ref[:, h * LANES:(h + 1) * LANES])
            _softmax_chunks(s_ref.at[buf], p_ref.at[buf], m_ref.at[h], a_ref.at[buf], n_rows,
                            causal if diagonal else (lambda s, r0: s), unrolled=True)
            acc_ref[h] = (_lane_tile(a_ref[buf], 2 * LANES) * acc_ref[h]
                          + _dot(p_ref[buf], _with_ones(v_ref[:, h * LANES:(h + 1) * LANES], 1)))

    @pl.when(ki < qi)
    def _below_diagonal():
        update(False)

    @pl.when(ki == qi)
    def _finish():
        update(True)
        lam = _lambda(lq1, lk1, lq2, lk2, lam_init)
        for h in range(N_DIFF_HEADS):
            o = acc_ref[h, :, 0:LANES] / acc_ref[h, :, LANES:2 * LANES]
            a = o[0:tile] - lam * o[tile:2 * tile]
            o_ref[:, h * LANES:(h + 1) * LANES] = (_rms(a) * g_ref[...] * (1.0 - lam_init)).astype(BF16)


def _diff_prompt(qd, kvdb, lams, subln, b, s, tile, lam_init):
    nq = s // tile
    pairs = [(qi, ki) for qi in range(nq) for ki in range(qi + 1)]
    qi_arr = jnp.asarray([p[0] for p in pairs], jnp.int32)
    ki_arr = jnp.asarray([p[1] for p in pairs], jnp.int32)
    grid_spec = pltpu.PrefetchScalarGridSpec(
        num_scalar_prefetch=2,
        grid=(b, len(pairs)),
        in_specs=[
            pl.BlockSpec((tile, DIFF_WIDTH), lambda bi, t, qa, ka: (bi * nq + qa[t], 0)),
            pl.BlockSpec((tile, DIFF_WIDTH), lambda bi, t, qa, ka: (bi * nq + ka[t], 0)),
            pl.BlockSpec((tile, DIFF_WIDTH), lambda bi, t, qa, ka: (bi * nq + ka[t], 1)),
        ] + [_const_spec((1, HEAD_DIM))] * 4 + [_const_spec((1, 2 * HEAD_DIM))],
        out_specs=pl.BlockSpec((tile, DIFF_WIDTH), lambda bi, t, qa, ka: (bi * nq + qa[t], 0)),
        scratch_shapes=[pltpu.VMEM((N_DIFF_HEADS, 2 * tile, LANES), F32),
                        pltpu.VMEM((2, 2 * tile, LANES), F32),
                        pltpu.VMEM((N_DIFF_HEADS, 2 * tile, 2 * LANES), F32),
                        pltpu.VMEM((2, 2 * tile, tile), F32),
                        pltpu.VMEM((2, 2 * tile, tile), BF16)],
    )
    return pl.pallas_call(
        functools.partial(_diff_prompt_kernel, tile=tile, lam_init=lam_init),
        grid_spec=grid_spec,
        out_shape=jax.ShapeDtypeStruct((b * s, DIFF_WIDTH), BF16),
        compiler_params=_params("parallel", "arbitrary"),
        name="diff_prompt",
    )(qi_arr, ki_arr, qd, kvdb, kvdb, *lams, subln)


def _diff_sample_kernel(pt_ref, *refs, n_pages, group, **static):
    page_refs = refs[:group * n_pages]
    q_ref, new_ref, lq1, lk1, lq2, lk2, g_ref, o_ref, kv_ref = refs[group * n_pages:]
    bodies = [_diff_sample_one(u, page_refs[u * n_pages:(u + 1) * n_pages], q_ref, new_ref, lq1, lk1, lq2, lk2, g_ref,
                               o_ref, kv_ref, n_pages=n_pages, **static) for u in range(group)]
    for _ in itertools.zip_longest(*bodies):
        pass


def _diff_sample_one(u, page_refs, q_ref, new_ref, lq1, lk1, lq2, lk2, g_ref, o_ref, kv_ref, *, n_pages, page, past, sq,
                     lam_init):
    n_cols = 2 * (n_pages + 1) * page
    n_rows = N_DIFF_HEADS * 2 * sq
    col = lax.broadcasted_iota(jnp.int32, (n_rows, n_cols), 1)
    tpos = past + lax.broadcasted_iota(jnp.int32, (n_rows, n_cols), 0) % sq
    visible = (col % 2 == 0) & (col // 2 <= tpos)
    lam = _lambda(lq1, lk1, lq2, lk2, lam_init)
    new0 = u * sq * DIFF_SLABS

    scores = []
    for h in range(N_DIFF_HEADS):
        kv = kv_ref.at[u * N_DIFF_HEADS + h]
        for i, r in enumerate(page_refs):
            kv[i * 2 * page:(i + 1) * 2 * page, :] = r[pl.ds(h, 2 * page, stride=N_DIFF_HEADS), :].astype(BF16)
        kv[n_pages * 2 * page:n_cols, :] = _pad_rows(new_ref[pl.ds(new0 + h, 2 * sq, stride=N_DIFF_HEADS), :],
                                                     2 * page).astype(BF16)
        q2 = _split_components(q_ref[u, :, h * LANES:(h + 1) * LANES].astype(F32))
        scores.append(_dot_nt(q2, kv[...]))
        yield
    s = jnp.where(visible, jnp.concatenate(scores, axis=0), NEG)
    m = jnp.max(s, axis=1, keepdims=True)
    p = jnp.where(visible, jnp.exp(s - m), 0.0)
    l = jnp.maximum(jnp.sum(p, axis=1, keepdims=True), 1e-30)
    p = pltpu.roll(p, 1, 1).astype(BF16)
    yield
    for h in range(N_DIFF_HEADS):
        rows = slice(h * 2 * sq, (h + 1) * 2 * sq)
        o = _dot(p[rows], kv_ref[u * N_DIFF_HEADS + h]) / l[rows]
        a = o[0:sq] - lam * o[sq:2 * sq]
        o_ref[u, :, h * LANES:(h + 1) * LANES] = (_rms(a) * g_ref[...] * (1.0 - lam_init)).astype(BF16)
        yield


def _diff_sample(cache_rows, page_table, qd3, kvd_rows, lams, subln, page, lam_init):
    db, sq, _ = qd3.shape
    n_pages = page_table.shape[1]
    past = n_pages * page
    group = SAMPLE_GROUP if db % SAMPLE_GROUP == 0 else 1
    page_specs = [pl.BlockSpec((page * DIFF_SLABS, LANES),
                               functools.partial(lambda bi, pt, u, p: (pt[bi * group + u, p], 0), u=u, p=p))
                  for u in range(group) for p in range(n_pages)]
    grid_spec = pltpu.PrefetchScalarGridSpec(
        num_scalar_prefetch=1,
        grid=(db // group,),
        in_specs=page_specs + [
            pl.BlockSpec((group, sq, DIFF_WIDTH), lambda bi, pt: (bi, 0, 0)),
            pl.BlockSpec((group * sq * DIFF_SLABS, LANES), lambda bi, pt: (bi, 0)),
        ] + [_const_spec((1, HEAD_DIM))] * 4 + [_const_spec((1, 2 * HEAD_DIM))],
        out_specs=pl.BlockSpec((group, sq, DIFF_WIDTH), lambda bi, pt: (bi, 0, 0)),
        scratch_shapes=[pltpu.VMEM((group * N_DIFF_HEADS, 2 * (n_pages + 1) * page, LANES), BF16)],
    )
    return pl.pallas_call(
        functools.partial(_diff_sample_kernel, n_pages=n_pages, group=group, page=page, past=past, sq=sq,
                          lam_init=lam_init),
        grid_spec=grid_spec,
        out_shape=jax.ShapeDtypeStruct((db, sq, DIFF_WIDTH), BF16),
        compiler_params=_params("parallel"),
        name="diff_sample",
    )(page_table, *([cache_rows] * (group * n_pages)), qd3, kvd_rows, *lams, subln)


def _chunk_pitch(n_chunks):
    assert n_chunks % (2 * SUBLANES) == 0
    return n_chunks + SUBLANES


def _store_token_rows(rows_ref, slot, first_chunk, rows, n_chunks):
    for c in range(rows.shape[0] // CMP_STRIDE):
        rows_ref[slot, pl.ds(first_chunk + c, CMP_STRIDE, stride=_chunk_pitch(n_chunks)), :] = (
            rows[c * CMP_STRIDE:(c + 1) * CMP_STRIDE])


def _compress(rows_ref, slot, n_chunks, pos_a, pos_b, w_a, w_b, w2):
    pitch = _chunk_pitch(n_chunks)
    xs = jnp.concatenate([rows_ref[slot, j * pitch:j * pitch + n_chunks, :] for j in range(CMP_STRIDE)],
                         axis=1)
    a = _dot((xs + pos_a).astype(BF16), w_a)
    b = _dot((xs + pos_b).astype(BF16), w_b)
    hidden = jax.nn.gelu(a + pltpu.roll(b, n_chunks - 1, 0))
    out = _dot(hidden.astype(BF16), w2)
    rid = lax.broadcasted_iota(jnp.int32, out.shape, 0)
    return jnp.where(rid < n_chunks - 1, out, 0.0)


def _compress_prompt_kernel(x_ref, pa_ref, pb_ref, wa_ref, wb_ref, w2_ref, o_ref, rows_ref):
    seq = x_ref.shape[2]
    for slot in range(2):
        for c in range(seq // LANES):
            _store_token_rows(rows_ref, slot, c * (LANES // CMP_STRIDE),
                              x_ref[0, slot * LANES:(slot + 1) * LANES, c * LANES:(c + 1) * LANES].T,
                              seq // CMP_STRIDE)
        o_ref[0, slot] = _compress(rows_ref, slot, seq // CMP_STRIDE, pa_ref[slot], pb_ref[slot], wa_ref[slot],
                                   wb_ref[slot], w2_ref[slot]).astype(BF16)


def _compress_prompt(nsa_t, cw):
    b, _, seq = nsa_t.shape
    nch = seq // CMP_STRIDE
    return pl.pallas_call(
        _compress_prompt_kernel,
        grid=(b,),
        in_specs=[pl.BlockSpec((1, 2 * LANES, seq), lambda i: (i, 0, 0))] + [_const_spec(a.shape) for a in cw],
        out_specs=pl.BlockSpec((1, 2, nch, LANES), lambda i: (i, 0, 0, 0)),
        out_shape=jax.ShapeDtypeStruct((b, 2, nch, LANES), BF16),
        scratch_shapes=[pltpu.VMEM((2, CMP_STRIDE * _chunk_pitch(nch), LANES), F32)],
        compiler_params=_params("parallel"),
        name="compress_prompt",
    )(nsa_t, *cw)


def _cmp_probs(qs, kc, tpos):
    s = _dot_nt(qs, kc)
    cend = lax.broadcasted_iota(jnp.int32, s.shape, 1) * CMP_STRIDE + (CMP_BLOCK - 1)
    vis = cend <= tpos
    s = jnp.where(vis, s, NEG)
    m = jnp.max(s, axis=1, keepdims=True)
    p = jnp.where(vis, jnp.exp(s - m), 0.0)
    return (p / jnp.maximum(jnp.sum(p, axis=1, keepdims=True), 1e-30)).astype(BF16)


def _block_scores(imp, blk, cur):
    forced = (blk == 0) | (blk == cur) | (blk == cur - 1)
    return jnp.where(blk > cur, NEG, jnp.where(forced, BIG, imp))


def _select_blocks(imp, tpos2):
    blk = lax.broadcasted_iota(jnp.int32, imp.shape, 1)
    score = _block_scores(imp, blk, tpos2 // SEL_BLOCK)
    rank = jnp.zeros(imp.shape, F32)
    for j in range(SEL_LANES):
        sj = score[:, j:j + 1]
        tie = jnp.where(blk > j, 1.0, 0.0)
        rank = rank + jnp.where(sj > score, 1.0, 0.0) + jnp.where(sj == score, tie, 0.0)
    return jnp.where(rank < SEL_TOP_N, 1.0, 0.0).astype(BF16)


def _select_blocks_t(imp_t, tpos_t):
    n = imp_t.shape[1]
    blk = lax.broadcasted_iota(jnp.int32, imp_t.shape, 0)
    score = _block_scores(imp_t, blk, tpos_t // SEL_BLOCK)
    n_groups = SEL_LANES // SUBLANES
    groups = [score[g * SUBLANES:(g + 1) * SUBLANES] for g in range(n_groups)]
    ranks = [jnp.zeros((SUBLANES, n), F32) for _ in range(n_groups)]
    rid = lax.broadcasted_iota(jnp.int32, (SUBLANES, n), 0)
    for j in range(SEL_LANES):
        sj = score[j:j + 1]
        for g in range(n_groups):
            if g * SUBLANES > j:
                beats = jnp.where(sj >= groups[g], 1.0, 0.0)
            elif (g + 1) * SUBLANES - 1 < j:
                beats = jnp.where(sj > groups[g], 1.0, 0.0)
            else:
                beats = jnp.where(rid + g * SUBLANES > j, jnp.where(sj >= groups[g], 1.0, 0.0),
                                  jnp.where(sj > groups[g], 1.0, 0.0))
            ranks[g] = ranks[g] + beats
    sel_t = jnp.concatenate([jnp.where(r < SEL_TOP_N, 1.0, 0.0) for r in ranks]
                            + [jnp.zeros((LANES - SEL_LANES, n), F32)], axis=0)
    return sel_t.T


def _stack_queries(q, n_q):
    lane = lax.broadcasted_iota(jnp.int32, (n_q, LANES), 1)
    low = lane < HEAD_DIM
    groups = [q[:, g * LANES:(g + 1) * LANES].astype(F32) for g in range(NSA_GROUP)]
    return jnp.concatenate([jnp.where(low, t, 0.0) for t in groups] + [jnp.where(low, 0.0, t) for t in groups],
                           axis=0).astype(BF16)


def _combine(o_c, o_s, o_w, gates, n_q, store):
    lane = lax.broadcasted_iota(jnp.int32, (n_q, LANES), 1)
    low = lane < HEAD_DIM
    for g in range(NSA_GROUP):
        def pick(o):
            return jnp.where(low, o[g * n_q:(g + 1) * n_q], o[(NSA_GROUP + g) * n_q:(NSA_GROUP + g + 1) * n_q])

        def gate(j):
            c = j * N_NSA_HEADS + 2 * g
            return jnp.where(low, gates[:, c:c + 1], gates[:, c + 1:c + 2])

        store(g, gate(0) * pick(o_c) + gate(1) * pick(o_s) + gate(2) * pick(o_w))


def _per_kv_head(x, n_q):
    return x.reshape(N_NSA_KV_HEADS, NSA_GROUP, n_q, x.shape[-1])


def _in_turn(bodies):
    for _ in itertools.zip_longest(*bodies):
        pass


def _nsa_prompt_kernel(qn_ref, gate_ref, cmp_ref, kv_ref, ovl_ref, exp_ref, o_ref,
                       m_ref, a_ref, acc_ref, s_ref, p_ref, sw_ref, pw_ref, sc_ref, pc_ref, qse_ref, *, tq, tk, group):
    m_rows = N_NSA_HEADS * tq
    step_start = pl.program_id(1) * (group * tq)
    o_cmp = [None] * group

    def head(u):
        q_start = step_start + u * tq
        qse, m, sc, pc = qse_ref.at[u], m_ref.at[u], sc_ref.at[u], pc_ref.at[u]
        qse[:, 0:LANES] = _stack_queries(qn_ref[u * tq:(u + 1) * tq, :], tq)
        tpos2_t = q_start + lax.broadcasted_iota(jnp.int32, (1, N_NSA_KV_HEADS * tq), 1) % tq
        m[...] = jnp.full(m.shape, NEG, F32)
        acc_ref[u] = jnp.zeros(acc_ref.shape[1:], F32)

        m[2] = jnp.full(m.shape[1:], 0.5 * NEG, F32)
        sc[...] = _dot_nt(qse[:, 0:LANES], cmp_ref[0, 0])
        yield

        def block_ended(s, r0):
            qpos = q_start + r0 % tq + lax.broadcasted_iota(jnp.int32, s.shape, 0)
            cend = lax.broadcasted_iota(jnp.int32, s.shape, 1) * CMP_STRIDE + (CMP_BLOCK - 1)
            return jnp.where(cend <= qpos, s, NEG)

        _softmax_chunks(sc, pc, m.at[2], None, m_rows, block_ended, unrolled=True)
        yield
        o_c = _dot(pc[...], _with_ones(cmp_ref[0, 1], 1))
        o_cmp[u] = o_c[:, 0:LANES] / jnp.maximum(o_c[:, LANES:2 * LANES], 1e-30)
        imp_t = _dot_nt(ovl_ref[...], pc[...])
        imp_t = imp_t[0:SEL_LANES] / jnp.maximum(imp_t[SEL_LANES:SEL_LANES + 1], 1e-30)
        imp_t = jnp.concatenate(
            [(imp_t[:, (4 * h) * tq:(4 * h + 1) * tq] + imp_t[:, (4 * h + 1) * tq:(4 * h + 2) * tq])
             + (imp_t[:, (4 * h + 2) * tq:(4 * h + 3) * tq] + imp_t[:, (4 * h + 3) * tq:(4 * h + 4) * tq])
             for h in range(N_NSA_KV_HEADS)], axis=1)
        yield
        sel = _select_blocks_t(imp_t, tpos2_t)
        yield
        block_bias = jnp.where(sel > 0.5, 0.0, NEG).astype(BF16)
        for r in range(N_NSA_HEADS):
            h = r // NSA_GROUP
            qse[r * tq:(r + 1) * tq, LANES:2 * LANES] = block_bias[h * tq:(h + 1) * tq]

    def slc_tile(u, k0, diagonal):
        q_start = step_start + u * tq
        cols = pl.ds(k0, tk)
        s_ref[u] = _dot(qse_ref[u], jnp.concatenate([kv_ref[0, 0:LANES, cols], exp_ref[:, cols]], axis=0))
        yield

        def causal(s, r0):
            qpos = q_start + r0 % tq + lax.broadcasted_iota(jnp.int32, s.shape, 0)
            return jnp.where(k0 + lax.broadcasted_iota(jnp.int32, s.shape, 1) <= qpos, s, NEG)

        _softmax_chunks(s_ref.at[u], p_ref.at[u], m_ref.at[u].at[0], a_ref.at[u], m_rows,
                        causal if diagonal else (lambda s, r0: s), unrolled=True)
        yield
        acc_ref[u] = (_lane_tile(a_ref[u], 2 * LANES) * acc_ref[u]
                      + _dot_nt(p_ref[u], _with_ones(kv_ref[0, LANES:2 * LANES, cols], 0)))

    def tail(u):
        q_start = step_start + u * tq
        o_s = acc_ref[u, :, 0:LANES] / acc_ref[u, :, LANES:2 * LANES]
        span = WINDOW + tq
        w0 = pl.multiple_of(jnp.maximum(q_start - WINDOW, 0), tq)
        sw_ref[u] = _dot(qse_ref[u, :, 0:LANES], kv_ref[0, 2 * LANES:3 * LANES, pl.ds(w0, span)])
        yield

        def windowed(s, r0):
            qpos = q_start + r0 % tq + lax.broadcasted_iota(jnp.int32, s.shape, 0)
            dist = qpos - (w0 + lax.broadcasted_iota(jnp.int32, s.shape, 1))
            return jnp.where(lax.bitcast_convert_type(dist, jnp.uint32) < WINDOW, s, NEG)

        _softmax_chunks(sw_ref.at[u], pw_ref.at[u], m_ref.at[u].at[1], None, m_rows, windowed, unrolled=True)
        yield
        o_w = _dot_nt(pw_ref[u], _with_ones(kv_ref[0, 3 * LANES:4 * LANES, pl.ds(w0, span)], 0))
        o_w = o_w[:, 0:LANES] / o_w[:, LANES:2 * LANES]
        yield

        def store(g, val):
            o_ref[u * tq:(u + 1) * tq, g * LANES:(g + 1) * LANES] = val.astype(BF16)

        _combine(o_cmp[u], o_s, o_w, gate_ref[u * tq:(u + 1) * tq, :], tq, store)

    tiles = range(group)
    _in_turn([head(u) for u in tiles])

    def below_diagonal(kt, carry):
        _in_turn([slc_tile(u, pl.multiple_of(kt * tk, tk), False) for u in tiles])
        return carry

    last_tile = step_start // tk
    lax.fori_loop(0, last_tile, below_diagonal, 0)
    _in_turn([slc_tile(u, pl.multiple_of(last_tile * tk, tk), True) for u in tiles])
    _in_turn([tail(u) for u in tiles])


def _nsa_prompt(qn, gates, cmp_kv, kv_t, ovl_t, expand, b, s, tq, tk, group):
    assert tk % (group * tq) == 0 and s % (group * tq) == 0
    rows = group * tq
    nq = s // rows
    nch = cmp_kv.shape[2]
    m_rows = N_NSA_HEADS * tq
    tile = lambda *shape: pltpu.VMEM((group,) + shape, F32)
    tile_bf16 = lambda *shape: pltpu.VMEM((group,) + shape, BF16)
    return pl.pallas_call(
        functools.partial(_nsa_prompt_kernel, tq=tq, tk=tk, group=group),
        grid=(b, nq),
        in_specs=[
            pl.BlockSpec((rows, NSA_WIDTH), lambda bi, qi: (bi * nq + qi, 0)),
            pl.BlockSpec((rows, LANES), lambda bi, qi: (bi * nq + qi, 0)),
            pl.BlockSpec((1, 2, nch, LANES), lambda bi, qi: (bi, 0, 0, 0)),
            pl.BlockSpec((1, 4 * LANES, s), lambda bi, qi: (bi, 0, 0)),
            _const_spec(ovl_t.shape), _const_spec(expand.shape),
        ],
        out_specs=pl.BlockSpec((rows, NSA_WIDTH), lambda bi, qi: (bi * nq + qi, 0)),
        out_shape=jax.ShapeDtypeStruct((b * s, NSA_WIDTH), BF16),
        scratch_shapes=[tile(3, m_rows, LANES),
                        tile(m_rows, LANES),
                        tile(m_rows, 2 * LANES),
                        tile(m_rows, tk), tile_bf16(m_rows, tk),
                        tile(m_rows, WINDOW + tq), tile_bf16(m_rows, WINDOW + tq),
                        tile(m_rows, nch), tile_bf16(m_rows, nch),
                        tile_bf16(m_rows, 2 * LANES)],
        compiler_params=_params("parallel", "arbitrary"),
        name="nsa_prompt",
    )(qn, gates, cmp_kv, kv_t, ovl_t, expand)


def _nsa_sample_kernel(pt_ref, *refs, n_pages, group, **static):
    page_refs = refs[:group * n_pages]
    per_batch = refs[group * n_pages:group * n_pages + 5]
    consts = refs[group * n_pages + 5:-3]
    o_ref, wout_ref, rows_ref = refs[-3:]
    bodies = []
    for u in range(group):
        one = pl.ds(u, 1)
        bodies.append(_nsa_sample_one(page_refs[u * n_pages:(u + 1) * n_pages], *[r.at[one] for r in per_batch],
                                      *consts, o_ref.at[one], wout_ref.at[one], rows_ref.at[pl.ds(2 * u, 2)],
                                      n_pages=n_pages, **static))
    for _ in itertools.zip_longest(*bodies):
        pass


def _nsa_sample_one(page_refs, win_ref, nsanew_ref, winnew_ref, qn_ref, gate_ref, pa_ref, pb_ref, wa_ref, wb_ref,
                    w2_ref, ovl_ref, exp_ref, o_ref, wout_ref, rows_ref, *, n_pages, page, past, sq):
    m_rows = N_NSA_HEADS * sq
    qs = _stack_queries(qn_ref[0], sq)
    tpos = past + lax.broadcasted_iota(jnp.int32, (m_rows, 1), 0) % sq
    tpos2 = past + lax.broadcasted_iota(jnp.int32, (N_NSA_KV_HEADS * sq, 1), 0) % sq

    def page_t(i, slot):
        return page_refs[i][0, slot].reshape(N_NSA_KV_HEADS * HEAD_DIM, page)

    cmp_kv = []
    for slot in range(2):
        for i in range(n_pages):
            _store_token_rows(rows_ref, slot, i * (page // CMP_STRIDE), page_t(i, slot).T, past // CMP_STRIDE)
        yield
        cmp_kv.append(_compress(rows_ref, slot, past // CMP_STRIDE, pa_ref[slot], pb_ref[slot], wa_ref[slot],
                                wb_ref[slot], w2_ref[slot]).astype(BF16))
        yield
    p_c = _cmp_probs(qs, cmp_kv[0], tpos)
    o_c = _dot(p_c, cmp_kv[1])
    imp = _per_kv_head(_dot(p_c, ovl_ref[...]), sq)
    imp = ((imp[:, 0] + imp[:, 1]) + (imp[:, 2] + imp[:, 3])).reshape(N_NSA_KV_HEADS * sq, SEL_LANES)
    yield
    sel = _select_blocks(imp, tpos2)
    yield

    nsanew = _pad_rows(nsanew_ref[0], page)
    n_keys = (n_pages + 1) * page
    s = jnp.concatenate([_dot(qs, page_t(i, 2).astype(BF16)) for i in range(n_pages)]
                        + [_dot_nt(qs, nsanew[:, 2 * LANES:3 * LANES].astype(BF16))], axis=1)
    yield
    picked = _dot(sel, exp_ref[...])
    kpos = lax.broadcasted_iota(jnp.int32, (1, n_keys), 1)
    bias = jnp.where(jnp.where(kpos <= tpos2, picked, 0.0) > 0.5, 0.0, NEG)
    s = (_per_kv_head(s, sq) + bias.reshape(N_NSA_KV_HEADS, 1, sq, n_keys)).reshape(m_rows, n_keys)
    m = jnp.max(s, axis=1, keepdims=True)
    p = jnp.exp(s - m)
    l = jnp.sum(p, axis=1, keepdims=True)
    pb = p.astype(BF16)
    acc = _dot(pb[:, past:n_keys], nsanew[:, 3 * LANES:4 * LANES].astype(BF16))
    for i in range(n_pages):
        acc = acc + _dot_nt(pb[:, i * page:(i + 1) * page], page_t(i, 3).astype(BF16))
    o_s = acc / l
    yield

    winnew = _pad_rows(winnew_ref[0], page)
    s = jnp.concatenate([_dot(qs, win_ref[0, 0:LANES, :].astype(BF16)),
                         _dot_nt(qs, winnew[:, 0:LANES].astype(BF16))], axis=1)
    col = lax.broadcasted_iota(jnp.int32, (1, WINDOW + page), 1)
    dist = tpos - (past - WINDOW + col)
    vis = (dist >= 0) & (dist < WINDOW)
    s = jnp.where(vis, s, NEG)
    m = jnp.max(s, axis=1, keepdims=True)
    p = jnp.where(vis, jnp.exp(s - m), 0.0)
    pb = p.astype(BF16)
    o_w = ((_dot_nt(pb[:, 0:WINDOW], win_ref[0, LANES:2 * LANES, :].astype(BF16))
            + _dot(pb[:, WINDOW:WINDOW + page], winnew[:, LANES:2 * LANES].astype(BF16)))
           / jnp.maximum(jnp.sum(p, axis=1, keepdims=True), 1e-30))

    def store(g, val):
        o_ref[0, :, g * LANES:(g + 1) * LANES] = val.astype(BF16)

    yield
    _combine(o_c, o_s, o_w, gate_ref[0], sq, store)

    extended = jnp.concatenate([win_ref[0], winnew.T], axis=1)
    wout_ref[0] = extended[:, sq:sq + WINDOW]


def _nsa_sample(cache_t, page_table, win_t, nsanew3, winnew3, qn3, gates3, cw, ovl, expand):
    db, sq, _ = qn3.shape
    n_pages = page_table.shape[1]
    page = cache_t.shape[-1]
    past = n_pages * page
    group = SAMPLE_GROUP if db % SAMPLE_GROUP == 0 else 1
    page_specs = [pl.BlockSpec((1,) + cache_t.shape[1:],
                               functools.partial(lambda bi, pt, u, p: (pt[bi * group + u, p], 0, 0, 0, 0), u=u, p=p))
                  for u in range(group) for p in range(n_pages)]
    per_b = lambda shape: pl.BlockSpec((group,) + shape, lambda bi, pt: (bi, 0, 0))
    grid_spec = pltpu.PrefetchScalarGridSpec(
        num_scalar_prefetch=1,
        grid=(db // group,),
        in_specs=page_specs + [
            per_b((2 * LANES, WINDOW)), per_b((sq, 4 * LANES)), per_b((sq, 2 * LANES)), per_b((sq, NSA_WIDTH)),
            per_b((sq, LANES)),
        ] + [_const_spec(a.shape) for a in cw] + [_const_spec(ovl.shape), _const_spec(expand.shape)],
        out_specs=(per_b((sq, NSA_WIDTH)), per_b((2 * LANES, WINDOW))),
        scratch_shapes=[pltpu.VMEM((2 * group, CMP_STRIDE * _chunk_pitch(past // CMP_STRIDE), LANES), F32)],
    )
    return pl.pallas_call(
        functools.partial(_nsa_sample_kernel, n_pages=n_pages, group=group, page=page, past=past, sq=sq),
        grid_spec=grid_spec,
        out_shape=(jax.ShapeDtypeStruct((db, sq, NSA_WIDTH), BF16),
                   jax.ShapeDtypeStruct((db, 2 * LANES, WINDOW), F32)),
        compiler_params=_params("parallel"),
        name="nsa_sample",
    )(page_table, *([cache_t] * (group * n_pages)), win_t, nsanew3, winnew3, qn3, gates3, *cw, ovl, expand)


def _mlp_kernel(x_ref, od_ref, on_ref, wo_ref, wu_ref, wd_ref, g1_ref, g2_ref, g3_ref, y_ref, *, ff_chunk):
    mix = _dot(od_ref[...], wo_ref[0:DIFF_WIDTH, :]) + _dot(on_ref[...], wo_ref[DIFF_WIDTH:DIFF_WIDTH + NSA_WIDTH, :])
    x1 = x_ref[...] + _rms(mix) * g1_ref[...]
    hm = (_rms(x1) * g2_ref[...]).astype(BF16)
    d_ff = wu_ref.shape[1]
    ff = jnp.zeros(x1.shape, F32)
    for c in range(d_ff // ff_chunk):
        u = jnp.maximum(_dot(hm, wu_ref[:, c * ff_chunk:(c + 1) * ff_chunk]), 0.0)
        ff = ff + _dot((u * u).astype(BF16), wd_ref[c * ff_chunk:(c + 1) * ff_chunk, :])
    y_ref[...] = x1 + _rms(ff) * g3_ref[...]


def _mlp(x2d, od, on, w_out, w_up, w_down, g1, g2, g3, tm):
    rows, d = x2d.shape
    row_spec = lambda n: pl.BlockSpec((tm, n), lambda i: (i, 0))
    resident = lambda a: pl.BlockSpec(a.shape, lambda i: (0, 0), pipeline_mode=pl.Buffered(1))
    return pl.pallas_call(
        functools.partial(_mlp_kernel, ff_chunk=1024),
        grid=(rows // tm,),
        in_specs=[row_spec(d), row_spec(DIFF_WIDTH), row_spec(NSA_WIDTH), resident(w_out), resident(w_up),
                  resident(w_down), _const_spec((1, d)), _const_spec((1, d)), _const_spec((1, d))],
        out_specs=row_spec(d),
        out_shape=jax.ShapeDtypeStruct((rows, d), F32),
        compiler_params=_params("parallel"),
        name="mlp",
    )(x2d, od, on, w_out, w_up, w_down, g1, g2, g3)


def _rope_tables(pos, reps):
    half = HEAD_DIM // 2
    inv = ROPE_THETA ** (-jnp.arange(half, dtype=F32) / half)
    ang = pos.astype(F32)[:, None] * inv[None, :]
    cos, sin = jnp.cos(ang), jnp.sin(ang)
    cos_t = jnp.tile(cos, (reps, LANES // half))
    sin_t = jnp.tile(jnp.concatenate([-sin, sin], axis=1), (reps, LANES // HEAD_DIM))
    return cos_t, sin_t


def _compress_weights(cmp_pos, cmp_w1, cmp_w2):
    eye = jnp.eye(N_NSA_KV_HEADS, dtype=F32)
    w1 = cmp_w1.reshape(2, CMP_BLOCK, HEAD_DIM, CMP_HIDDEN)

    def expand_w1(w):
        t = jnp.einsum('sldf,hg->slhdgf', w, eye)
        return t.reshape(2, CMP_STRIDE * LANES, N_NSA_KV_HEADS * CMP_HIDDEN).astype(BF16)

    def expand_pos(p):
        return jnp.tile(p[:, :, None, :], (1, 1, N_NSA_KV_HEADS, 1)).reshape(2, 1, CMP_STRIDE * LANES)

    w2 = jnp.einsum('sfd,hg->shfgd', cmp_w2, eye).reshape(2, N_NSA_KV_HEADS * CMP_HIDDEN, LANES).astype(BF16)
    return (expand_pos(cmp_pos[:, :CMP_STRIDE]), expand_pos(cmp_pos[:, CMP_STRIDE:]),
            expand_w1(w1[:, :CMP_STRIDE]), expand_w1(w1[:, CMP_STRIDE:]), w2)


def _selection_constants(n_chunks, n_keys, expand_rows):
    n = np.arange(n_chunks)
    m = np.arange(SEL_LANES)
    cs, ss = n * CMP_STRIDE, m * SEL_BLOCK
    ovl = (cs[:, None] < ss[None, :] + SEL_BLOCK) & (cs[:, None] + CMP_BLOCK > ss[None, :]) & (n[:, None] < n_chunks - 1)
    expand = (np.arange(n_keys)[None, :] // SEL_BLOCK) == np.arange(expand_rows)[:, None]
    return ovl, jnp.asarray(expand, BF16)


def _row_tile(rows, want):
    t = min(rows, want)
    assert rows % t == 0
    return t


def kernel(x_prompt, x_sample, cache_diff_kv, cache_nsa_kv, state_nsa_win_kv, page_table, w_in, w_out, w_up, w_down,
           g_pre_mix, g_post_mix, g_pre_mlp, g_post_mlp, lam_q1, lam_k1, lam_q2, lam_k2, diff_subln, cmp_pos,
           cmp_w1, cmp_w2):
    depth = w_in.shape[0]
    assert depth == 1, "one layer: the sample group's paged caches are read in place"
    b, s, d = x_prompt.shape
    db, sq, _ = x_sample.shape
    n_phys, page = cache_diff_kv.shape[1:3]
    n_pages = page_table.shape[1]
    past = n_pages * page
    assert s % (4 * LANES) == 0 and s >= WINDOW + LANES and s <= SEL_LANES * SEL_BLOCK
    assert past >= WINDOW and sq < CMP_STRIDE and sq <= page and past + sq <= SEL_LANES * SEL_BLOCK
    assert state_nsa_win_kv.shape[2] == WINDOW and page == LANES
    layer = 0
    lam_init = 0.8 - 0.6 * math.exp(-0.3 * layer)

    wl = w_in[layer]
    qn0 = 3 * DIFF_WIDTH
    qn_cols = np.concatenate([qn0 + h * HEAD_DIM + np.arange(HEAD_DIM) for h in _HEAD_PERM])
    w_qkv = wl[:, np.concatenate([np.arange(qn0), qn_cols])].astype(BF16)
    w_kv = wl[:, QKV_WIDTH:QKV_WIDTH + KV_SLOTS * LANES].astype(BF16)
    gate0 = QKV_WIDTH + KV_SLOTS * LANES
    gate_cols = np.asarray([gate0 + h * 3 + j for j in range(3) for h in _HEAD_PERM])
    w_gate = jnp.pad(wl[:, gate_cols], ((0, 0), (0, LANES - N_GATES))).astype(BF16)
    out_rows = np.concatenate([np.arange(DIFF_WIDTH)] + [DIFF_WIDTH + h * HEAD_DIM + np.arange(HEAD_DIM) for h in _HEAD_PERM])
    wo = w_out[layer][out_rows].astype(BF16)
    wu = w_up[layer].astype(BF16)
    wd = w_down[layer].astype(BF16)
    vec = lambda a: a[layer].reshape(1, -1)
    lams = (vec(lam_q1), vec(lam_k1), vec(lam_q2), vec(lam_k2))
    cw = _compress_weights(cmp_pos[layer], cmp_w1[layer], cmp_w2[layer])

    rows_p = b * s
    tm_p = _row_tile(s, PROJ_ROWS)
    cos_p, sin_p = _rope_tables(jnp.arange(s), 1)
    xp = x_prompt.reshape(rows_p, d)
    qd, kvd, kvdb, qn, nsa_t, win_t, kvb_t, gates = _inproj(
        xp, vec(g_pre_mix), w_qkv, w_kv.T, w_gate, (cos_p, sin_p, cos_p.T, sin_p.T), tm_p, s // tm_p, b, True)
    od = _diff_prompt(qd, kvdb, lams, vec(diff_subln), b, s, _row_tile(s, DIFF_TILE), lam_init)
    cmp_kv = _compress_prompt(nsa_t, cw)
    ovl_p, exp_p = _selection_constants(s // CMP_STRIDE, s, LANES)
    ovl_rows = np.concatenate([ovl_p.T, np.ones((SUBLANES, ovl_p.shape[0])),
                               np.zeros((LANES - SEL_LANES - SUBLANES, ovl_p.shape[0]))], axis=0)
    on = _nsa_prompt(qn, gates, cmp_kv, kvb_t, jnp.asarray(ovl_rows, BF16), exp_p, b, s, NSA_QUERIES, NSA_KEYS,
                     NSA_TILES)
    yp = _mlp(xp, od, on, wo, wu, wd, vec(g_post_mix), vec(g_pre_mlp), vec(g_post_mlp), tm_p).reshape(b, s, d)
    p_diff = kvd.reshape(1, b, s, 2, N_DIFF_HEADS, 2 * HEAD_DIM)
    token_minor = lambda a, slots: jnp.transpose(
        a.reshape(a.shape[0], slots, N_NSA_KV_HEADS, HEAD_DIM, a.shape[-1]), (0, 4, 1, 2, 3))[None]
    p_nsa = token_minor(nsa_t, 4)
    p_win = token_minor(win_t[:, :, s - WINDOW:], 2)

    rows_s = db * sq
    tm_s = _row_tile(rows_s, PROJ_ROWS)
    assert tm_s % sq == 0
    cos_s, sin_s = _rope_tables(past + jnp.arange(sq), tm_s // sq)
    xs = x_sample.reshape(rows_s, d)
    qd, kvd, qn, nsa4, win2, gates = _inproj(xs, vec(g_pre_mix), w_qkv, w_kv, w_gate, (cos_s, sin_s), tm_s, 1, db,
                                             False)
    r3 = lambda a: a.reshape(db, sq, a.shape[1])
    diff_rows = cache_diff_kv[layer].reshape(n_phys * page * DIFF_SLABS, LANES)
    od = _diff_sample(diff_rows, page_table, r3(qd), kvd, lams, vec(diff_subln), page, lam_init)
    nsa_cache_t = jnp.transpose(cache_nsa_kv[layer], (0, 2, 3, 4, 1))
    win_state_t = jnp.transpose(state_nsa_win_kv[layer], (0, 2, 3, 4, 1)).reshape(db, 2 * LANES, WINDOW)
    ovl_s, exp_s = _selection_constants(past // CMP_STRIDE, past + page, SEL_LANES)
    on, win_new_t = _nsa_sample(nsa_cache_t, page_table, win_state_t, r3(nsa4), r3(win2), r3(qn), r3(gates), cw,
                                jnp.asarray(ovl_s, BF16), exp_s)
    ys = _mlp(xs, od.reshape(rows_s, DIFF_WIDTH), on.reshape(rows_s, NSA_WIDTH), wo, wu, wd, vec(g_post_mix),
              vec(g_pre_mlp), vec(g_post_mlp), tm_s).reshape(db, sq, d)
    s_diff = kvd.reshape(1, db, sq, 2, N_DIFF_HEADS, 2 * HEAD_DIM)
    s_nsa = nsa4.reshape(1, db, sq, 4, N_NSA_KV_HEADS, HEAD_DIM)
    s_win = token_minor(win_new_t, 2)

    return yp, ys, p_diff, p_nsa, p_win, s_diff, s_nsa, s_win
```
